```python
import math
import jax, jax.numpy as jnp
from jax import lax
import numpy as np

D_MODEL = 1024
BATCH = 1
SEQ = 16384
DEPTH = 2
DEC_BATCH = 4
DEC_SEQ = 4096
PAST_LEN = 128

N_META = 16
EPS = 1e-6
D_MIX = D_MODEL
RG_WIDTH = D_MIX // 2
RG_HEADS = 8
RG_BLOCK = RG_WIDTH // RG_HEADS
CONV_W = 4
RG_C = 8.0
N_HEADS = 8
QK_NOPE = 64
QK_ROPE = 32
QK_DIM = QK_NOPE + QK_ROPE
V_DIM = (D_MIX - RG_WIDTH) // N_HEADS
Q_LORA = 256
KV_LORA = 128
ROPE_THETA = 10000.0
Q_BLOCK = 128
P_IN = 2 * RG_WIDTH + Q_LORA + KV_LORA + QK_ROPE
PEER_HEADS = 8
N_KEYS = 128
N_EXPERTS = N_KEYS * N_KEYS
PEER_TOPK = 16
PEER_KEY_DIM = 256
PEER_HALF = PEER_KEY_DIM // 2
PEER_BLOCK = 256

kernel_name = "hybrid_rglru_mla_peer_encoder"


def rms_norm(x, g):
    xf = x.astype(jnp.float32)
    y = xf * lax.rsqrt(jnp.mean(xf * xf, axis=-1, keepdims=True) + EPS)
    return (y * g.astype(jnp.float32)).astype(x.dtype)


def rope_tables(length, dtype):
    pos = jnp.arange(length, dtype=jnp.float32)
    inv = ROPE_THETA ** (-jnp.arange(0, QK_ROPE, 2, dtype=jnp.float32) / QK_ROPE)
    ang = pos[:, None] * inv[None, :]
    return jnp.cos(ang).astype(dtype), jnp.sin(ang).astype(dtype)


def rope_tail(x, cos, sin):
    xn, xr = x[..., :QK_NOPE], x[..., QK_NOPE:]
    x1, x2 = xr[..., : QK_ROPE // 2], xr[..., QK_ROPE // 2:]
    c, s = cos[None, :, None, :], sin[None, :, None, :]
    return jnp.concatenate([xn, x1 * c - x2 * s, x2 * c + x1 * s], axis=-1)


def _lin_combine(e1, e2):
    a1, b1 = e1
    a2, b2 = e2
    return a1 * a2, a2 * b1 + b2


def rglru_group(xr, gate, conv_w, conv_b, wa, ba, wi, bi, lam):
    B, L, _ = xr.shape
    left = CONV_W // 2
    xc = lax.conv_general_dilated(
        xr, conv_w[:, None, :], window_strides=(1,),
        padding=[(left, CONV_W - 1 - left)],
        dimension_numbers=('NWC', 'WIO', 'NWC'),
        feature_group_count=RG_WIDTH) + conv_b
    xf = xc.astype(jnp.float32)
    xh = xf.reshape(B, L, RG_HEADS, RG_BLOCK)

    def direction(d, reverse):
        r = jax.nn.sigmoid(jnp.einsum('blhi,hij->blhj', xh, wa[d].astype(jnp.float32))
                           + ba[d].astype(jnp.float32)).reshape(B, L, RG_WIDTH)
        i = jax.nn.sigmoid(jnp.einsum('blhi,hij->blhj', xh, wi[d].astype(jnp.float32))
                           + bi[d].astype(jnp.float32)).reshape(B, L, RG_WIDTH)
        log_a = -RG_C * r * jax.nn.softplus(-lam[d].astype(jnp.float32))
        a = jnp.exp(log_a)
        b = jnp.sqrt(-jnp.expm1(2.0 * log_a)) * (i * xf)
        _, hs = lax.associative_scan(_lin_combine, (a, b), reverse=reverse, axis=1)
        return hs

    h = direction(0, False) + direction(1, True)
    y = h * jax.nn.gelu(gate.astype(jnp.float32))
    return y.astype(xr.dtype)


def bidir_attention(q, k, v):
    B, L, H, DK = q.shape
    front = (-N_META) % Q_BLOCK
    back = (-(front + L)) % Q_BLOCK
    qp = jnp.pad(q, ((0, 0), (front, back), (0, 0), (0, 0)))
    nb = qp.shape[1] // Q_BLOCK
    qb = qp.reshape(B, nb, Q_BLOCK, H, DK).transpose(1, 0, 2, 3, 4)
    scale = DK ** -0.5

    def block(qi):
        s = jnp.einsum('bqhd,bkhd->bhqk', qi, k, preferred_element_type=jnp.float32) * scale
        p = jax.nn.softmax(s, axis=-1).astype(v.dtype)
        return jnp.einsum('bhqk,bkhd->bqhd', p, v)

    o = lax.map(block, qb)
    o = o.transpose(1, 0, 2, 3, 4).reshape(B, nb * Q_BLOCK, H, v.shape[-1])
    return o[:, front:front + L]


def mla_group(qc, kvc, kpe, q_norm_g, w_uq, kv_norm_g, w_ukv, q_head_g, k_head_g, cos, sin):
    B, L, _ = qc.shape
    q = (rms_norm(qc, q_norm_g) @ w_uq).reshape(B, L, N_HEADS, QK_DIM)
    kv = (rms_norm(kvc, kv_norm_g) @ w_ukv).reshape(B, L, N_HEADS, QK_NOPE + V_DIM)
    k_nope, v = kv[..., :QK_NOPE], kv[..., QK_NOPE:]
    k = jnp.concatenate(
        [k_nope, jnp.broadcast_to(kpe[:, :, None, :], (B, L, N_HEADS, QK_ROPE))], axis=-1)
    q = rope_tail(rms_norm(q, q_head_g), cos, sin)
    k = rope_tail(rms_norm(k, k_head_g), cos, sin)
    return bidir_attention(q, k, v).reshape(B, L, N_HEADS * V_DIM)


def peer_ffn(x, w_q, sub_keys, u_tab, v_tab):
    B, L, D = x.shape
    T = B * L
    xt = x.reshape(T, D)
    q = (xt @ w_q).reshape(T, PEER_HEADS, 2, PEER_HALF)
    s = jnp.einsum('thsd,hsnd->thsn', q, sub_keys, preferred_element_type=jnp.float32)
    s1, i1 = lax.top_k(s[:, :, 0], PEER_TOPK)
    s2, i2 = lax.top_k(s[:, :, 1], PEER_TOPK)
    cand_s = (s1[..., :, None] + s2[..., None, :]).reshape(T, PEER_HEADS, PEER_TOPK * PEER_TOPK)
    cand_i = (i1[..., :, None] * N_KEYS + i2[..., None, :]).reshape(T, PEER_HEADS, PEER_TOPK * PEER_TOPK)
    top_s, pos = lax.top_k(cand_s, PEER_TOPK)
    idx = jnp.take_along_axis(cand_i, pos, axis=-1).reshape(T, PEER_HEADS * PEER_TOPK)
    gates = jax.nn.softmax(top_s, axis=-1).reshape(T, PEER_HEADS * PEER_TOPK).astype(x.dtype)
    pad = (-T) % PEER_BLOCK
    nb = (T + pad) // PEER_BLOCK
    xb = jnp.pad(xt, ((0, pad), (0, 0))).reshape(nb, PEER_BLOCK, D)
    ib = jnp.pad(idx, ((0, pad), (0, 0))).reshape(nb, PEER_BLOCK, PEER_HEADS * PEER_TOPK)
    gb = jnp.pad(gates, ((0, pad), (0, 0))).reshape(nb, PEER_BLOCK, PEER_HEADS * PEER_TOPK)

    def block(args):
        xi, ii, gi = args
        u = jnp.take(u_tab, ii, axis=0)
        hid = jax.nn.gelu(jnp.einsum('td,tkd->tk', xi, u))
        vv = jnp.take(v_tab, ii, axis=0)
        return jnp.einsum('tk,tkd->td', gi * hid, vv)

    out = lax.map(block, (xb, ib, gb)).reshape(nb * PEER_BLOCK, D)[:T]
    return out.reshape(B, L, D)


def encoder_layer(h, cos, sin, norm1_g, w_in, conv_w, conv_b, rg_wa, rg_ba, rg_wi, rg_bi, rg_lambda,
                  q_norm_g, w_uq, kv_norm_g, w_ukv, q_head_g, k_head_g, out_g_rg, out_g_attn, w_out,
                  norm2_g, peer_wq, peer_subkeys, peer_u, peer_v):
    xn = rms_norm(h, norm1_g)
    p = xn @ w_in
    o1 = RG_WIDTH
    o2 = 2 * RG_WIDTH
    o3 = o2 + Q_LORA
    o4 = o3 + KV_LORA
    xr, gate, qc, kvc, kpe = p[..., :o1], p[..., o1:o2], p[..., o2:o3], p[..., o3:o4], p[..., o4:]
    rg = rglru_group(xr, gate, conv_w, conv_b, rg_wa, rg_ba, rg_wi, rg_bi, rg_lambda)
    att = mla_group(qc, kvc, kpe, q_norm_g, w_uq, kv_norm_g, w_ukv, q_head_g, k_head_g, cos, sin)
    mix = jnp.concatenate([rms_norm(rg, out_g_rg), rms_norm(att, out_g_attn)], axis=-1)
    h = h + mix @ w_out
    h = h + peer_ffn(rms_norm(h, norm2_g), peer_wq, peer_subkeys, peer_u, peer_v)
    return h


def encode(x, meta_tokens, layers):
    B = x.shape[0]
    meta = jnp.broadcast_to(meta_tokens[None].astype(x.dtype), (B, N_META, D_MODEL))
    h = jnp.concatenate([meta, x], axis=1)
    cos, sin = rope_tables(h.shape[1], h.dtype)
    for i in range(DEPTH):
        h = encoder_layer(h, cos, sin, *[p[i] for p in layers])
    return h[:, N_META:]


def setup_inputs(seed: int = 0) -> dict:
    key = jax.random.key(seed)
    ks = jax.random.split(key, 32)

    def nrm(k, shape, scale):
        return jax.random.normal(k, shape, jnp.float32) * scale

    def gain(k, shape):
        return 1.0 + 0.05 * jax.random.normal(k, shape, jnp.float32)

    u = jax.random.uniform(ks[8], (DEPTH, 2, RG_WIDTH), jnp.float32, minval=0.9, maxval=0.999)
    a0 = u ** (1.0 / RG_C)
    lam = jnp.log(a0) - jnp.log1p(-a0)
    return {
        "x_prompt": nrm(ks[0], (BATCH, SEQ, D_MODEL), 1.0),
        "x_sample": nrm(ks[1], (DEC_BATCH, DEC_SEQ, D_MODEL), 1.0),
        "meta_tokens": nrm(ks[2], (N_META, D_MODEL), 1.0),
        "norm1_g": gain(ks[3], (DEPTH, D_MODEL)),
        "w_in": nrm(ks[4], (DEPTH, D_MODEL, P_IN), D_MODEL ** -0.5),
        "conv_w": nrm(ks[5], (DEPTH, CONV_W, RG_WIDTH), CONV_W ** -0.5),
        "conv_b": nrm(ks[6], (DEPTH, RG_WIDTH), 0.01),
        "rg_wa": nrm(ks[7], (DEPTH, 2, RG_HEADS, RG_BLOCK, RG_BLOCK), RG_BLOCK ** -0.5),
        "rg_ba": nrm(ks[9], (DEPTH, 2, RG_HEADS, RG_BLOCK), 0.01),
        "rg_wi": nrm(ks[10], (DEPTH, 2, RG_HEADS, RG_BLOCK, RG_BLOCK), RG_BLOCK ** -0.5),
        "rg_bi": nrm(ks[11], (DEPTH, 2, RG_HEADS, RG_BLOCK), 0.01),
        "rg_lambda": lam,
        "q_norm_g": gain(ks[12], (DEPTH, Q_LORA)),
        "w_uq": nrm(ks[13], (DEPTH, Q_LORA, N_HEADS * QK_DIM), Q_LORA ** -0.5),
        "kv_norm_g": gain(ks[14], (DEPTH, KV_LORA)),
        "w_ukv": nrm(ks[15], (DEPTH, KV_LORA, N_HEADS * (QK_NOPE + V_DIM)), KV_LORA ** -0.5),
        "q_head_g": gain(ks[16], (DEPTH, QK_DIM)),
        "k_head_g": gain(ks[17], (DEPTH, QK_DIM)),
        "out_g_rg": gain(ks[18], (DEPTH, RG_WIDTH)),
        "out_g_attn": gain(ks[19], (DEPTH, N_HEADS * V_DIM)),
        "w_out": nrm(ks[20], (DEPTH, D_MIX, D_MODEL), D_MIX ** -0.5),
        "norm2_g": gain(ks[21], (DEPTH, D_MODEL)),
        "peer_wq": nrm(ks[22], (DEPTH, D_MODEL, PEER_HEADS * PEER_KEY_DIM), D_MODEL ** -0.5),
        "peer_subkeys": nrm(ks[23], (DEPTH, PEER_HEADS, 2, N_KEYS, PEER_HALF), PEER_HALF ** -0.5),
        "peer_u": nrm(ks[24], (DEPTH, N_EXPERTS, D_MODEL), D_MODEL ** -0.5),
        "peer_v": nrm(ks[25], (DEPTH, N_EXPERTS, D_MODEL), PEER_HEADS ** -0.5),
    }


def reference(x_prompt, x_sample, meta_tokens, norm1_g, w_in, conv_w, conv_b, rg_wa, rg_ba, rg_wi, rg_bi,
              rg_lambda, q_norm_g, w_uq, kv_norm_g, w_ukv, q_head_g, k_head_g, out_g_rg, out_g_attn, w_out,
              norm2_g, peer_wq, peer_subkeys, peer_u, peer_v):
    layers = (norm1_g, w_in, conv_w, conv_b, rg_wa, rg_ba, rg_wi, rg_bi, rg_lambda,
              q_norm_g, w_uq, kv_norm_g, w_ukv, q_head_g, k_head_g, out_g_rg, out_g_attn, w_out,
              norm2_g, peer_wq, peer_subkeys, peer_u, peer_v)
    y_prompt = encode(x_prompt, meta_tokens, layers)
    y_sample = encode(x_sample, meta_tokens, layers)
    return (y_prompt, y_sample)
```

```python
import functools
import math

import jax
import jax.numpy as jnp
from jax import lax
from jax.experimental import pallas as pl
from jax.experimental.pallas import tpu as pltpu

F32 = jnp.float32
BF16 = jnp.bfloat16

D_MODEL = 1024
N_META = 16
EPS = 1e-6
RG_WIDTH = 512
RG_HEADS = 8
RG_BLOCK = 64
CONV_W = 4
RG_C = 8.0
N_HEADS = 8
QK_NOPE = 64
QK_ROPE = 32
QK_DIM = 96
V_DIM = 64
Q_LORA = 256
KV_LORA = 128
ROPE_THETA = 10000.0
P_IN = 2 * RG_WIDTH + Q_LORA + KV_LORA + QK_ROPE
PEER_HEADS = 8
N_KEYS = 128
N_EXPERTS = N_KEYS * N_KEYS
PEER_TOPK = 16
PEER_HALF = 128

LANES = 128
SUBLANES = 8
HEAD_PAD = LANES
P_IN_PAD = 12 * LANES
ROW_TILE = 256
KV_CHUNK = 512
PEER_TOKENS = 512
PEER_I1_BLOCK = 8
VMEM_LIMIT = 52 * 1024 * 1024
NEG_INF = float("-inf")
MASK_VALUE = -1e30
LOG2E = 1.4426950408889634


def _cparams(*sem):
    return pltpu.CompilerParams(dimension_semantics=sem, vmem_limit_bytes=VMEM_LIMIT)


def _rms(x, g):
    return x * lax.rsqrt(jnp.mean(x * x, axis=-1, keepdims=True) + EPS) * g


def _gelu_tanh(x):
    return x * (0.5 * (1.0 + jnp.tanh(0.7978845608028654 * (x + 0.044715 * (x * x * x)))))


def _sigmoid(x):
    return 1.0 / (1.0 + jnp.exp(-x))


def _neg_expm1(y):
    u = jnp.exp(y)
    um1 = u - 1.0
    tiny = um1 == 0.0
    r = jnp.where(tiny, y, um1 * y / jnp.where(tiny, 1.0, jnp.log(u)))
    return -jnp.where(um1 == -1.0, -1.0, r)


def _dot(a, b):
    return jnp.dot(a, b, preferred_element_type=F32)


def _dot_nt(a, b):
    return lax.dot_general(a, b, (((1,), (1,)), ((), ())), preferred_element_type=F32)


def _full(shape):
    n = len(shape)
    return pl.BlockSpec(shape, lambda *_: (0,) * n)


def _inproj_body(x_ref, cos_ref, sin_ref, g1_ref, win_ref, qg_ref, wq_ref, wqs_ref, kvg_ref,
                 wk_ref, ek_ref, eks_ref, wv_ref, gq_ref, gqs_ref, gk_ref, gks_ref,
                 xr_ref, gate_ref, q_ref, k_ref, v_ref):
    xn = _rms(x_ref[...], g1_ref[...])
    p = _dot(xn.astype(BF16), win_ref[...])
    xr_ref[...] = p[:, :RG_WIDTH]
    gate_ref[...] = p[:, RG_WIDTH:2 * RG_WIDTH]
    o2 = 2 * RG_WIDTH
    qcn = _rms(p[:, o2:o2 + Q_LORA], qg_ref[...]).astype(BF16)
    kvn = _rms(p[:, o2 + Q_LORA:o2 + Q_LORA + KV_LORA], kvg_ref[...]).astype(BF16)
    kpe = p[:, o2 + Q_LORA + KV_LORA:].astype(BF16)
    q = _dot(qcn, wq_ref[...])
    qs = _dot(qcn, wqs_ref[...])
    k = _dot(kvn, wk_ref[...]) + _dot(kpe, ek_ref[...])
    ks = _dot(kpe, eks_ref[...])
    v_ref[...] = _dot(kvn, wv_ref[...]).astype(BF16)
    c = cos_ref[...]
    s = sin_ref[...]
    qscale = QK_DIM ** -0.5 * LOG2E
    for h in range(N_HEADS):
        sl = slice(h * HEAD_PAD, (h + 1) * HEAD_PAD)
        qh = q[:, sl]
        rq = lax.rsqrt(jnp.sum(qh * qh, axis=-1, keepdims=True) * (1.0 / QK_DIM) + EPS)
        q_ref[:, sl] = ((qh * gq_ref[...] * c + qs[:, sl] * gqs_ref[...] * s) * (rq * qscale)).astype(BF16)
        kh = k[:, sl]
        rk = lax.rsqrt(jnp.sum(kh * kh, axis=-1, keepdims=True) * (1.0 / QK_DIM) + EPS)
        k_ref[:, sl] = ((kh * gk_ref[...] * c + ks[:, sl] * gks_ref[...] * s) * rk).astype(BF16)


def _inproj(h, cos_t, sin_t, w, *, rows, pos_tiles):
    tm = ROW_TILE
    row = lambda n: pl.BlockSpec((tm, n), lambda i: (i, 0))
    pos = pl.BlockSpec((tm, HEAD_PAD), lambda i: (i % pos_tiles, 0))
    wide = N_HEADS * HEAD_PAD
    return pl.pallas_call(
        _inproj_body,
        grid=(rows // tm,),
        in_specs=[row(D_MODEL), pos, pos, _full((1, D_MODEL)), _full((D_MODEL, P_IN_PAD)),
                  _full((1, Q_LORA)), _full((Q_LORA, wide)), _full((Q_LORA, wide)),
                  _full((1, KV_LORA)), _full((KV_LORA, wide)), _full((LANES, wide)), _full((LANES, wide)),
                  _full((KV_LORA, wide)),
                  _full((1, HEAD_PAD)), _full((1, HEAD_PAD)), _full((1, HEAD_PAD)), _full((1, HEAD_PAD))],
        out_specs=[row(RG_WIDTH), row(RG_WIDTH), row(wide), row(wide), row(wide)],
        out_shape=[jax.ShapeDtypeStruct((rows, RG_WIDTH), F32), jax.ShapeDtypeStruct((rows, RG_WIDTH), F32),
                   jax.ShapeDtypeStruct((rows, wide), BF16), jax.ShapeDtypeStruct((rows, wide), BF16),
                   jax.ShapeDtypeStruct((rows, wide), BF16)],
        compiler_params=_cparams("parallel"),
        name="inproj",
    )(h, cos_t, sin_t, w["g1"], w["win"], w["qg"], w["wq"], w["wqs"], w["kvg"], w["wk"], w["ek"], w["eks"],
      w["wv"], w["gq"], w["gqs"], w["gk"], w["gks"])


def _scan_body(xm_f, xp_f, xn_f, xm_b, xp_b, xn_b, cw_ref, cb_ref, wf_ref, wb_ref, bias_ref, lam_ref,
               hf_ref, hb_ref, ext_s, a_s, b_s, cf_s, cb_s, *, seq_len, chunk, n_chunks):
    i = pl.program_id(1)
    C = chunk
    groups = C // SUBLANES

    @pl.when(i == 0)
    def _():
        cf_s[...] = jnp.zeros_like(cf_s)
        cb_s[...] = jnp.zeros_like(cb_s)

    rows8 = lax.broadcasted_iota(jnp.int32, (SUBLANES, RG_WIDTH), 0)
    rows_c = lax.broadcasted_iota(jnp.int32, (C, RG_WIDTH), 0)
    rmod = rows_c & (SUBLANES - 1)

    def masked(x, pos):
        return jnp.where((pos >= 0) & (pos < seq_len), x, 0.0)

    def affine_terms(xm, xp, xn, j, w_ref, d):
        base = j * C
        ext_s[0:SUBLANES, :] = masked(xp[...], base - SUBLANES + rows8)
        ext_s[SUBLANES:SUBLANES + C, :] = masked(xm[...], base + rows_c)
        ext_s[SUBLANES + C:, :] = masked(xn[...], base + C + rows8)
        xc = cb_ref[...]
        for t in range(CONV_W):
            xc = xc + cw_ref[t:t + 1, :] * ext_s[SUBLANES - 2 + t:SUBLANES - 2 + t + C, :]
        gm = _dot(xc.astype(BF16), w_ref[...])
        r = _sigmoid(gm[:, :RG_WIDTH] + bias_ref[2 * d:2 * d + 1, :])
        ig = _sigmoid(gm[:, RG_WIDTH:] + bias_ref[2 * d + 1:2 * d + 2, :])
        z = -lam_ref[d:d + 1, :]
        softplus = jnp.maximum(z, 0.0) + jnp.log1p(jnp.exp(-jnp.abs(z)))
        log_a = (-RG_C) * r * softplus
        a = jnp.exp(log_a)
        b = jnp.sqrt(_neg_expm1(2.0 * log_a)) * (ig * xc)
        b = jnp.where(base + rows_c < seq_len, b, 0.0)
        return a, b

    a, b = affine_terms(xm_f, xp_f, xn_f, i, wf_ref, 0)
    for d in (1, 2, 4):
        m = rmod >= d
        a_sh = pltpu.roll(a, d, 0)
        b_sh = pltpu.roll(b, d, 0)
        b = jnp.where(m, a * b_sh + b, b)
        a = jnp.where(m, a * a_sh, a)
    a_s[...] = a
    b_s[...] = b

    def fwd_group(g, carry):
        r0 = pl.multiple_of(g * SUBLANES, SUBLANES)
        hrow = a_s[pl.ds(r0, SUBLANES), :] * carry + b_s[pl.ds(r0, SUBLANES), :]
        hf_ref[pl.ds(r0, SUBLANES), :] = hrow
        return jnp.broadcast_to(hrow[SUBLANES - 1:SUBLANES, :], (SUBLANES, RG_WIDTH))

    cf_s[...] = lax.fori_loop(0, groups, fwd_group, cf_s[...])

    a, b = affine_terms(xm_b, xp_b, xn_b, n_chunks - 1 - i, wb_ref, 1)
    for d in (1, 2, 4):
        m = rmod < SUBLANES - d
        a_sh = pltpu.roll(a, C - d, 0)
        b_sh = pltpu.roll(b, C - d, 0)
        b = jnp.where(m, a * b_sh + b, b)
        a = jnp.where(m, a * a_sh, a)
    a_s[...] = a
    b_s[...] = b

    def bwd_group(g, carry):
        r0 = pl.multiple_of((groups - 1 - g) * SUBLANES, SUBLANES)
        hrow = a_s[pl.ds(r0, SUBLANES), :] * carry + b_s[pl.ds(r0, SUBLANES), :]
        hb_ref[pl.ds(r0, SUBLANES), :] = hrow
        return jnp.broadcast_to(hrow[0:1, :], (SUBLANES, RG_WIDTH))

    cb_s[...] = lax.fori_loop(0, groups, bwd_group, cb_s[...])


def _scan(xr, w, *, batch, padded_len, seq_len):
    C = ROW_TILE
    n = padded_len // C
    per8 = C // SUBLANES
    last8 = batch * padded_len // SUBLANES - 1

    def chunk_f(b, i):
        return b * n + i

    def chunk_b(b, i):
        return b * n + (n - 1 - i)

    def specs(chunk_of):
        main = pl.BlockSpec((C, RG_WIDTH), lambda b, i: (chunk_of(b, i), 0))
        prev = pl.BlockSpec((SUBLANES, RG_WIDTH), lambda b, i: (jnp.maximum(chunk_of(b, i) * per8 - 1, 0), 0))
        nxt = pl.BlockSpec((SUBLANES, RG_WIDTH),
                           lambda b, i: (jnp.minimum((chunk_of(b, i) + 1) * per8, last8), 0))
        return [main, prev, nxt]

    rows = batch * padded_len
    return pl.pallas_call(
        functools.partial(_scan_body, seq_len=seq_len, chunk=C, n_chunks=n),
        grid=(batch, n),
        in_specs=specs(chunk_f) + specs(chunk_b) + [
            _full((CONV_W, RG_WIDTH)), _full((1, RG_WIDTH)),
            _full((RG_WIDTH, 2 * RG_WIDTH)), _full((RG_WIDTH, 2 * RG_WIDTH)),
            _full((4, RG_WIDTH)), _full((2, RG_WIDTH))],
        out_specs=[pl.BlockSpec((C, RG_WIDTH), lambda b, i: (chunk_f(b, i), 0)),
                   pl.BlockSpec((C, RG_WIDTH), lambda b, i: (chunk_b(b, i), 0))],
        out_shape=[jax.ShapeDtypeStruct((rows, RG_WIDTH), F32), jax.ShapeDtypeStruct((rows, RG_WIDTH), F32)],
        scratch_shapes=[pltpu.VMEM((C + 2 * SUBLANES, RG_WIDTH), F32), pltpu.VMEM((C, RG_WIDTH), F32),
                        pltpu.VMEM((C, RG_WIDTH), F32), pltpu.VMEM((SUBLANES, RG_WIDTH), F32),
                        pltpu.VMEM((SUBLANES, RG_WIDTH), F32)],
        compiler_params=_cparams("arbitrary", "arbitrary"),
        name="rglru_scan",
    )(xr, xr, xr, xr, xr, xr, w["conv_w"], w["conv_b"], w["wgate_f"], w["wgate_b"], w["gate_bias"], w["lam"])


def _attn_body(q_ref, kt_ref, v_ref, o_ref, *, seq_len, n_full, rem):
    q = q_ref[0]
    tq = q.shape[0]

    def step(start, width, carry, mask):
        m, l, acc = carry
        kt = kt_ref[0, 0, :, pl.ds(start, width)]
        s = _dot(q, kt)
        if mask:
            kpos = start + lax.broadcasted_iota(jnp.int32, (tq, width), 1)
            s = jnp.where(kpos < seq_len, s, MASK_VALUE)
        m_new = jnp.maximum(m, jnp.max(s, axis=-1, keepdims=True))
        alpha = jnp.exp2(m - m_new)
        p = jnp.exp2(s - m_new)
        l = alpha * l + jnp.sum(p, axis=-1, keepdims=True)
        acc = alpha * acc + _dot(p.astype(BF16), v_ref[0, pl.ds(start, width), :])
        return m_new, l, acc

    carry = (jnp.full((tq, 1), NEG_INF, F32), jnp.zeros((tq, 1), F32), jnp.zeros((tq, HEAD_PAD), F32))
    carry = lax.fori_loop(
        0, n_full, lambda c, cr: step(pl.multiple_of(c * KV_CHUNK, KV_CHUNK), KV_CHUNK, cr, False), carry)
    if rem:
        carry = step(n_full * KV_CHUNK, rem, carry, True)
    _, l, acc = carry
    o_ref[0] = acc / l


def _attention(q, kt, v, *, batch, padded_len, seq_len):
    tq = ROW_TILE
    n_full = seq_len // KV_CHUNK
    rem = -(-(seq_len - n_full * KV_CHUNK) // ROW_TILE) * ROW_TILE
    wide = N_HEADS * HEAD_PAD
    return pl.pallas_call(
        functools.partial(_attn_body, seq_len=seq_len, n_full=n_full, rem=rem),
        grid=(batch, N_HEADS, padded_len // tq),
        in_specs=[pl.BlockSpec((1, tq, HEAD_PAD), lambda b, h, i: (b, i, h)),
                  pl.BlockSpec((1, 1, HEAD_PAD, padded_len), lambda b, h, i: (b, h, 0, 0)),
                  pl.BlockSpec((1, padded_len, HEAD_PAD), lambda b, h, i: (b, 0, h))],
        out_specs=pl.BlockSpec((1, tq, HEAD_PAD), lambda b, h, i: (b, i, h)),
        out_shape=jax.ShapeDtypeStruct((batch, padded_len, wide), F32),
        compiler_params=_cparams("parallel", "parallel", "arbitrary"),
        name="attention",
    )(q, kt, v)


def _outproj_body(hf_ref, hb_ref, gate_ref, o_ref, h_ref, grg_ref, gat_ref, wrg_ref, wat_ref, g2_ref,
                  wqt_ref, sk_ref, hn_ref, xn2_ref, sc_ref):
    rg = (hf_ref[...] + hb_ref[...]) * _gelu_tanh(gate_ref[...])
    rgn = _rms(rg, grg_ref[...])
    o = o_ref[...]
    ms = jnp.sum(o * o, axis=-1, keepdims=True) * (1.0 / (N_HEADS * V_DIM))
    attn = o * lax.rsqrt(ms + EPS) * gat_ref[...]
    hn = h_ref[...] + _dot(rgn.astype(BF16), wrg_ref[...]) + _dot(attn.astype(BF16), wat_ref[...])
    hn_ref[...] = hn
    xn2 = _rms(hn, g2_ref[...]).astype(BF16)
    xn2_ref[...] = xn2
    qpt = _dot_nt(wqt_ref[...], xn2)
    for h in range(PEER_HEADS):
        for s in range(2):
            r0 = (h * 2 + s) * PEER_HALF
            blk = qpt[r0:r0 + PEER_HALF, :].astype(BF16)
            o0 = s * PEER_HEADS * N_KEYS + h * N_KEYS
            sc_ref[o0:o0 + N_KEYS, :] = _dot(sk_ref[h * 2 + s], blk)


def _outproj(hf, hb, gate, o, h, w, *, rows):
    tm = ROW_TILE
    row = lambda n: pl.BlockSpec((tm, n), lambda i: (i, 0))
    wide = N_HEADS * HEAD_PAD
    nsc = 2 * PEER_HEADS * N_KEYS
    return pl.pallas_call(
        _outproj_body,
        grid=(rows // tm,),
        in_specs=[row(RG_WIDTH), row(RG_WIDTH), row(RG_WIDTH), row(wide), row(D_MODEL),
                  _full((1, RG_WIDTH)), _full((1, wide)), _full((RG_WIDTH, D_MODEL)), _full((wide, D_MODEL)),
                  _full((1, D_MODEL)), _full((nsc, D_MODEL)), _full((2 * PEER_HEADS, N_KEYS, PEER_HALF))],
        out_specs=[row(D_MODEL), row(D_MODEL), pl.BlockSpec((nsc, tm), lambda i: (0, i))],
        out_shape=[jax.ShapeDtypeStruct((rows, D_MODEL), F32), jax.ShapeDtypeStruct((rows, D_MODEL), BF16),
                   jax.ShapeDtypeStruct((nsc, rows), F32)],
        compiler_params=_cparams("parallel"),
        name="outproj",
    )(hf, hb, gate, o, h, w["grg"], w["gat"], w["wout_rg"], w["wout_at"], w["g2"], w["wqt"], w["subk"])


def _top_rows(x, count):
    rows = []
    for r in range(count):
        m = jnp.max(x, axis=0, keepdims=True)
        rows.append(m)
        if r + 1 < count:
            x = jnp.where(x == m, NEG_INF, x)
    return rows


def _route_body(sc_ref, th_ref, e1_ref, e2_ref):
    t = sc_ref.shape[1]
    half = PEER_HEADS * N_KEYS
    jrow = lax.broadcasted_iota(jnp.int32, (SUBLANES, t), 0)

    def bc(row):
        return jnp.broadcast_to(row, (SUBLANES, t))

    def head(h, carry):
        r0 = pl.multiple_of(h * N_KEYS, N_KEYS)
        s1 = sc_ref[pl.ds(r0, N_KEYS), :]
        s2 = sc_ref[pl.ds(half + r0, N_KEYS), :]
        a = _top_rows(s1, PEER_TOPK + 1)
        b = _top_rows(s2, PEER_TOPK + 1)
        b_lo = jnp.concatenate(b[:SUBLANES], axis=0)
        b_hi = jnp.concatenate(b[SUBLANES:2 * SUBLANES], axis=0)
        a_hi = jnp.concatenate(a[SUBLANES:2 * SUBLANES], axis=0)
        pieces = [bc(a[0]) + b_lo, bc(a[0]) + b_hi, bc(a[1]) + b_lo]
        for i in range(2, SUBLANES):
            pieces.append(jnp.where(jrow < (PEER_TOPK + 1) // (i + 1), bc(a[i]) + b_lo, NEG_INF))
        pieces.append(a_hi + bc(b[0]))
        ends = jnp.concatenate([a[0] + b[PEER_TOPK], a[PEER_TOPK] + b[0]]
                               + [jnp.full((1, t), NEG_INF, F32)] * (SUBLANES - 2), axis=0)
        pieces.append(ends)
        cand = jnp.concatenate(pieces, axis=0)
        x = cand
        for _ in range(PEER_TOPK - 1):
            x = jnp.where(x == jnp.max(x, axis=0, keepdims=True), NEG_INF, x)
        c16 = jnp.max(x, axis=0, keepdims=True)
        c17 = jnp.max(jnp.where(x == c16, NEG_INF, x), axis=0, keepdims=True)
        thr = 0.5 * (c16 + c17)
        top = a[0] + b[0]
        z = jnp.sum(jnp.where(cand >= thr, jnp.exp(cand - top), 0.0), axis=0, keepdims=True)
        th_ref[pl.ds(r0, N_KEYS), :] = thr - s1
        e1_ref[pl.ds(r0, N_KEYS), :] = jnp.exp(s1 - a[0]) / z
        e2_ref[pl.ds(r0, N_KEYS), :] = jnp.exp(s2 - b[0])
        return carry

    lax.fori_loop(0, PEER_HEADS, head, 0)


def _route(sc, *, rows):
    t = LANES
    half = PEER_HEADS * N_KEYS
    out = pl.BlockSpec((half, t), lambda i: (0, i))
    return pl.pallas_call(
        _route_body,
        grid=(rows // t,),
        in_specs=[pl.BlockSpec((2 * half, t), lambda i: (0, i))],
        out_specs=[out, out, out],
        out_shape=[jax.ShapeDtypeStruct((half, rows), F32)] * 3,
        compiler_params=_cparams("parallel"),
        name="peer_route",
    )(sc)


def _peer_body(xn_ref, h_ref, s2_ref, th_ref, e1_ref, e2_ref, u_ref, vt_ref, out_ref, acc_s, hid_s, w_s):
    j = pl.program_id(1)
    tt = xn_ref.shape[0]

    @pl.when(j == 0)
    def _():
        acc_s[...] = jnp.zeros_like(acc_s)

    hid_s[...] = _dot_nt(u_ref[...], xn_ref[...])

    i1_base = pl.multiple_of(j * PEER_I1_BLOCK, PEER_I1_BLOCK)
    for il in range(PEER_I1_BLOCK):
        es = slice(il * N_KEYS, (il + 1) * N_KEYS)
        for lt in range(tt // LANES):
            ln = slice(lt * LANES, (lt + 1) * LANES)
            g = jnp.zeros((N_KEYS, LANES), F32)
            for h in range(PEER_HEADS):
                hs = slice(h * N_KEYS, (h + 1) * N_KEYS)
                th = th_ref[pl.ds(h * N_KEYS + i1_base, PEER_I1_BLOCK), ln][il:il + 1, :]
                e1 = e1_ref[pl.ds(h * N_KEYS + i1_base, PEER_I1_BLOCK), ln][il:il + 1, :]
                g = g + jnp.where(s2_ref[hs, ln] >= th, e2_ref[hs, ln], 0.0) * e1
            w_s[es, ln] = (g * _gelu_tanh(hid_s[es, ln])).astype(BF16)
    acc_s[...] += _dot(vt_ref[...], w_s[...])

    @pl.when(j == pl.num_programs(1) - 1)
    def _():
        out_ref[...] = h_ref[...] + acc_s[...].T


def _peer(xn2, hn, sc, th, e1, e2, u, vt, *, rows):
    tt = PEER_TOKENS if rows % PEER_TOKENS == 0 else ROW_TILE
    eb = PEER_I1_BLOCK * N_KEYS
    half = PEER_HEADS * N_KEYS
    tok = lambda dt: pl.BlockSpec((tt, D_MODEL), lambda i, j: (i, 0))
    tab = lambda blk: pl.BlockSpec((half, tt), lambda i, j: (blk, i))
    return pl.pallas_call(
        _peer_body,
        grid=(rows // tt, N_EXPERTS // eb),
        in_specs=[tok(BF16), tok(F32), tab(1), tab(0), tab(0), tab(0),
                  pl.BlockSpec((eb, D_MODEL), lambda i, j: (j, 0)),
                  pl.BlockSpec((D_MODEL, eb), lambda i, j: (0, j))],
        out_specs=pl.BlockSpec((tt, D_MODEL), lambda i, j: (i, 0)),
        out_shape=jax.ShapeDtypeStruct((rows, D_MODEL), F32),
        scratch_shapes=[pltpu.VMEM((D_MODEL, tt), F32), pltpu.VMEM((eb, tt), F32), pltpu.VMEM((eb, tt), BF16)],
        compiler_params=_cparams("parallel", "arbitrary"),
        name="peer_dense",
    )(xn2, hn, sc, th, e1, e2, u, vt)


def _block_diag(w):
    eye = jnp.eye(RG_HEADS, dtype=w.dtype)
    return jnp.einsum("hij,hg->higj", w, eye).reshape(RG_WIDTH, RG_WIDTH)


def _head_pad(w, lo, hi):
    k = w.shape[0]
    part = w[:, :, lo:hi]
    return jnp.pad(part, ((0, 0), (0, 0), (0, HEAD_PAD - (hi - lo)))).reshape(k, N_HEADS * HEAD_PAD)


def _rope_partner(x1, x2):
    z64 = jnp.zeros(x1.shape[:-1] + (QK_NOPE,), x1.dtype)
    z32 = jnp.zeros(x1.shape[:-1] + (HEAD_PAD - QK_DIM,), x1.dtype)
    return jnp.concatenate([z64, x2, x1, z32], axis=-1)


def _layer_weights(i, norm1_g, w_in, conv_w, conv_b, rg_wa, rg_ba, rg_wi, rg_bi, rg_lambda, q_norm_g, w_uq,
                   kv_norm_g, w_ukv, q_head_g, k_head_g, out_g_rg, out_g_attn, w_out, norm2_g, peer_wq,
                   peer_subkeys, peer_u, peer_v):
    half = QK_ROPE // 2
    wq3 = w_uq[i].reshape(Q_LORA, N_HEADS, QK_DIM)
    wkv3 = w_ukv[i].reshape(KV_LORA, N_HEADS, QK_NOPE + V_DIM)
    eye = jnp.eye(QK_ROPE, dtype=F32)
    ek = jnp.concatenate([jnp.zeros((QK_ROPE, QK_NOPE), F32), eye,
                          jnp.zeros((QK_ROPE, HEAD_PAD - QK_DIM), F32)], axis=-1)
    eks = _rope_partner(eye[:, :half], eye[:, half:])
    place = lambda e: jnp.pad(jnp.tile(e, (1, N_HEADS)), ((0, LANES - QK_ROPE), (0, 0))).astype(BF16)

    def head_gain(g):
        main = jnp.pad(g, (0, HEAD_PAD - QK_DIM)).reshape(1, HEAD_PAD)
        swap = _rope_partner(g[QK_NOPE:QK_NOPE + half], g[QK_NOPE + half:]).reshape(1, HEAD_PAD)
        return main, swap

    gq, gqs = head_gain(q_head_g[i])
    gk, gks = head_gain(k_head_g[i])
    wout = w_out[i]
    wout_at = jnp.pad(wout[RG_WIDTH:].reshape(N_HEADS, V_DIM, D_MODEL),
                      ((0, 0), (0, HEAD_PAD - V_DIM), (0, 0))).reshape(N_HEADS * HEAD_PAD, D_MODEL)
    gat = jnp.pad(out_g_attn[i].reshape(N_HEADS, V_DIM), ((0, 0), (0, HEAD_PAD - V_DIM))).reshape(1, -1)
    return dict(
        g1=norm1_g[i].reshape(1, -1),
        win=jnp.pad(w_in[i], ((0, 0), (0, P_IN_PAD - P_IN))).astype(BF16),
        qg=q_norm_g[i].reshape(1, -1),
        wq=_head_pad(wq3, 0, QK_DIM).astype(BF16),
        wqs=_rope_partner(wq3[:, :, QK_NOPE:QK_NOPE + half], wq3[:, :, QK_NOPE + half:])
        .reshape(Q_LORA, -1).astype(BF16),
        kvg=kv_norm_g[i].reshape(1, -1),
        wk=_head_pad(wkv3, 0, QK_NOPE).astype(BF16),
        wv=_head_pad(wkv3, QK_NOPE, QK_NOPE + V_DIM).astype(BF16),
        ek=place(ek), eks=place(eks), gq=gq, gqs=gqs, gk=gk, gks=gks,
        conv_w=conv_w[i], conv_b=conv_b[i].reshape(1, -1),
        wgate_f=jnp.concatenate([_block_diag(rg_wa[i, 0]), _block_diag(rg_wi[i, 0])], axis=1).astype(BF16),
        wgate_b=jnp.concatenate([_block_diag(rg_wa[i, 1]), _block_diag(rg_wi[i, 1])], axis=1).astype(BF16),
        gate_bias=jnp.stack([rg_ba[i, 0].reshape(-1), rg_bi[i, 0].reshape(-1),
                             rg_ba[i, 1].reshape(-1), rg_bi[i, 1].reshape(-1)]),
        lam=rg_lambda[i],
        grg=out_g_rg[i].reshape(1, -1), gat=gat,
        wout_rg=wout[:RG_WIDTH].astype(BF16), wout_at=wout_at.astype(BF16),
        g2=norm2_g[i].reshape(1, -1),
        wqt=peer_wq[i].T.astype(BF16),
        subk=peer_subkeys[i].reshape(2 * PEER_HEADS, N_KEYS, PEER_HALF).astype(BF16),
        u=peer_u[i].astype(BF16),
        vt=peer_v[i].T.astype(BF16),
    )


def _rope_tables(length):
    pos = jnp.arange(length, dtype=F32)
    inv = ROPE_THETA ** (-jnp.arange(0, QK_ROPE, 2, dtype=F32) / QK_ROPE)
    ang = pos[:, None] * inv[None, :]
    cos, sin = jnp.cos(ang), jnp.sin(ang)
    ones = jnp.ones((length, QK_NOPE), F32)
    zpad = jnp.zeros((length, HEAD_PAD - QK_DIM), F32)
    cos_t = jnp.concatenate([ones, cos, cos, zpad], axis=-1)
    sin_t = jnp.concatenate([jnp.zeros((length, QK_NOPE), F32), -sin, sin, zpad], axis=-1)
    return cos_t, sin_t


def _padded_len(batch, length):
    lp = -(-length // ROW_TILE) * ROW_TILE
    if (batch * lp) % PEER_TOKENS:
        lp += ROW_TILE
    return lp


def _encode(x, meta_tokens, layers):
    batch, n_tok, _ = x.shape
    seq_len = n_tok + N_META
    lp = _padded_len(batch, seq_len)
    rows = batch * lp
    meta = jnp.broadcast_to(meta_tokens[None].astype(x.dtype), (batch, N_META, D_MODEL))
    h = jnp.concatenate([meta, x, jnp.zeros((batch, lp - seq_len, D_MODEL), x.dtype)], axis=1)
    h = h.reshape(rows, D_MODEL)
    cos_t, sin_t = _rope_tables(lp)
    wide = N_HEADS * HEAD_PAD
    for w in layers:
        xr, gate, q, k, v = _inproj(h, cos_t, sin_t, w, rows=rows, pos_tiles=lp // ROW_TILE)
        hf, hb = _scan(xr, w, batch=batch, padded_len=lp, seq_len=seq_len)
        kt = k.reshape(batch, lp, N_HEADS, HEAD_PAD).transpose(0, 2, 3, 1)
        o = _attention(q.reshape(batch, lp, wide), kt, v.reshape(batch, lp, wide),
                       batch=batch, padded_len=lp, seq_len=seq_len)
        hn, xn2, sc = _outproj(hf, hb, gate, o.reshape(rows, wide), h, w, rows=rows)
        th, e1, e2 = _route(sc, rows=rows)
        h = _peer(xn2, hn, sc, th, e1, e2, w["u"], w["vt"], rows=rows)
    return h.reshape(batch, lp, D_MODEL)[:, N_META:seq_len]


def kernel(x_prompt, x_sample, meta_tokens, norm1_g, w_in, conv_w, conv_b, rg_wa, rg_ba, rg_wi, rg_bi, rg_lambda, q_norm_g, w_uq, kv_norm_g, w_ukv, q_head_g, k_head_g, out_g_rg, out_g_attn, w_out, norm2_g, peer_wq, peer_subkeys, peer_u, peer_v):
    params = (norm1_g, w_in, conv_w, conv_b, rg_wa, rg_ba, rg_wi, rg_bi, rg_lambda, q_norm_g, w_uq, kv_norm_g,
              w_ukv, q_head_g, k_head_g, out_g_rg, out_g_attn, w_out, norm2_g, peer_wq, peer_subkeys, peer_u,
              peer_v)
    layers = [_layer_weights(i, *params) for i in range(norm1_g.shape[0])]
    y_prompt = _encode(x_prompt, meta_tokens, layers)
    y_sample = _encode(x_sample, meta_tokens, layers)
    return (y_prompt, y_sample)
```

```python
import functools
import math

import jax
import jax.numpy as jnp
from jax import lax
from jax.experimental import pallas as pl
from jax.experimental.pallas import tpu as pltpu

F32 = jnp.float32
BF16 = jnp.bfloat16

D_MODEL = 1024
N_META = 16
EPS = 1e-6
RG_WIDTH = 512
RG_HEADS = 8
RG_BLOCK = 64
CONV_W = 4
RG_C = 8.0
N_HEADS = 8
QK_NOPE = 64
QK_ROPE = 32
QK_DIM = 96
V_DIM = 64
Q_LORA = 256
KV_LORA = 128
ROPE_THETA = 10000.0
P_IN = 2 * RG_WIDTH + Q_LORA + KV_LORA + QK_ROPE
PEER_HEADS = 8
N_KEYS = 128
N_EXPERTS = N_KEYS * N_KEYS
PEER_TOPK = 16
PEER_HALF = 128

LANES = 128
SUBLANES = 8
HEAD_PAD = LANES
P_IN_PAD = 12 * LANES
ROW_TILE = 256
ATTN_Q_TILE = 512
KV_CHUNK = 1024
PEER_TOKENS = 512
PEER_I1_BLOCK = 8
VMEM_LIMIT = 52 * 1024 * 1024
NEG_INF = float("-inf")
MASK_VALUE = -1e30
LOG2E = 1.4426950408889634


def _cparams(*sem):
    return pltpu.CompilerParams(dimension_semantics=sem, vmem_limit_bytes=VMEM_LIMIT)


def _rms(x, g):
    return x * lax.rsqrt(jnp.mean(x * x, axis=-1, keepdims=True) + EPS) * g


def _gelu_tanh(x):
    return x * (0.5 * (1.0 + jnp.tanh(0.7978845608028654 * (x + 0.044715 * (x * x * x)))))


def _sigmoid(x):
    return 1.0 / (1.0 + jnp.exp(-x))


def _neg_expm1(y):
    u = jnp.exp(y)
    um1 = u - 1.0
    tiny = um1 == 0.0
    r = jnp.where(tiny, y, um1 * y / jnp.where(tiny, 1.0, jnp.log(u)))
    return -jnp.where(um1 == -1.0, -1.0, r)


def _dot(a, b):
    return jnp.dot(a, b, preferred_element_type=F32)


def _dot_nt(a, b):
    return lax.dot_general(a, b, (((1,), (1,)), ((), ())), preferred_element_type=F32)


def _full(shape):
    n = len(shape)
    return pl.BlockSpec(shape, lambda *_: (0,) * n)


def _inproj_body(x_ref, cos_ref, sin_ref, g1_ref, win_ref, qg_ref, wq_ref, wqs_ref, kvg_ref,
                 wk_ref, ek_ref, eks_ref, wv_ref, gq_ref, gqs_ref, gk_ref, gks_ref,
                 xr_ref, gate_ref, q_ref, k_ref, v_ref):
    xn = _rms(x_ref[...], g1_ref[...])
    p = _dot(xn.astype(BF16), win_ref[...])
    xr_ref[...] = p[:, :RG_WIDTH]
    gate_ref[...] = p[:, RG_WIDTH:2 * RG_WIDTH]
    o2 = 2 * RG_WIDTH
    qcn = _rms(p[:, o2:o2 + Q_LORA], qg_ref[...]).astype(BF16)
    kvn = _rms(p[:, o2 + Q_LORA:o2 + Q_LORA + KV_LORA], kvg_ref[...]).astype(BF16)
    kpe = p[:, o2 + Q_LORA + KV_LORA:].astype(BF16)
    q = _dot(qcn, wq_ref[...])
    qs = _dot(qcn, wqs_ref[...])
    k = _dot(kvn, wk_ref[...]) + _dot(kpe, ek_ref[...])
    ks = _dot(kpe, eks_ref[...])
    v_ref[...] = _dot(kvn, wv_ref[...]).astype(BF16)
    c = cos_ref[...]
    s = sin_ref[...]
    qscale = QK_DIM ** -0.5 * LOG2E
    for h in range(N_HEADS):
        sl = slice(h * HEAD_PAD, (h + 1) * HEAD_PAD)
        qh = q[:, sl]
        rq = lax.rsqrt(jnp.sum(qh * qh, axis=-1, keepdims=True) * (1.0 / QK_DIM) + EPS)
        q_ref[:, sl] = ((qh * gq_ref[...] * c + qs[:, sl] * gqs_ref[...] * s) * (rq * qscale)).astype(BF16)
        kh = k[:, sl]
        rk = lax.rsqrt(jnp.sum(kh * kh, axis=-1, keepdims=True) * (1.0 / QK_DIM) + EPS)
        k_ref[:, sl] = ((kh * gk_ref[...] * c + ks[:, sl] * gks_ref[...] * s) * rk).astype(BF16)


def _inproj(h, cos_t, sin_t, w, *, rows, pos_tiles):
    tm = ROW_TILE
    row = lambda n: pl.BlockSpec((tm, n), lambda i: (i, 0))
    pos = pl.BlockSpec((tm, HEAD_PAD), lambda i: (i % pos_tiles, 0))
    wide = N_HEADS * HEAD_PAD
    return pl.pallas_call(
        _inproj_body,
        grid=(rows // tm,),
        in_specs=[row(D_MODEL), pos, pos, _full((1, D_MODEL)), _full((D_MODEL, P_IN_PAD)),
                  _full((1, Q_LORA)), _full((Q_LORA, wide)), _full((Q_LORA, wide)),
                  _full((1, KV_LORA)), _full((KV_LORA, wide)), _full((LANES, wide)), _full((LANES, wide)),
                  _full((KV_LORA, wide)),
                  _full((1, HEAD_PAD)), _full((1, HEAD_PAD)), _full((1, HEAD_PAD)), _full((1, HEAD_PAD))],
        out_specs=[row(RG_WIDTH), row(RG_WIDTH), row(wide), row(wide), row(wide)],
        out_shape=[jax.ShapeDtypeStruct((rows, RG_WIDTH), F32), jax.ShapeDtypeStruct((rows, RG_WIDTH), F32),
                   jax.ShapeDtypeStruct((rows, wide), BF16), jax.ShapeDtypeStruct((rows, wide), BF16),
                   jax.ShapeDtypeStruct((rows, wide), BF16)],
        compiler_params=_cparams("parallel"),
        name="inproj",
    )(h, cos_t, sin_t, w["g1"], w["win"], w["qg"], w["wq"], w["wqs"], w["kvg"], w["wk"], w["ek"], w["eks"],
      w["wv"], w["gq"], w["gqs"], w["gk"], w["gks"])


def _scan_body(xm_f, xp_f, xn_f, xm_b, xp_b, xn_b, cw_ref, cb_ref, wf_ref, wb_ref, bias_ref, lam_ref,
               hf_ref, hb_ref, ext_s, a_s, b_s, cf_s, cb_s, *, seq_len, chunk, n_chunks):
    i = pl.program_id(1)
    C = chunk
    groups = C // SUBLANES

    @pl.when(i == 0)
    def _():
        cf_s[...] = jnp.zeros_like(cf_s)
        cb_s[...] = jnp.zeros_like(cb_s)

    rows8 = lax.broadcasted_iota(jnp.int32, (SUBLANES, RG_WIDTH), 0)
    rows_c = lax.broadcasted_iota(jnp.int32, (C, RG_WIDTH), 0)
    rmod = rows_c & (SUBLANES - 1)

    def masked(x, pos):
        return jnp.where((pos >= 0) & (pos < seq_len), x, 0.0)

    def affine_terms(xm, xp, xn, j, w_ref, d):
        base = j * C
        ext_s[0:SUBLANES, :] = masked(xp[...], base - SUBLANES + rows8)
        ext_s[SUBLANES:SUBLANES + C, :] = masked(xm[...], base + rows_c)
        ext_s[SUBLANES + C:, :] = masked(xn[...], base + C + rows8)
        xc = cb_ref[...]
        for t in range(CONV_W):
            xc = xc + cw_ref[t:t + 1, :] * ext_s[SUBLANES - 2 + t:SUBLANES - 2 + t + C, :]
        gm = _dot(xc.astype(BF16), w_ref[...])
        r = _sigmoid(gm[:, :RG_WIDTH] + bias_ref[2 * d:2 * d + 1, :])
        ig = _sigmoid(gm[:, RG_WIDTH:] + bias_ref[2 * d + 1:2 * d + 2, :])
        z = -lam_ref[d:d + 1, :]
        softplus = jnp.maximum(z, 0.0) + jnp.log1p(jnp.exp(-jnp.abs(z)))
        log_a = (-RG_C) * r * softplus
        a = jnp.exp(log_a)
        b = jnp.sqrt(_neg_expm1(2.0 * log_a)) * (ig * xc)
        b = jnp.where(base + rows_c < seq_len, b, 0.0)
        return a, b

    a, b = affine_terms(xm_f, xp_f, xn_f, i, wf_ref, 0)
    for d in (1, 2, 4):
        m = rmod >= d
        a_sh = pltpu.roll(a, d, 0)
        b_sh = pltpu.roll(b, d, 0)
        b = jnp.where(m, a * b_sh + b, b)
        a = jnp.where(m, a * a_sh, a)
    a_s[...] = a
    b_s[...] = b

    def fwd_group(g, carry):
        r0 = pl.multiple_of(g * SUBLANES, SUBLANES)
        hrow = a_s[pl.ds(r0, SUBLANES), :] * carry + b_s[pl.ds(r0, SUBLANES), :]
        hf_ref[pl.ds(r0, SUBLANES), :] = hrow
        return jnp.broadcast_to(hrow[SUBLANES - 1:SUBLANES, :], (SUBLANES, RG_WIDTH))

    cf_s[...] = lax.fori_loop(0, groups, fwd_group, cf_s[...])

    a, b = affine_terms(xm_b, xp_b, xn_b, n_chunks - 1 - i, wb_ref, 1)
    for d in (1, 2, 4):
        m = rmod < SUBLANES - d
        a_sh = pltpu.roll(a, C - d, 0)
        b_sh = pltpu.roll(b, C - d, 0)
        b = jnp.where(m, a * b_sh + b, b)
        a = jnp.where(m, a * a_sh, a)
    a_s[...] = a
    b_s[...] = b

    def bwd_group(g, carry):
        r0 = pl.multiple_of((groups - 1 - g) * SUBLANES, SUBLANES)
        hrow = a_s[pl.ds(r0, SUBLANES), :] * carry + b_s[pl.ds(r0, SUBLANES), :]
        hb_ref[pl.ds(r0, SUBLANES), :] = hrow
        return jnp.broadcast_to(hrow[0:1, :], (SUBLANES, RG_WIDTH))

    cb_s[...] = lax.fori_loop(0, groups, bwd_group, cb_s[...])


def _scan(xr, w, *, batch, padded_len, seq_len):
    C = ROW_TILE
    n = padded_len // C
    per8 = C // SUBLANES
    last8 = batch * padded_len // SUBLANES - 1

    def chunk_f(b, i):
        return b * n + i

    def chunk_b(b, i):
        return b * n + (n - 1 - i)

    def specs(chunk_of):
        main = pl.BlockSpec((C, RG_WIDTH), lambda b, i: (chunk_of(b, i), 0))
        prev = pl.BlockSpec((SUBLANES, RG_WIDTH), lambda b, i: (jnp.maximum(chunk_of(b, i) * per8 - 1, 0), 0))
        nxt = pl.BlockSpec((SUBLANES, RG_WIDTH),
                           lambda b, i: (jnp.minimum((chunk_of(b, i) + 1) * per8, last8), 0))
        return [main, prev, nxt]

    rows = batch * padded_len
    return pl.pallas_call(
        functools.partial(_scan_body, seq_len=seq_len, chunk=C, n_chunks=n),
        grid=(batch, n),
        in_specs=specs(chunk_f) + specs(chunk_b) + [
            _full((CONV_W, RG_WIDTH)), _full((1, RG_WIDTH)),
            _full((RG_WIDTH, 2 * RG_WIDTH)), _full((RG_WIDTH, 2 * RG_WIDTH)),
            _full((4, RG_WIDTH)), _full((2, RG_WIDTH))],
        out_specs=[pl.BlockSpec((C, RG_WIDTH), lambda b, i: (chunk_f(b, i), 0)),
                   pl.BlockSpec((C, RG_WIDTH), lambda b, i: (chunk_b(b, i), 0))],
        out_shape=[jax.ShapeDtypeStruct((rows, RG_WIDTH), F32), jax.ShapeDtypeStruct((rows, RG_WIDTH), F32)],
        scratch_shapes=[pltpu.VMEM((C + 2 * SUBLANES, RG_WIDTH), F32), pltpu.VMEM((C, RG_WIDTH), F32),
                        pltpu.VMEM((C, RG_WIDTH), F32), pltpu.VMEM((SUBLANES, RG_WIDTH), F32),
                        pltpu.VMEM((SUBLANES, RG_WIDTH), F32)],
        compiler_params=_cparams("arbitrary", "arbitrary"),
        name="rglru_scan",
    )(xr, xr, xr, xr, xr, xr, w["conv_w"], w["conv_b"], w["wgate_f"], w["wgate_b"], w["gate_bias"], w["lam"])


def _attn_body(q_ref, kt_ref, v_ref, o_ref, s_a, s_b, m_s, l_s, acc_s, *, seq_len, n_full, rem):
    tq = q_ref.shape[1]
    m_s[...] = jnp.full_like(m_s, NEG_INF)
    l_s[...] = jnp.zeros_like(l_s)
    acc_s[...] = jnp.zeros_like(acc_s)

    def scores(dst, start, width):
        dst[:, :width] = _dot(q_ref[0], kt_ref[0, 0, :, pl.ds(start, width)])

    def update(src, start, width, mask):
        s = src[:, :width]
        if mask:
            kpos = start + lax.broadcasted_iota(jnp.int32, (tq, width), 1)
            s = jnp.where(kpos < seq_len, s, MASK_VALUE)
        m = m_s[...]
        m_new = jnp.maximum(m, jnp.max(s, axis=-1, keepdims=True))
        alpha = jnp.exp2(m - m_new)
        p = jnp.exp2(s - m_new)
        m_s[...] = m_new
        l_s[...] = alpha * l_s[...] + jnp.sum(p, axis=-1, keepdims=True)
        acc_s[...] = alpha * acc_s[...] + _dot(p.astype(BF16), v_ref[0, pl.ds(start, width), :])

    if rem:
        scores(s_b, n_full * KV_CHUNK, rem)
    if n_full:
        scores(s_a, 0, KV_CHUNK)
    if rem:
        update(s_b, n_full * KV_CHUNK, rem, True)
    pairs = max((n_full - 1) // 2, 0)
    tail = [(c * KV_CHUNK, KV_CHUNK, False) for c in range(2 * pairs, n_full)]

    def pair(i, carry):
        c0 = pl.multiple_of(2 * i * KV_CHUNK, KV_CHUNK)
        scores(s_b, c0 + KV_CHUNK, KV_CHUNK)
        update(s_a, c0, KV_CHUNK, False)
        scores(s_a, c0 + 2 * KV_CHUNK, KV_CHUNK)
        update(s_b, c0 + KV_CHUNK, KV_CHUNK, False)
        return carry

    lax.fori_loop(0, pairs, pair, 0)
    bufs = (s_a, s_b)
    for n, (start, width, mask) in enumerate(tail):
        if n + 1 < len(tail):
            scores(bufs[(n + 1) % 2], tail[n + 1][0], tail[n + 1][1])
        update(bufs[n % 2], start, width, mask)
    o_ref[0] = acc_s[...] / l_s[...]


def _attention(q, kt, v, *, batch, padded_len, seq_len):
    tq = ATTN_Q_TILE
    n_full = seq_len // KV_CHUNK
    rem = -(-(seq_len - n_full * KV_CHUNK) // LANES) * LANES
    wide = N_HEADS * HEAD_PAD
    return pl.pallas_call(
        functools.partial(_attn_body, seq_len=seq_len, n_full=n_full, rem=rem),
        grid=(batch, N_HEADS, padded_len // tq),
        in_specs=[pl.BlockSpec((1, tq, HEAD_PAD), lambda b, h, i: (b, i, h)),
                  pl.BlockSpec((1, 1, HEAD_PAD, padded_len), lambda b, h, i: (b, h, 0, 0)),
                  pl.BlockSpec((1, padded_len, HEAD_PAD), lambda b, h, i: (b, 0, h))],
        out_specs=pl.BlockSpec((1, tq, HEAD_PAD), lambda b, h, i: (b, i, h)),
        out_shape=jax.ShapeDtypeStruct((batch, padded_len, wide), F32),
        scratch_shapes=[pltpu.VMEM((tq, KV_CHUNK), F32), pltpu.VMEM((tq, KV_CHUNK), F32),
                        pltpu.VMEM((tq, 1), F32), pltpu.VMEM((tq, 1), F32), pltpu.VMEM((tq, HEAD_PAD), F32)],
        compiler_params=_cparams("parallel", "parallel", "arbitrary"),
        name="attention",
    )(q, kt, v)


def _outproj_body(hf_ref, hb_ref, gate_ref, o_ref, h_ref, grg_ref, gat_ref, wrg_ref, wat_ref, g2_ref,
                  wqt_ref, sk_ref, hn_ref, xn2_ref, sc_ref):
    rg = (hf_ref[...] + hb_ref[...]) * _gelu_tanh(gate_ref[...])
    rgn = _rms(rg, grg_ref[...])
    o = o_ref[...]
    ms = jnp.sum(o * o, axis=-1, keepdims=True) * (1.0 / (N_HEADS * V_DIM))
    attn = o * lax.rsqrt(ms + EPS) * gat_ref[...]
    hn = h_ref[...] + _dot(rgn.astype(BF16), wrg_ref[...]) + _dot(attn.astype(BF16), wat_ref[...])
    hn_ref[...] = hn
    xn2 = _rms(hn, g2_ref[...]).astype(BF16)
    xn2_ref[...] = xn2
    qpt = _dot_nt(wqt_ref[...], xn2)
    for h in range(PEER_HEADS):
        for s in range(2):
            r0 = (h * 2 + s) * PEER_HALF
            blk = qpt[r0:r0 + PEER_HALF, :].astype(BF16)
            o0 = s * PEER_HEADS * N_KEYS + h * N_KEYS
            sc_ref[o0:o0 + N_KEYS, :] = _dot(sk_ref[h * 2 + s], blk)


def _outproj(hf, hb, gate, o, h, w, *, rows):
    tm = ROW_TILE
    row = lambda n: pl.BlockSpec((tm, n), lambda i: (i, 0))
    wide = N_HEADS * HEAD_PAD
    nsc = 2 * PEER_HEADS * N_KEYS
    return pl.pallas_call(
        _outproj_body,
        grid=(rows // tm,),
        in_specs=[row(RG_WIDTH), row(RG_WIDTH), row(RG_WIDTH), row(wide), row(D_MODEL),
                  _full((1, RG_WIDTH)), _full((1, wide)), _full((RG_WIDTH, D_MODEL)), _full((wide, D_MODEL)),
                  _full((1, D_MODEL)), _full((nsc, D_MODEL)), _full((2 * PEER_HEADS, N_KEYS, PEER_HALF))],
        out_specs=[row(D_MODEL), row(D_MODEL), pl.BlockSpec((nsc, tm), lambda i: (0, i))],
        out_shape=[jax.ShapeDtypeStruct((rows, D_MODEL), F32), jax.ShapeDtypeStruct((rows, D_MODEL), BF16),
                   jax.ShapeDtypeStruct((nsc, rows), F32)],
        compiler_params=_cparams("parallel"),
        name="outproj",
    )(hf, hb, gate, o, h, w["grg"], w["gat"], w["wout_rg"], w["wout_at"], w["g2"], w["wqt"], w["subk"])


def _top_rows(x, count):
    rows = []
    for r in range(count):
        m = jnp.max(x, axis=0, keepdims=True)
        rows.append(m)
        if r + 1 < count:
            x = jnp.where(x == m, NEG_INF, x)
    return rows


def _route_body(sc_ref, th_ref, e1_ref, e2_ref):
    t = sc_ref.shape[1]
    half = PEER_HEADS * N_KEYS
    jrow = lax.broadcasted_iota(jnp.int32, (SUBLANES, t), 0)

    def bc(row):
        return jnp.broadcast_to(row, (SUBLANES, t))

    def head(h, carry):
        r0 = pl.multiple_of(h * N_KEYS, N_KEYS)
        s1 = sc_ref[pl.ds(r0, N_KEYS), :]
        s2 = sc_ref[pl.ds(half + r0, N_KEYS), :]
        a = _top_rows(s1, PEER_TOPK + 1)
        b = _top_rows(s2, PEER_TOPK + 1)
        b_lo = jnp.concatenate(b[:SUBLANES], axis=0)
        b_hi = jnp.concatenate(b[SUBLANES:2 * SUBLANES], axis=0)
        a_hi = jnp.concatenate(a[SUBLANES:2 * SUBLANES], axis=0)
        pieces = [bc(a[0]) + b_lo, bc(a[0]) + b_hi, bc(a[1]) + b_lo]
        for i in range(2, SUBLANES):
            pieces.append(jnp.where(jrow < (PEER_TOPK + 1) // (i + 1), bc(a[i]) + b_lo, NEG_INF))
        pieces.append(a_hi + bc(b[0]))
        ends = jnp.concatenate([a[0] + b[PEER_TOPK], a[PEER_TOPK] + b[0]]
                               + [jnp.full((1, t), NEG_INF, F32)] * (SUBLANES - 2), axis=0)
        pieces.append(ends)
        cand = jnp.concatenate(pieces, axis=0)
        x = cand
        for _ in range(PEER_TOPK - 1):
            x = jnp.where(x == jnp.max(x, axis=0, keepdims=True), NEG_INF, x)
        c16 = jnp.max(x, axis=0, keepdims=True)
        c17 = jnp.max(jnp.where(x == c16, NEG_INF, x), axis=0, keepdims=True)
        thr = 0.5 * (c16 + c17)
        top = a[0] + b[0]
        z = jnp.sum(jnp.where(cand >= thr, jnp.exp(cand - top), 0.0), axis=0, keepdims=True)
        th_ref[pl.ds(r0, N_KEYS), :] = thr - s1
        e1_ref[pl.ds(r0, N_KEYS), :] = jnp.exp(s1 - a[0]) / z
        e2_ref[pl.ds(r0, N_KEYS), :] = jnp.exp(s2 - b[0])
        return carry

    lax.fori_loop(0, PEER_HEADS, head, 0)


def _route(sc, *, rows):
    t = LANES
    half = PEER_HEADS * N_KEYS
    out = pl.BlockSpec((half, t), lambda i: (0, i))
    return pl.pallas_call(
        _route_body,
        grid=(rows // t,),
        in_specs=[pl.BlockSpec((2 * half, t), lambda i: (0, i))],
        out_specs=[out, out, out],
        out_shape=[jax.ShapeDtypeStruct((half, rows), F32)] * 3,
        compiler_params=_cparams("parallel"),
        name="peer_route",
    )(sc)


def _peer_body(xn_ref, h_ref, s2_ref, th_ref, e1_ref, e2_ref, u_ref, vt_ref, out_ref, acc_s, hid_s, w_s):
    j = pl.program_id(1)
    tt = xn_ref.shape[0]

    @pl.when(j == 0)
    def _():
        acc_s[...] = jnp.zeros_like(acc_s)

    hid_s[...] = _dot_nt(u_ref[...], xn_ref[...])

    i1_base = pl.multiple_of(j * PEER_I1_BLOCK, PEER_I1_BLOCK)
    for il in range(PEER_I1_BLOCK):
        es = slice(il * N_KEYS, (il + 1) * N_KEYS)
        for lt in range(tt // LANES):
            ln = slice(lt * LANES, (lt + 1) * LANES)
            g = jnp.zeros((N_KEYS, LANES), F32)
            for h in range(PEER_HEADS):
                hs = slice(h * N_KEYS, (h + 1) * N_KEYS)
                th = th_ref[pl.ds(h * N_KEYS + i1_base, PEER_I1_BLOCK), ln][il:il + 1, :]
                e1 = e1_ref[pl.ds(h * N_KEYS + i1_base, PEER_I1_BLOCK), ln][il:il + 1, :]
                g = g + jnp.where(s2_ref[hs, ln] >= th, e2_ref[hs, ln], 0.0) * e1
            w_s[es, ln] = (g * _gelu_tanh(hid_s[es, ln])).astype(BF16)
    acc_s[...] += _dot(vt_ref[...], w_s[...])

    @pl.when(j == pl.num_programs(1) - 1)
    def _():
        out_ref[...] = h_ref[...] + acc_s[...].T


def _peer(xn2, hn, sc, th, e1, e2, u, vt, *, rows):
    tt = PEER_TOKENS if rows % PEER_TOKENS == 0 else ROW_TILE
    eb = PEER_I1_BLOCK * N_KEYS
    half = PEER_HEADS * N_KEYS
    tok = lambda dt: pl.BlockSpec((tt, D_MODEL), lambda i, j: (i, 0))
    tab = lambda blk: pl.BlockSpec((half, tt), lambda i, j: (blk, i))
    return pl.pallas_call(
        _peer_body,
        grid=(rows // tt, N_EXPERTS // eb),
        in_specs=[tok(BF16), tok(F32), tab(1), tab(0), tab(0), tab(0),
                  pl.BlockSpec((eb, D_MODEL), lambda i, j: (j, 0)),
                  pl.BlockSpec((D_MODEL, eb), lambda i, j: (0, j))],
        out_specs=pl.BlockSpec((tt, D_MODEL), lambda i, j: (i, 0)),
        out_shape=jax.ShapeDtypeStruct((rows, D_MODEL), F32),
        scratch_shapes=[pltpu.VMEM((D_MODEL, tt), F32), pltpu.VMEM((eb, tt), F32), pltpu.VMEM((eb, tt), BF16)],
        compiler_params=_cparams("parallel", "arbitrary"),
        name="peer_dense",
    )(xn2, hn, sc, th, e1, e2, u, vt)


def _block_diag(w):
    eye = jnp.eye(RG_HEADS, dtype=w.dtype)
    return jnp.einsum("hij,hg->higj", w, eye).reshape(RG_WIDTH, RG_WIDTH)


def _head_pad(w, lo, hi):
    k = w.shape[0]
    part = w[:, :, lo:hi]
    return jnp.pad(part, ((0, 0), (0, 0), (0, HEAD_PAD - (hi - lo)))).reshape(k, N_HEADS * HEAD_PAD)


def _rope_partner(x1, x2):
    z64 = jnp.zeros(x1.shape[:-1] + (QK_NOPE,), x1.dtype)
    z32 = jnp.zeros(x1.shape[:-1] + (HEAD_PAD - QK_DIM,), x1.dtype)
    return jnp.concatenate([z64, x2, x1, z32], axis=-1)


def _layer_weights(i, norm1_g, w_in, conv_w, conv_b, rg_wa, rg_ba, rg_wi, rg_bi, rg_lambda, q_norm_g, w_uq,
                   kv_norm_g, w_ukv, q_head_g, k_head_g, out_g_rg, out_g_attn, w_out, norm2_g, peer_wq,
                   peer_subkeys, peer_u, peer_v):
    half = QK_ROPE // 2
    wq3 = w_uq[i].reshape(Q_LORA, N_HEADS, QK_DIM)
    wkv3 = w_ukv[i].reshape(KV_LORA, N_HEADS, QK_NOPE + V_DIM)
    eye = jnp.eye(QK_ROPE, dtype=F32)
    ek = jnp.concatenate([jnp.zeros((QK_ROPE, QK_NOPE), F32), eye,
                          jnp.zeros((QK_ROPE, HEAD_PAD - QK_DIM), F32)], axis=-1)
    eks = _rope_partner(eye[:, :half], eye[:, half:])
    place = lambda e: jnp.pad(jnp.tile(e, (1, N_HEADS)), ((0, LANES - QK_ROPE), (0, 0))).astype(BF16)

    def head_gain(g):
        main = jnp.pad(g, (0, HEAD_PAD - QK_DIM)).reshape(1, HEAD_PAD)
        swap = _rope_partner(g[QK_NOPE:QK_NOPE + half], g[QK_NOPE + half:]).reshape(1, HEAD_PAD)
        return main, swap

    gq, gqs = head_gain(q_head_g[i])
    gk, gks = head_gain(k_head_g[i])
    wout = w_out[i]
    wout_at = jnp.pad(wout[RG_WIDTH:].reshape(N_HEADS, V_DIM, D_MODEL),
                      ((0, 0), (0, HEAD_PAD - V_DIM), (0, 0))).reshape(N_HEADS * HEAD_PAD, D_MODEL)
    gat = jnp.pad(out_g_attn[i].reshape(N_HEADS, V_DIM), ((0, 0), (0, HEAD_PAD - V_DIM))).reshape(1, -1)
    return dict(
        g1=norm1_g[i].reshape(1, -1),
        win=jnp.pad(w_in[i], ((0, 0), (0, P_IN_PAD - P_IN))).astype(BF16),
        qg=q_norm_g[i].reshape(1, -1),
        wq=_head_pad(wq3, 0, QK_DIM).astype(BF16),
        wqs=_rope_partner(wq3[:, :, QK_NOPE:QK_NOPE + half], wq3[:, :, QK_NOPE + half:])
        .reshape(Q_LORA, -1).astype(BF16),
        kvg=kv_norm_g[i].reshape(1, -1),
        wk=_head_pad(wkv3, 0, QK_NOPE).astype(BF16),
        wv=_head_pad(wkv3, QK_NOPE, QK_NOPE + V_DIM).astype(BF16),
        ek=place(ek), eks=place(eks), gq=gq, gqs=gqs, gk=gk, gks=gks,
        conv_w=conv_w[i], conv_b=conv_b[i].reshape(1, -1),
        wgate_f=jnp.concatenate([_block_diag(rg_wa[i, 0]), _block_diag(rg_wi[i, 0])], axis=1).astype(BF16),
        wgate_b=jnp.concatenate([_block_diag(rg_wa[i, 1]), _block_diag(rg_wi[i, 1])], axis=1).astype(BF16),
        gate_bias=jnp.stack([rg_ba[i, 0].reshape(-1), rg_bi[i, 0].reshape(-1),
                             rg_ba[i, 1].reshape(-1), rg_bi[i, 1].reshape(-1)]),
        lam=rg_lambda[i],
        grg=out_g_rg[i].reshape(1, -1), gat=gat,
        wout_rg=wout[:RG_WIDTH].astype(BF16), wout_at=wout_at.astype(BF16),
        g2=norm2_g[i].reshape(1, -1),
        wqt=peer_wq[i].T.astype(BF16),
        subk=peer_subkeys[i].reshape(2 * PEER_HEADS, N_KEYS, PEER_HALF).astype(BF16),
        u=peer_u[i].astype(BF16),
        vt=peer_v[i].T.astype(BF16),
    )


def _rope_tables(length):
    pos = jnp.arange(length, dtype=F32)
    inv = ROPE_THETA ** (-jnp.arange(0, QK_ROPE, 2, dtype=F32) / QK_ROPE)
    ang = pos[:, None] * inv[None, :]
    cos, sin = jnp.cos(ang), jnp.sin(ang)
    ones = jnp.ones((length, QK_NOPE), F32)
    zpad = jnp.zeros((length, HEAD_PAD - QK_DIM), F32)
    cos_t = jnp.concatenate([ones, cos, cos, zpad], axis=-1)
    sin_t = jnp.concatenate([jnp.zeros((length, QK_NOPE), F32), -sin, sin, zpad], axis=-1)
    return cos_t, sin_t


def _padded_len(length):
    tile = math.lcm(ROW_TILE, ATTN_Q_TILE, PEER_TOKENS)
    return -(-length // tile) * tile


def _encode(x, meta_tokens, layers):
    batch, n_tok, _ = x.shape
    seq_len = n_tok + N_META
    lp = _padded_len(seq_len)
    rows = batch * lp
    meta = jnp.broadcast_to(meta_tokens[None].astype(x.dtype), (batch, N_META, D_MODEL))
    h = jnp.concatenate([meta, x, jnp.zeros((batch, lp - seq_len, D_MODEL), x.dtype)], axis=1)
    h = h.reshape(rows, D_MODEL)
    cos_t, sin_t = _rope_tables(lp)
    wide = N_HEADS * HEAD_PAD
    for w in layers:
        xr, gate, q, k, v = _inproj(h, cos_t, sin_t, w, rows=rows, pos_tiles=lp // ROW_TILE)
        hf, hb = _scan(xr, w, batch=batch, padded_len=lp, seq_len=seq_len)
        kt = k.reshape(batch, lp, N_HEADS, HEAD_PAD).transpose(0, 2, 3, 1)
        o = _attention(q.reshape(batch, lp, wide), kt, v.reshape(batch, lp, wide),
                       batch=batch, padded_len=lp, seq_len=seq_len)
        hn, xn2, sc = _outproj(hf, hb, gate, o.reshape(rows, wide), h, w, rows=rows)
        th, e1, e2 = _route(sc, rows=rows)
        h = _peer(xn2, hn, sc, th, e1, e2, w["u"], w["vt"], rows=rows)
    return h.reshape(batch, lp, D_MODEL)[:, N_META:seq_len]


def kernel(x_prompt, x_sample, meta_tokens, norm1_g, w_in, conv_w, conv_b, rg_wa, rg_ba, rg_wi, rg_bi, rg_lambda, q_norm_g, w_uq, kv_norm_g, w_ukv, q_head_g, k_head_g, out_g_rg, out_g_attn, w_out, norm2_g, peer_wq, peer_subkeys, peer_u, peer_v):
    params = (norm1_g, w_in, conv_w, conv_b, rg_wa, rg_ba, rg_wi, rg_bi, rg_lambda, q_norm_g, w_uq, kv_norm_g,
              w_ukv, q_head_g, k_head_g, out_g_rg, out_g_attn, w_out, norm2_g, peer_wq, peer_subkeys, peer_u,
              peer_v)
    layers = [_layer_weights(i, *params) for i in range(norm1_g.shape[0])]
    y_prompt = _encode(x_prompt, meta_tokens, layers)
    y_sample = _encode(x_sample, meta_tokens, layers)
    return (y_prompt, y_sample)
```

```python
import functools
import math

import jax
import jax.numpy as jnp
from jax import lax
from jax.experimental import pallas as pl
from jax.experimental.pallas import tpu as pltpu

F32 = jnp.float32
BF16 = jnp.bfloat16

D_MODEL = 1024
N_META = 16
EPS = 1e-6
RG_WIDTH = 512
RG_HEADS = 8
RG_BLOCK = 64
CONV_W = 4
RG_C = 8.0
N_HEADS = 8
QK_NOPE = 64
QK_ROPE = 32
QK_DIM = 96
V_DIM = 64
Q_LORA = 256
KV_LORA = 128
ROPE_THETA = 10000.0
P_IN = 2 * RG_WIDTH + Q_LORA + KV_LORA + QK_ROPE
PEER_HEADS = 8
N_KEYS = 128
N_EXPERTS = N_KEYS * N_KEYS
PEER_TOPK = 16
PEER_HALF = 128

LANES = 128
SUBLANES = 8
HEAD_PAD = LANES
P_IN_PAD = 12 * LANES
ROW_TILE = 256
ATTN_Q_TILE = 512
KV_CHUNK = 1024
PEER_TOKENS = 512
PEER_I1_BLOCK = 8
VMEM_LIMIT = 52 * 1024 * 1024
NEG_INF = float("-inf")
MASK_VALUE = -1e30
LOG2E = 1.4426950408889634


def _cparams(*sem):
    return pltpu.CompilerParams(dimension_semantics=sem, vmem_limit_bytes=VMEM_LIMIT)


def _rms(x, g):
    return x * lax.rsqrt(jnp.mean(x * x, axis=-1, keepdims=True) + EPS) * g


def _gelu_tanh(x):
    return x * (0.5 * (1.0 + jnp.tanh(0.7978845608028654 * (x + 0.044715 * (x * x * x)))))


def _sigmoid(x):
    return 1.0 / (1.0 + jnp.exp(-x))


def _neg_expm1(y):
    u = jnp.exp(y)
    um1 = u - 1.0
    tiny = um1 == 0.0
    r = jnp.where(tiny, y, um1 * y / jnp.where(tiny, 1.0, jnp.log(u)))
    return -jnp.where(um1 == -1.0, -1.0, r)


def _dot(a, b):
    return jnp.dot(a, b, preferred_element_type=F32)


def _dot_nt(a, b):
    return lax.dot_general(a, b, (((1,), (1,)), ((), ())), preferred_element_type=F32)


def _full(shape):
    n = len(shape)
    return pl.BlockSpec(shape, lambda *_: (0,) * n)


def _inproj_body(x_ref, cos_ref, sin_ref, g1_ref, win_ref, qg_ref, wq_ref, wqs_ref, kvg_ref,
                 wk_ref, ek_ref, eks_ref, wv_ref, gq_ref, gqs_ref, gk_ref, gks_ref,
                 xr_ref, gate_ref, q_ref, k_ref, v_ref):
    xn = _rms(x_ref[...], g1_ref[...])
    p = _dot(xn.astype(BF16), win_ref[...])
    xr_ref[...] = p[:, :RG_WIDTH]
    gate_ref[...] = p[:, RG_WIDTH:2 * RG_WIDTH]
    o2 = 2 * RG_WIDTH
    qcn = _rms(p[:, o2:o2 + Q_LORA], qg_ref[...]).astype(BF16)
    kvn = _rms(p[:, o2 + Q_LORA:o2 + Q_LORA + KV_LORA], kvg_ref[...]).astype(BF16)
    kpe = p[:, o2 + Q_LORA + KV_LORA:].astype(BF16)
    q = _dot(qcn, wq_ref[...])
    qs = _dot(qcn, wqs_ref[...])
    k = _dot(kvn, wk_ref[...]) + _dot(kpe, ek_ref[...])
    ks = _dot(kpe, eks_ref[...])
    v_ref[...] = _dot(kvn, wv_ref[...]).astype(BF16)
    c = cos_ref[...]
    s = sin_ref[...]
    qscale = QK_DIM ** -0.5 * LOG2E
    for h in range(N_HEADS):
        sl = slice(h * HEAD_PAD, (h + 1) * HEAD_PAD)
        qh = q[:, sl]
        rq = lax.rsqrt(jnp.sum(qh * qh, axis=-1, keepdims=True) * (1.0 / QK_DIM) + EPS)
        q_ref[:, sl] = ((qh * gq_ref[...] * c + qs[:, sl] * gqs_ref[...] * s) * (rq * qscale)).astype(BF16)
        kh = k[:, sl]
        rk = lax.rsqrt(jnp.sum(kh * kh, axis=-1, keepdims=True) * (1.0 / QK_DIM) + EPS)
        k_ref[:, sl] = ((kh * gk_ref[...] * c + ks[:, sl] * gks_ref[...] * s) * rk).astype(BF16)


def _inproj(h, cos_t, sin_t, w, *, rows, pos_tiles):
    tm = ROW_TILE
    row = lambda n: pl.BlockSpec((tm, n), lambda i: (i, 0))
    pos = pl.BlockSpec((tm, HEAD_PAD), lambda i: (i % pos_tiles, 0))
    wide = N_HEADS * HEAD_PAD
    return pl.pallas_call(
        _inproj_body,
        grid=(rows // tm,),
        in_specs=[row(D_MODEL), pos, pos, _full((1, D_MODEL)), _full((D_MODEL, P_IN_PAD)),
                  _full((1, Q_LORA)), _full((Q_LORA, wide)), _full((Q_LORA, wide)),
                  _full((1, KV_LORA)), _full((KV_LORA, wide)), _full((LANES, wide)), _full((LANES, wide)),
                  _full((KV_LORA, wide)),
                  _full((1, HEAD_PAD)), _full((1, HEAD_PAD)), _full((1, HEAD_PAD)), _full((1, HEAD_PAD))],
        out_specs=[row(RG_WIDTH), row(RG_WIDTH), row(wide), row(wide), row(wide)],
        out_shape=[jax.ShapeDtypeStruct((rows, RG_WIDTH), F32), jax.ShapeDtypeStruct((rows, RG_WIDTH), F32),
                   jax.ShapeDtypeStruct((rows, wide), BF16), jax.ShapeDtypeStruct((rows, wide), BF16),
                   jax.ShapeDtypeStruct((rows, wide), BF16)],
        compiler_params=_cparams("parallel"),
        name="inproj",
    )(h, cos_t, sin_t, w["g1"], w["win"], w["qg"], w["wq"], w["wqs"], w["kvg"], w["wk"], w["ek"], w["eks"],
      w["wv"], w["gq"], w["gqs"], w["gk"], w["gks"])


def _scan_body(xm_f, xp_f, xn_f, xm_b, xp_b, xn_b, cw_ref, cb_ref, wf_ref, wb_ref, bias_ref, lam_ref,
               hf_ref, hb_ref, ext_s, a_s, b_s, cf_s, cb_s, *, seq_len, chunk, n_chunks):
    i = pl.program_id(1)
    C = chunk
    groups = C // SUBLANES

    @pl.when(i == 0)
    def _():
        cf_s[...] = jnp.zeros_like(cf_s)
        cb_s[...] = jnp.zeros_like(cb_s)

    rows8 = lax.broadcasted_iota(jnp.int32, (SUBLANES, RG_WIDTH), 0)
    rows_c = lax.broadcasted_iota(jnp.int32, (C, RG_WIDTH), 0)
    rmod = rows_c & (SUBLANES - 1)

    def masked(x, pos):
        return jnp.where((pos >= 0) & (pos < seq_len), x, 0.0)

    def affine_terms(xm, xp, xn, j, w_ref, d):
        base = j * C
        ext_s[0:SUBLANES, :] = masked(xp[...], base - SUBLANES + rows8)
        ext_s[SUBLANES:SUBLANES + C, :] = masked(xm[...], base + rows_c)
        ext_s[SUBLANES + C:, :] = masked(xn[...], base + C + rows8)
        xc = cb_ref[...]
        for t in range(CONV_W):
            xc = xc + cw_ref[t:t + 1, :] * ext_s[SUBLANES - 2 + t:SUBLANES - 2 + t + C, :]
        gm = _dot(xc.astype(BF16), w_ref[...])
        r = _sigmoid(gm[:, :RG_WIDTH] + bias_ref[2 * d:2 * d + 1, :])
        ig = _sigmoid(gm[:, RG_WIDTH:] + bias_ref[2 * d + 1:2 * d + 2, :])
        z = -lam_ref[d:d + 1, :]
        softplus = jnp.maximum(z, 0.0) + jnp.log1p(jnp.exp(-jnp.abs(z)))
        log_a = (-RG_C) * r * softplus
        a = jnp.exp(log_a)
        b = jnp.sqrt(_neg_expm1(2.0 * log_a)) * (ig * xc)
        b = jnp.where(base + rows_c < seq_len, b, 0.0)
        return a, b

    a, b = affine_terms(xm_f, xp_f, xn_f, i, wf_ref, 0)
    for d in (1, 2, 4):
        m = rmod >= d
        a_sh = pltpu.roll(a, d, 0)
        b_sh = pltpu.roll(b, d, 0)
        b = jnp.where(m, a * b_sh + b, b)
        a = jnp.where(m, a * a_sh, a)
    a_s[...] = a
    b_s[...] = b

    def fwd_group(g, carry):
        r0 = pl.multiple_of(g * SUBLANES, SUBLANES)
        hrow = a_s[pl.ds(r0, SUBLANES), :] * carry + b_s[pl.ds(r0, SUBLANES), :]
        hf_ref[pl.ds(r0, SUBLANES), :] = hrow
        return jnp.broadcast_to(hrow[SUBLANES - 1:SUBLANES, :], (SUBLANES, RG_WIDTH))

    cf_s[...] = lax.fori_loop(0, groups, fwd_group, cf_s[...])

    a, b = affine_terms(xm_b, xp_b, xn_b, n_chunks - 1 - i, wb_ref, 1)
    for d in (1, 2, 4):
        m = rmod < SUBLANES - d
        a_sh = pltpu.roll(a, C - d, 0)
        b_sh = pltpu.roll(b, C - d, 0)
        b = jnp.where(m, a * b_sh + b, b)
        a = jnp.where(m, a * a_sh, a)
    a_s[...] = a
    b_s[...] = b

    def bwd_group(g, carry):
        r0 = pl.multiple_of((groups - 1 - g) * SUBLANES, SUBLANES)
        hrow = a_s[pl.ds(r0, SUBLANES), :] * carry + b_s[pl.ds(r0, SUBLANES), :]
        hb_ref[pl.ds(r0, SUBLANES), :] = hrow
        return jnp.broadcast_to(hrow[0:1, :], (SUBLANES, RG_WIDTH))

    cb_s[...] = lax.fori_loop(0, groups, bwd_group, cb_s[...])


def _scan(xr, w, *, batch, padded_len, seq_len):
    C = ROW_TILE
    n = padded_len // C
    per8 = C // SUBLANES
    last8 = batch * padded_len // SUBLANES - 1

    def chunk_f(b, i):
        return b * n + i

    def chunk_b(b, i):
        return b * n + (n - 1 - i)

    def specs(chunk_of):
        main = pl.BlockSpec((C, RG_WIDTH), lambda b, i: (chunk_of(b, i), 0))
        prev = pl.BlockSpec((SUBLANES, RG_WIDTH), lambda b, i: (jnp.maximum(chunk_of(b, i) * per8 - 1, 0), 0))
        nxt = pl.BlockSpec((SUBLANES, RG_WIDTH),
                           lambda b, i: (jnp.minimum((chunk_of(b, i) + 1) * per8, last8), 0))
        return [main, prev, nxt]

    rows = batch * padded_len
    return pl.pallas_call(
        functools.partial(_scan_body, seq_len=seq_len, chunk=C, n_chunks=n),
        grid=(batch, n),
        in_specs=specs(chunk_f) + specs(chunk_b) + [
            _full((CONV_W, RG_WIDTH)), _full((1, RG_WIDTH)),
            _full((RG_WIDTH, 2 * RG_WIDTH)), _full((RG_WIDTH, 2 * RG_WIDTH)),
            _full((4, RG_WIDTH)), _full((2, RG_WIDTH))],
        out_specs=[pl.BlockSpec((C, RG_WIDTH), lambda b, i: (chunk_f(b, i), 0)),
                   pl.BlockSpec((C, RG_WIDTH), lambda b, i: (chunk_b(b, i), 0))],
        out_shape=[jax.ShapeDtypeStruct((rows, RG_WIDTH), F32), jax.ShapeDtypeStruct((rows, RG_WIDTH), F32)],
        scratch_shapes=[pltpu.VMEM((C + 2 * SUBLANES, RG_WIDTH), F32), pltpu.VMEM((C, RG_WIDTH), F32),
                        pltpu.VMEM((C, RG_WIDTH), F32), pltpu.VMEM((SUBLANES, RG_WIDTH), F32),
                        pltpu.VMEM((SUBLANES, RG_WIDTH), F32)],
        compiler_params=_cparams("arbitrary", "arbitrary"),
        name="rglru_scan",
    )(xr, xr, xr, xr, xr, xr, w["conv_w"], w["conv_b"], w["wgate_f"], w["wgate_b"], w["gate_bias"], w["lam"])


def _attn_body(q_ref, kt_ref, v_ref, o_ref, s_a, s_b, m_s, l_s, acc_s, *, seq_len, n_full, rem):
    tq = q_ref.shape[1]
    m_s[...] = jnp.full_like(m_s, NEG_INF)
    l_s[...] = jnp.zeros_like(l_s)
    acc_s[...] = jnp.zeros_like(acc_s)

    def scores(dst, start, width):
        dst[:, :width] = _dot(q_ref[0], kt_ref[0, 0, :, pl.ds(start, width)])

    def update(src, start, width, mask):
        s = src[:, :width]
        if mask:
            kpos = start + lax.broadcasted_iota(jnp.int32, (tq, width), 1)
            s = jnp.where(kpos < seq_len, s, MASK_VALUE)
        m = m_s[...]
        m_new = jnp.maximum(m, jnp.max(s, axis=-1, keepdims=True))
        alpha = jnp.exp2(m - m_new)
        p = jnp.exp2(s - m_new)
        m_s[...] = m_new
        l_s[...] = alpha * l_s[...] + jnp.sum(p, axis=-1, keepdims=True)
        acc_s[...] = alpha * acc_s[...] + _dot(p.astype(BF16), v_ref[0, pl.ds(start, width), :])

    if rem:
        scores(s_b, n_full * KV_CHUNK, rem)
    if n_full:
        scores(s_a, 0, KV_CHUNK)
    if rem:
        update(s_b, n_full * KV_CHUNK, rem, True)
    pairs = max((n_full - 1) // 2, 0)
    tail = [(c * KV_CHUNK, KV_CHUNK, False) for c in range(2 * pairs, n_full)]

    def pair(i, carry):
        c0 = pl.multiple_of(2 * i * KV_CHUNK, KV_CHUNK)
        scores(s_b, c0 + KV_CHUNK, KV_CHUNK)
        update(s_a, c0, KV_CHUNK, False)
        scores(s_a, c0 + 2 * KV_CHUNK, KV_CHUNK)
        update(s_b, c0 + KV_CHUNK, KV_CHUNK, False)
        return carry

    lax.fori_loop(0, pairs, pair, 0)
    bufs = (s_a, s_b)
    for n, (start, width, mask) in enumerate(tail):
        if n + 1 < len(tail):
            scores(bufs[(n + 1) % 2], tail[n + 1][0], tail[n + 1][1])
        update(bufs[n % 2], start, width, mask)
    o_ref[0] = acc_s[...] / l_s[...]


def _attention(q, kt, v, *, batch, padded_len, seq_len):
    tq = ATTN_Q_TILE
    n_full = seq_len // KV_CHUNK
    rem = -(-(seq_len - n_full * KV_CHUNK) // LANES) * LANES
    wide = N_HEADS * HEAD_PAD
    return pl.pallas_call(
        functools.partial(_attn_body, seq_len=seq_len, n_full=n_full, rem=rem),
        grid=(batch, N_HEADS, padded_len // tq),
        in_specs=[pl.BlockSpec((1, tq, HEAD_PAD), lambda b, h, i: (b, i, h)),
                  pl.BlockSpec((1, 1, HEAD_PAD, padded_len), lambda b, h, i: (b, h, 0, 0)),
                  pl.BlockSpec((1, padded_len, HEAD_PAD), lambda b, h, i: (b, 0, h))],
        out_specs=pl.BlockSpec((1, tq, HEAD_PAD), lambda b, h, i: (b, i, h)),
        out_shape=jax.ShapeDtypeStruct((batch, padded_len, wide), F32),
        scratch_shapes=[pltpu.VMEM((tq, KV_CHUNK), F32), pltpu.VMEM((tq, KV_CHUNK), F32),
                        pltpu.VMEM((tq, 1), F32), pltpu.VMEM((tq, 1), F32), pltpu.VMEM((tq, HEAD_PAD), F32)],
        compiler_params=_cparams("parallel", "parallel", "arbitrary"),
        name="attention",
    )(q, kt, v)


def _outproj_body(hf_ref, hb_ref, gate_ref, o_ref, h_ref, grg_ref, gat_ref, wrg_ref, wat_ref, g2_ref,
                  wqt_ref, sk_ref, hn_ref, xn2_ref, sc_ref):
    rg = (hf_ref[...] + hb_ref[...]) * _gelu_tanh(gate_ref[...])
    rgn = _rms(rg, grg_ref[...])
    o = o_ref[...]
    ms = jnp.sum(o * o, axis=-1, keepdims=True) * (1.0 / (N_HEADS * V_DIM))
    attn = o * lax.rsqrt(ms + EPS) * gat_ref[...]
    hn = h_ref[...] + _dot(rgn.astype(BF16), wrg_ref[...]) + _dot(attn.astype(BF16), wat_ref[...])
    hn_ref[...] = hn
    xn2 = _rms(hn, g2_ref[...]).astype(BF16)
    xn2_ref[...] = xn2
    qpt = _dot_nt(wqt_ref[...], xn2)
    for h in range(PEER_HEADS):
        for s in range(2):
            r0 = (h * 2 + s) * PEER_HALF
            blk = qpt[r0:r0 + PEER_HALF, :].astype(BF16)
            o0 = s * PEER_HEADS * N_KEYS + h * N_KEYS
            sc_ref[o0:o0 + N_KEYS, :] = _dot(sk_ref[h * 2 + s], blk)


def _outproj(hf, hb, gate, o, h, w, *, rows):
    tm = ROW_TILE
    row = lambda n: pl.BlockSpec((tm, n), lambda i: (i, 0))
    wide = N_HEADS * HEAD_PAD
    nsc = 2 * PEER_HEADS * N_KEYS
    return pl.pallas_call(
        _outproj_body,
        grid=(rows // tm,),
        in_specs=[row(RG_WIDTH), row(RG_WIDTH), row(RG_WIDTH), row(wide), row(D_MODEL),
                  _full((1, RG_WIDTH)), _full((1, wide)), _full((RG_WIDTH, D_MODEL)), _full((wide, D_MODEL)),
                  _full((1, D_MODEL)), _full((nsc, D_MODEL)), _full((2 * PEER_HEADS, N_KEYS, PEER_HALF))],
        out_specs=[row(D_MODEL), row(D_MODEL), pl.BlockSpec((nsc, tm), lambda i: (0, i))],
        out_shape=[jax.ShapeDtypeStruct((rows, D_MODEL), F32), jax.ShapeDtypeStruct((rows, D_MODEL), BF16),
                   jax.ShapeDtypeStruct((nsc, rows), F32)],
        compiler_params=_cparams("parallel"),
        name="outproj",
    )(hf, hb, gate, o, h, w["grg"], w["gat"], w["wout_rg"], w["wout_at"], w["g2"], w["wqt"], w["subk"])


def _top_rows(x, count):
    rows = []
    for r in range(count):
        m = jnp.max(x, axis=0, keepdims=True)
        rows.append(m)
        if r + 1 < count:
            x = jnp.where(x == m, NEG_INF, x)
    return rows


def _route_body(sc_ref, th_ref, e1_ref, s2_ref, e2_ref):
    t = sc_ref.shape[1]
    half = PEER_HEADS * N_KEYS
    jrow = lax.broadcasted_iota(jnp.int32, (SUBLANES, t), 0)

    def bc(row):
        return jnp.broadcast_to(row, (SUBLANES, t))

    def head(h, carry):
        r0 = pl.multiple_of(h * N_KEYS, N_KEYS)
        s1 = sc_ref[pl.ds(r0, N_KEYS), :]
        s2 = sc_ref[pl.ds(half + r0, N_KEYS), :]
        a = _top_rows(s1, PEER_TOPK + 1)
        b = _top_rows(s2, PEER_TOPK + 1)
        b_lo = jnp.concatenate(b[:SUBLANES], axis=0)
        b_hi = jnp.concatenate(b[SUBLANES:2 * SUBLANES], axis=0)
        a_hi = jnp.concatenate(a[SUBLANES:2 * SUBLANES], axis=0)
        pieces = [bc(a[0]) + b_lo, bc(a[0]) + b_hi, bc(a[1]) + b_lo]
        for i in range(2, SUBLANES):
            pieces.append(jnp.where(jrow < (PEER_TOPK + 1) // (i + 1), bc(a[i]) + b_lo, NEG_INF))
        pieces.append(a_hi + bc(b[0]))
        ends = jnp.concatenate([a[0] + b[PEER_TOPK], a[PEER_TOPK] + b[0]]
                               + [jnp.full((1, t), NEG_INF, F32)] * (SUBLANES - 2), axis=0)
        pieces.append(ends)
        cand = jnp.concatenate(pieces, axis=0)
        x = cand
        for _ in range(PEER_TOPK - 1):
            x = jnp.where(x == jnp.max(x, axis=0, keepdims=True), NEG_INF, x)
        c16 = jnp.max(x, axis=0, keepdims=True)
        c17 = jnp.max(jnp.where(x == c16, NEG_INF, x), axis=0, keepdims=True)
        thr = 0.5 * (c16 + c17)
        top = a[0] + b[0]
        z = jnp.sum(jnp.where(cand >= thr, jnp.exp(cand - top), 0.0), axis=0, keepdims=True)
        hrows = pl.ds(pl.multiple_of(h * PEER_I1_BLOCK, PEER_I1_BLOCK), PEER_I1_BLOCK)
        blocks = (N_KEYS // PEER_I1_BLOCK, PEER_I1_BLOCK, t)
        th_ref[0, :, hrows, :] = (thr - s1).reshape(blocks)
        e1_ref[0, :, hrows, :] = (jnp.exp(s1 - a[0]) / z).reshape(blocks)
        s2_ref[0, pl.ds(r0, N_KEYS), :] = s2
        e2_ref[0, pl.ds(r0, N_KEYS), :] = jnp.exp(s2 - b[0])
        return carry

    lax.fori_loop(0, PEER_HEADS, head, 0)


def _route(sc, *, rows):
    t = LANES
    half = PEER_HEADS * N_KEYS
    nblk = N_KEYS // PEER_I1_BLOCK
    per_blk = PEER_HEADS * PEER_I1_BLOCK
    first = pl.BlockSpec((1, nblk, per_blk, t), lambda i: (i, 0, 0, 0))
    first_shape = jax.ShapeDtypeStruct((rows // t, nblk, per_blk, t), F32)
    second = pl.BlockSpec((1, half, t), lambda i: (i, 0, 0))
    second_shape = jax.ShapeDtypeStruct((rows // t, half, t), F32)
    return pl.pallas_call(
        _route_body,
        grid=(rows // t,),
        in_specs=[pl.BlockSpec((2 * half, t), lambda i: (0, i))],
        out_specs=[first, first, second, second],
        out_shape=[first_shape, first_shape, second_shape, second_shape],
        compiler_params=_cparams("parallel"),
        name="peer_route",
    )(sc)


def _peer_body(xn_ref, h_ref, s2_ref, th_ref, e1_ref, e2_ref, u_ref, vt_ref, out_ref, acc_s, hid_s, w_s):
    j = pl.program_id(1)
    n_lt = s2_ref.shape[0]
    pair = 2 * LANES

    @pl.when(j == 0)
    def _():
        acc_s[...] = jnp.zeros_like(acc_s)

    for k in range(n_lt // 2):
        res = _dot_nt(u_ref[...], xn_ref[k * pair:(k + 1) * pair, :])
        hid_s[2 * k] = res[:, :LANES]
        hid_s[2 * k + 1] = res[:, LANES:]

    for il in range(PEER_I1_BLOCK):
        es = slice(il * N_KEYS, (il + 1) * N_KEYS)
        for lt in range(n_lt):
            g = jnp.zeros((N_KEYS, LANES), F32)
            for h in range(PEER_HEADS):
                hs = slice(h * N_KEYS, (h + 1) * N_KEYS)
                r = h * PEER_I1_BLOCK + il
                th = th_ref[lt, 0, r:r + 1, :]
                e1 = e1_ref[lt, 0, r:r + 1, :]
                g = g + jnp.where(s2_ref[lt, hs, :] >= th, e2_ref[lt, hs, :], 0.0) * e1
            w_s[lt, es, :] = (g * _gelu_tanh(hid_s[lt, es, :])).astype(BF16)

    for k in range(n_lt // 2):
        w = jnp.concatenate([w_s[2 * k], w_s[2 * k + 1]], axis=1)
        res = _dot(vt_ref[...], w)
        acc_s[2 * k] += res[:, :LANES]
        acc_s[2 * k + 1] += res[:, LANES:]

    @pl.when(j == pl.num_programs(1) - 1)
    def _():
        for lt in range(n_lt):
            tok = slice(lt * LANES, (lt + 1) * LANES)
            out_ref[tok, :] = h_ref[tok, :] + acc_s[lt].T


def _peer(xn2, hn, th, e1, s2, e2, u, vt, *, rows):
    tt = PEER_TOKENS
    n_lt = tt // LANES
    eb = PEER_I1_BLOCK * N_KEYS
    nb = N_EXPERTS // eb
    half = PEER_HEADS * N_KEYS
    per_blk = PEER_HEADS * PEER_I1_BLOCK
    tok = pl.BlockSpec((tt, D_MODEL), lambda i, j: (i, 0))
    second = pl.BlockSpec((n_lt, half, LANES), lambda i, j: (i, 0, 0))
    first = pl.BlockSpec((n_lt, 1, per_blk, LANES), lambda i, j: (i, j, 0, 0))
    return pl.pallas_call(
        _peer_body,
        grid=(rows // tt, nb),
        in_specs=[tok, tok, second, first, first, second,
                  pl.BlockSpec((eb, D_MODEL), lambda i, j: (j, 0)),
                  pl.BlockSpec((D_MODEL, eb), lambda i, j: (0, j))],
        out_specs=pl.BlockSpec((tt, D_MODEL), lambda i, j: (i, 0)),
        out_shape=jax.ShapeDtypeStruct((rows, D_MODEL), F32),
        scratch_shapes=[pltpu.VMEM((n_lt, D_MODEL, LANES), F32), pltpu.VMEM((n_lt, eb, LANES), F32),
                        pltpu.VMEM((n_lt, eb, LANES), BF16)],
        compiler_params=_cparams("parallel", "arbitrary"),
        name="peer_dense",
    )(xn2, hn, s2, th, e1, e2, u, vt)


def _block_diag(w):
    eye = jnp.eye(RG_HEADS, dtype=w.dtype)
    return jnp.einsum("hij,hg->higj", w, eye).reshape(RG_WIDTH, RG_WIDTH)


def _head_pad(w, lo, hi):
    k = w.shape[0]
    part = w[:, :, lo:hi]
    return jnp.pad(part, ((0, 0), (0, 0), (0, HEAD_PAD - (hi - lo)))).reshape(k, N_HEADS * HEAD_PAD)


def _rope_partner(x1, x2):
    z64 = jnp.zeros(x1.shape[:-1] + (QK_NOPE,), x1.dtype)
    z32 = jnp.zeros(x1.shape[:-1] + (HEAD_PAD - QK_DIM,), x1.dtype)
    return jnp.concatenate([z64, x2, x1, z32], axis=-1)


def _layer_weights(i, norm1_g, w_in, conv_w, conv_b, rg_wa, rg_ba, rg_wi, rg_bi, rg_lambda, q_norm_g, w_uq,
                   kv_norm_g, w_ukv, q_head_g, k_head_g, out_g_rg, out_g_attn, w_out, norm2_g, peer_wq,
                   peer_subkeys, peer_u, peer_v):
    half = QK_ROPE // 2
    wq3 = w_uq[i].reshape(Q_LORA, N_HEADS, QK_DIM)
    wkv3 = w_ukv[i].reshape(KV_LORA, N_HEADS, QK_NOPE + V_DIM)
    eye = jnp.eye(QK_ROPE, dtype=F32)
    ek = jnp.concatenate([jnp.zeros((QK_ROPE, QK_NOPE), F32), eye,
                          jnp.zeros((QK_ROPE, HEAD_PAD - QK_DIM), F32)], axis=-1)
    eks = _rope_partner(eye[:, :half], eye[:, half:])
    place = lambda e: jnp.pad(jnp.tile(e, (1, N_HEADS)), ((0, LANES - QK_ROPE), (0, 0))).astype(BF16)

    def head_gain(g):
        main = jnp.pad(g, (0, HEAD_PAD - QK_DIM)).reshape(1, HEAD_PAD)
        swap = _rope_partner(g[QK_NOPE:QK_NOPE + half], g[QK_NOPE + half:]).reshape(1, HEAD_PAD)
        return main, swap

    gq, gqs = head_gain(q_head_g[i])
    gk, gks = head_gain(k_head_g[i])
    wout = w_out[i]
    wout_at = jnp.pad(wout[RG_WIDTH:].reshape(N_HEADS, V_DIM, D_MODEL),
                      ((0, 0), (0, HEAD_PAD - V_DIM), (0, 0))).reshape(N_HEADS * HEAD_PAD, D_MODEL)
    gat = jnp.pad(out_g_attn[i].reshape(N_HEADS, V_DIM), ((0, 0), (0, HEAD_PAD - V_DIM))).reshape(1, -1)
    return dict(
        g1=norm1_g[i].reshape(1, -1),
        win=jnp.pad(w_in[i], ((0, 0), (0, P_IN_PAD - P_IN))).astype(BF16),
        qg=q_norm_g[i].reshape(1, -1),
        wq=_head_pad(wq3, 0, QK_DIM).astype(BF16),
        wqs=_rope_partner(wq3[:, :, QK_NOPE:QK_NOPE + half], wq3[:, :, QK_NOPE + half:])
        .reshape(Q_LORA, -1).astype(BF16),
        kvg=kv_norm_g[i].reshape(1, -1),
        wk=_head_pad(wkv3, 0, QK_NOPE).astype(BF16),
        wv=_head_pad(wkv3, QK_NOPE, QK_NOPE + V_DIM).astype(BF16),
        ek=place(ek), eks=place(eks), gq=gq, gqs=gqs, gk=gk, gks=gks,
        conv_w=conv_w[i], conv_b=conv_b[i].reshape(1, -1),
        wgate_f=jnp.concatenate([_block_diag(rg_wa[i, 0]), _block_diag(rg_wi[i, 0])], axis=1).astype(BF16),
        wgate_b=jnp.concatenate([_block_diag(rg_wa[i, 1]), _block_diag(rg_wi[i, 1])], axis=1).astype(BF16),
        gate_bias=jnp.stack([rg_ba[i, 0].reshape(-1), rg_bi[i, 0].reshape(-1),
                             rg_ba[i, 1].reshape(-1), rg_bi[i, 1].reshape(-1)]),
        lam=rg_lambda[i],
        grg=out_g_rg[i].reshape(1, -1), gat=gat,
        wout_rg=wout[:RG_WIDTH].astype(BF16), wout_at=wout_at.astype(BF16),
        g2=norm2_g[i].reshape(1, -1),
        wqt=peer_wq[i].T.astype(BF16),
        subk=peer_subkeys[i].reshape(2 * PEER_HEADS, N_KEYS, PEER_HALF).astype(BF16),
        u=peer_u[i].astype(BF16),
        vt=peer_v[i].T.astype(BF16),
    )


def _rope_tables(length):
    pos = jnp.arange(length, dtype=F32)
    inv = ROPE_THETA ** (-jnp.arange(0, QK_ROPE, 2, dtype=F32) / QK_ROPE)
    ang = pos[:, None] * inv[None, :]
    cos, sin = jnp.cos(ang), jnp.sin(ang)
    ones = jnp.ones((length, QK_NOPE), F32)
    zpad = jnp.zeros((length, HEAD_PAD - QK_DIM), F32)
    cos_t = jnp.concatenate([ones, cos, cos, zpad], axis=-1)
    sin_t = jnp.concatenate([jnp.zeros((length, QK_NOPE), F32), -sin, sin, zpad], axis=-1)
    return cos_t, sin_t


def _padded_len(length):
    tile = math.lcm(ROW_TILE, ATTN_Q_TILE, PEER_TOKENS)
    return -(-length // tile) * tile


def _encode(x, meta_tokens, layers):
    batch, n_tok, _ = x.shape
    seq_len = n_tok + N_META
    lp = _padded_len(seq_len)
    rows = batch * lp
    meta = jnp.broadcast_to(meta_tokens[None].astype(x.dtype), (batch, N_META, D_MODEL))
    h = jnp.concatenate([meta, x, jnp.zeros((batch, lp - seq_len, D_MODEL), x.dtype)], axis=1)
    h = h.reshape(rows, D_MODEL)
    cos_t, sin_t = _rope_tables(lp)
    wide = N_HEADS * HEAD_PAD
    for w in layers:
        xr, gate, q, k, v = _inproj(h, cos_t, sin_t, w, rows=rows, pos_tiles=lp // ROW_TILE)
        hf, hb = _scan(xr, w, batch=batch, padded_len=lp, seq_len=seq_len)
        kt = k.reshape(batch, lp, N_HEADS, HEAD_PAD).transpose(0, 2, 3, 1)
        o = _attention(q.reshape(batch, lp, wide), kt, v.reshape(batch, lp, wide),
                       batch=batch, padded_len=lp, seq_len=seq_len)
        hn, xn2, sc = _outproj(hf, hb, gate, o.reshape(rows, wide), h, w, rows=rows)
        th, e1, s2, e2 = _route(sc, rows=rows)
        h = _peer(xn2, hn, th, e1, s2, e2, w["u"], w["vt"], rows=rows)
    return h.reshape(batch, lp, D_MODEL)[:, N_META:seq_len]


def kernel(x_prompt, x_sample, meta_tokens, norm1_g, w_in, conv_w, conv_b, rg_wa, rg_ba, rg_wi, rg_bi, rg_lambda, q_norm_g, w_uq, kv_norm_g, w_ukv, q_head_g, k_head_g, out_g_rg, out_g_attn, w_out, norm2_g, peer_wq, peer_subkeys, peer_u, peer_v):
    params = (norm1_g, w_in, conv_w, conv_b, rg_wa, rg_ba, rg_wi, rg_bi, rg_lambda, q_norm_g, w_uq, kv_norm_g,
              w_ukv, q_head_g, k_head_g, out_g_rg, out_g_attn, w_out, norm2_g, peer_wq, peer_subkeys, peer_u,
              peer_v)
    layers = [_layer_weights(i, *params) for i in range(norm1_g.shape[0])]
    y_prompt = _encode(x_prompt, meta_tokens, layers)
    y_sample = _encode(x_sample, meta_tokens, layers)
    return (y_prompt, y_sample)
```

```python
import functools
import math

import jax
import jax.numpy as jnp
from jax import lax
from jax.experimental import pallas as pl
from jax.experimental.pallas import tpu as pltpu

F32 = jnp.float32
BF16 = jnp.bfloat16

D_MODEL = 1024
N_META = 16
EPS = 1e-6
RG_WIDTH = 512
RG_HEADS = 8
RG_BLOCK = 64
CONV_W = 4
RG_C = 8.0
N_HEADS = 8
QK_NOPE = 64
QK_ROPE = 32
QK_DIM = 96
V_DIM = 64
Q_LORA = 256
KV_LORA = 128
ROPE_THETA = 10000.0
P_IN = 2 * RG_WIDTH + Q_LORA + KV_LORA + QK_ROPE
PEER_HEADS = 8
N_KEYS = 128
N_EXPERTS = N_KEYS * N_KEYS
PEER_TOPK = 16
PEER_HALF = 128

LANES = 128
SUBLANES = 8
HEAD_PAD = LANES
P_IN_PAD = 12 * LANES
ROW_TILE = 256
ATTN_Q_TILE = 512
KV_CHUNK = 1024
PEER_TOKENS = 512
ROUTE_TOKENS = 256
PEER_I1_BLOCK = 8
VMEM_LIMIT = 52 * 1024 * 1024
NEG_INF = float("-inf")
MASK_VALUE = -1e30
LOG2E = 1.4426950408889634


def _cparams(*sem):
    return pltpu.CompilerParams(dimension_semantics=sem, vmem_limit_bytes=VMEM_LIMIT)


def _rms(x, g):
    return x * lax.rsqrt(jnp.mean(x * x, axis=-1, keepdims=True) + EPS) * g


def _gelu_tanh(x):
    return x * (0.5 * (1.0 + jnp.tanh(0.7978845608028654 * (x + 0.044715 * (x * x * x)))))


def _sigmoid(x):
    return 1.0 / (1.0 + jnp.exp(-x))


def _neg_expm1(y):
    u = jnp.exp(y)
    um1 = u - 1.0
    tiny = um1 == 0.0
    r = jnp.where(tiny, y, um1 * y / jnp.where(tiny, 1.0, jnp.log(u)))
    return -jnp.where(um1 == -1.0, -1.0, r)


def _dot(a, b):
    return jnp.dot(a, b, preferred_element_type=F32)


def _dot_nt(a, b):
    return lax.dot_general(a, b, (((1,), (1,)), ((), ())), preferred_element_type=F32)


def _full(shape):
    n = len(shape)
    return pl.BlockSpec(shape, lambda *_: (0,) * n)


def _inproj_body(x_ref, cos_ref, sin_ref, g1_ref, win_ref, qg_ref, wq_ref, wqs_ref, kvg_ref,
                 wk_ref, ek_ref, eks_ref, wv_ref, gq_ref, gqs_ref, gk_ref, gks_ref,
                 xr_ref, gate_ref, q_ref, k_ref, v_ref):
    xn = _rms(x_ref[...], g1_ref[...])
    p = _dot(xn.astype(BF16), win_ref[...])
    xr_ref[...] = p[:, :RG_WIDTH]
    gate_ref[...] = p[:, RG_WIDTH:2 * RG_WIDTH]
    o2 = 2 * RG_WIDTH
    qcn = _rms(p[:, o2:o2 + Q_LORA], qg_ref[...]).astype(BF16)
    kvn = _rms(p[:, o2 + Q_LORA:o2 + Q_LORA + KV_LORA], kvg_ref[...]).astype(BF16)
    kpe = p[:, o2 + Q_LORA + KV_LORA:].astype(BF16)
    q = _dot(qcn, wq_ref[...])
    qs = _dot(qcn, wqs_ref[...])
    k = _dot(kvn, wk_ref[...]) + _dot(kpe, ek_ref[...])
    ks = _dot(kpe, eks_ref[...])
    v_ref[...] = _dot(kvn, wv_ref[...]).astype(BF16)
    c = cos_ref[...]
    s = sin_ref[...]
    qscale = QK_DIM ** -0.5 * LOG2E
    for h in range(N_HEADS):
        sl = slice(h * HEAD_PAD, (h + 1) * HEAD_PAD)
        qh = q[:, sl]
        rq = lax.rsqrt(jnp.sum(qh * qh, axis=-1, keepdims=True) * (1.0 / QK_DIM) + EPS)
        q_ref[:, sl] = ((qh * gq_ref[...] * c + qs[:, sl] * gqs_ref[...] * s) * (rq * qscale)).astype(BF16)
        kh = k[:, sl]
        rk = lax.rsqrt(jnp.sum(kh * kh, axis=-1, keepdims=True) * (1.0 / QK_DIM) + EPS)
        k_ref[:, sl] = ((kh * gk_ref[...] * c + ks[:, sl] * gks_ref[...] * s) * rk).astype(BF16)


def _inproj(h, cos_t, sin_t, w, *, rows, pos_tiles):
    tm = ROW_TILE
    row = lambda n: pl.BlockSpec((tm, n), lambda i: (i, 0))
    pos = pl.BlockSpec((tm, HEAD_PAD), lambda i: (i % pos_tiles, 0))
    wide = N_HEADS * HEAD_PAD
    return pl.pallas_call(
        _inproj_body,
        grid=(rows // tm,),
        in_specs=[row(D_MODEL), pos, pos, _full((1, D_MODEL)), _full((D_MODEL, P_IN_PAD)),
                  _full((1, Q_LORA)), _full((Q_LORA, wide)), _full((Q_LORA, wide)),
                  _full((1, KV_LORA)), _full((KV_LORA, wide)), _full((LANES, wide)), _full((LANES, wide)),
                  _full((KV_LORA, wide)),
                  _full((1, HEAD_PAD)), _full((1, HEAD_PAD)), _full((1, HEAD_PAD)), _full((1, HEAD_PAD))],
        out_specs=[row(RG_WIDTH), row(RG_WIDTH), row(wide), row(wide), row(wide)],
        out_shape=[jax.ShapeDtypeStruct((rows, RG_WIDTH), F32), jax.ShapeDtypeStruct((rows, RG_WIDTH), F32),
                   jax.ShapeDtypeStruct((rows, wide), BF16), jax.ShapeDtypeStruct((rows, wide), BF16),
                   jax.ShapeDtypeStruct((rows, wide), BF16)],
        compiler_params=_cparams("parallel"),
        name="inproj",
    )(h, cos_t, sin_t, w["g1"], w["win"], w["qg"], w["wq"], w["wqs"], w["kvg"], w["wk"], w["ek"], w["eks"],
      w["wv"], w["gq"], w["gqs"], w["gk"], w["gks"])


def _scan_body(xm_f, xp_f, xn_f, xm_b, xp_b, xn_b, cw_ref, cb_ref, wf_ref, wb_ref, bias_ref, lam_ref,
               hf_ref, hb_ref, ext_s, a_s, b_s, cf_s, cb_s, *, seq_len, chunk, n_chunks):
    i = pl.program_id(1)
    C = chunk
    groups = C // SUBLANES

    @pl.when(i == 0)
    def _():
        cf_s[...] = jnp.zeros_like(cf_s)
        cb_s[...] = jnp.zeros_like(cb_s)

    rows8 = lax.broadcasted_iota(jnp.int32, (SUBLANES, RG_WIDTH), 0)
    rows_c = lax.broadcasted_iota(jnp.int32, (C, RG_WIDTH), 0)
    rmod = rows_c & (SUBLANES - 1)

    def masked(x, pos):
        return jnp.where((pos >= 0) & (pos < seq_len), x, 0.0)

    def affine_terms(xm, xp, xn, j, w_ref, d):
        base = j * C
        ext_s[0:SUBLANES, :] = masked(xp[...], base - SUBLANES + rows8)
        ext_s[SUBLANES:SUBLANES + C, :] = masked(xm[...], base + rows_c)
        ext_s[SUBLANES + C:, :] = masked(xn[...], base + C + rows8)
        xc = cb_ref[...]
        for t in range(CONV_W):
            xc = xc + cw_ref[t:t + 1, :] * ext_s[SUBLANES - 2 + t:SUBLANES - 2 + t + C, :]
        gm = _dot(xc.astype(BF16), w_ref[...])
        r = _sigmoid(gm[:, :RG_WIDTH] + bias_ref[2 * d:2 * d + 1, :])
        ig = _sigmoid(gm[:, RG_WIDTH:] + bias_ref[2 * d + 1:2 * d + 2, :])
        z = -lam_ref[d:d + 1, :]
        softplus = jnp.maximum(z, 0.0) + jnp.log1p(jnp.exp(-jnp.abs(z)))
        log_a = (-RG_C) * r * softplus
        a = jnp.exp(log_a)
        b = jnp.sqrt(_neg_expm1(2.0 * log_a)) * (ig * xc)
        b = jnp.where(base + rows_c < seq_len, b, 0.0)
        return a, b

    a, b = affine_terms(xm_f, xp_f, xn_f, i, wf_ref, 0)
    for d in (1, 2, 4):
        m = rmod >= d
        a_sh = pltpu.roll(a, d, 0)
        b_sh = pltpu.roll(b, d, 0)
        b = jnp.where(m, a * b_sh + b, b)
        a = jnp.where(m, a * a_sh, a)
    a_s[...] = a
    b_s[...] = b

    def fwd_group(g, carry):
        r0 = pl.multiple_of(g * SUBLANES, SUBLANES)
        hrow = a_s[pl.ds(r0, SUBLANES), :] * carry + b_s[pl.ds(r0, SUBLANES), :]
        hf_ref[pl.ds(r0, SUBLANES), :] = hrow
        return jnp.broadcast_to(hrow[SUBLANES - 1:SUBLANES, :], (SUBLANES, RG_WIDTH))

    cf_s[...] = lax.fori_loop(0, groups, fwd_group, cf_s[...])

    a, b = affine_terms(xm_b, xp_b, xn_b, n_chunks - 1 - i, wb_ref, 1)
    for d in (1, 2, 4):
        m = rmod < SUBLANES - d
        a_sh = pltpu.roll(a, C - d, 0)
        b_sh = pltpu.roll(b, C - d, 0)
        b = jnp.where(m, a * b_sh + b, b)
        a = jnp.where(m, a * a_sh, a)
    a_s[...] = a
    b_s[...] = b

    def bwd_group(g, carry):
        r0 = pl.multiple_of((groups - 1 - g) * SUBLANES, SUBLANES)
        hrow = a_s[pl.ds(r0, SUBLANES), :] * carry + b_s[pl.ds(r0, SUBLANES), :]
        hb_ref[pl.ds(r0, SUBLANES), :] = hrow
        return jnp.broadcast_to(hrow[0:1, :], (SUBLANES, RG_WIDTH))

    cb_s[...] = lax.fori_loop(0, groups, bwd_group, cb_s[...])


def _scan(xr, w, *, batch, padded_len, seq_len):
    C = ROW_TILE
    n = padded_len // C
    per8 = C // SUBLANES
    last8 = batch * padded_len // SUBLANES - 1

    def chunk_f(b, i):
        return b * n + i

    def chunk_b(b, i):
        return b * n + (n - 1 - i)

    def specs(chunk_of):
        main = pl.BlockSpec((C, RG_WIDTH), lambda b, i: (chunk_of(b, i), 0))
        prev = pl.BlockSpec((SUBLANES, RG_WIDTH), lambda b, i: (jnp.maximum(chunk_of(b, i) * per8 - 1, 0), 0))
        nxt = pl.BlockSpec((SUBLANES, RG_WIDTH),
                           lambda b, i: (jnp.minimum((chunk_of(b, i) + 1) * per8, last8), 0))
        return [main, prev, nxt]

    rows = batch * padded_len
    return pl.pallas_call(
        functools.partial(_scan_body, seq_len=seq_len, chunk=C, n_chunks=n),
        grid=(batch, n),
        in_specs=specs(chunk_f) + specs(chunk_b) + [
            _full((CONV_W, RG_WIDTH)), _full((1, RG_WIDTH)),
            _full((RG_WIDTH, 2 * RG_WIDTH)), _full((RG_WIDTH, 2 * RG_WIDTH)),
            _full((4, RG_WIDTH)), _full((2, RG_WIDTH))],
        out_specs=[pl.BlockSpec((C, RG_WIDTH), lambda b, i: (chunk_f(b, i), 0)),
                   pl.BlockSpec((C, RG_WIDTH), lambda b, i: (chunk_b(b, i), 0))],
        out_shape=[jax.ShapeDtypeStruct((rows, RG_WIDTH), F32), jax.ShapeDtypeStruct((rows, RG_WIDTH), F32)],
        scratch_shapes=[pltpu.VMEM((C + 2 * SUBLANES, RG_WIDTH), F32), pltpu.VMEM((C, RG_WIDTH), F32),
                        pltpu.VMEM((C, RG_WIDTH), F32), pltpu.VMEM((SUBLANES, RG_WIDTH), F32),
                        pltpu.VMEM((SUBLANES, RG_WIDTH), F32)],
        compiler_params=_cparams("arbitrary", "arbitrary"),
        name="rglru_scan",
    )(xr, xr, xr, xr, xr, xr, w["conv_w"], w["conv_b"], w["wgate_f"], w["wgate_b"], w["gate_bias"], w["lam"])


def _attn_body(q_ref, kt_ref, v_ref, o_ref, s_a, s_b, m_s, l_s, acc_s, *, seq_len, n_full, rem):
    tq = q_ref.shape[1]
    m_s[...] = jnp.full_like(m_s, NEG_INF)
    l_s[...] = jnp.zeros_like(l_s)
    acc_s[...] = jnp.zeros_like(acc_s)

    def scores(dst, start, width):
        dst[:, :width] = _dot(q_ref[0], kt_ref[0, 0, :, pl.ds(start, width)])

    def update(src, start, width, mask):
        s = src[:, :width]
        if mask:
            kpos = start + lax.broadcasted_iota(jnp.int32, (tq, width), 1)
            s = jnp.where(kpos < seq_len, s, MASK_VALUE)
        m = m_s[...]
        m_new = jnp.maximum(m, jnp.max(s, axis=-1, keepdims=True))
        alpha = jnp.exp2(m - m_new)
        p = jnp.exp2(s - m_new)
        m_s[...] = m_new
        l_s[...] = alpha * l_s[...] + jnp.sum(p, axis=-1, keepdims=True)
        acc_s[...] = alpha * acc_s[...] + _dot(p.astype(BF16), v_ref[0, pl.ds(start, width), :])

    if rem:
        scores(s_b, n_full * KV_CHUNK, rem)
    if n_full:
        scores(s_a, 0, KV_CHUNK)
    if rem:
        update(s_b, n_full * KV_CHUNK, rem, True)
    pairs = max((n_full - 1) // 2, 0)
    tail = [(c * KV_CHUNK, KV_CHUNK, False) for c in range(2 * pairs, n_full)]

    def pair(i, carry):
        c0 = pl.multiple_of(2 * i * KV_CHUNK, KV_CHUNK)
        scores(s_b, c0 + KV_CHUNK, KV_CHUNK)
        update(s_a, c0, KV_CHUNK, False)
        scores(s_a, c0 + 2 * KV_CHUNK, KV_CHUNK)
        update(s_b, c0 + KV_CHUNK, KV_CHUNK, False)
        return carry

    lax.fori_loop(0, pairs, pair, 0)
    bufs = (s_a, s_b)
    for n, (start, width, mask) in enumerate(tail):
        if n + 1 < len(tail):
            scores(bufs[(n + 1) % 2], tail[n + 1][0], tail[n + 1][1])
        update(bufs[n % 2], start, width, mask)
    o_ref[0] = acc_s[...] / l_s[...]


def _attention(q, kt, v, *, batch, padded_len, seq_len):
    tq = ATTN_Q_TILE
    n_full = seq_len // KV_CHUNK
    rem = -(-(seq_len - n_full * KV_CHUNK) // LANES) * LANES
    wide = N_HEADS * HEAD_PAD
    return pl.pallas_call(
        functools.partial(_attn_body, seq_len=seq_len, n_full=n_full, rem=rem),
        grid=(batch, N_HEADS, padded_len // tq),
        in_specs=[pl.BlockSpec((1, tq, HEAD_PAD), lambda b, h, i: (b, i, h)),
                  pl.BlockSpec((1, 1, HEAD_PAD, padded_len), lambda b, h, i: (b, h, 0, 0)),
                  pl.BlockSpec((1, padded_len, HEAD_PAD), lambda b, h, i: (b, 0, h))],
        out_specs=pl.BlockSpec((1, tq, HEAD_PAD), lambda b, h, i: (b, i, h)),
        out_shape=jax.ShapeDtypeStruct((batch, padded_len, wide), F32),
        scratch_shapes=[pltpu.VMEM((tq, KV_CHUNK), F32), pltpu.VMEM((tq, KV_CHUNK), F32),
                        pltpu.VMEM((tq, 1), F32), pltpu.VMEM((tq, 1), F32), pltpu.VMEM((tq, HEAD_PAD), F32)],
        compiler_params=_cparams("parallel", "parallel", "arbitrary"),
        name="attention",
    )(q, kt, v)


def _outproj_body(hf_ref, hb_ref, gate_ref, o_ref, h_ref, grg_ref, gat_ref, wrg_ref, wat_ref, g2_ref,
                  wqt_ref, sk_ref, hn_ref, xn2_ref, sc_ref):
    rg = (hf_ref[...] + hb_ref[...]) * _gelu_tanh(gate_ref[...])
    rgn = _rms(rg, grg_ref[...])
    o = o_ref[...]
    ms = jnp.sum(o * o, axis=-1, keepdims=True) * (1.0 / (N_HEADS * V_DIM))
    attn = o * lax.rsqrt(ms + EPS) * gat_ref[...]
    hn = h_ref[...] + _dot(rgn.astype(BF16), wrg_ref[...]) + _dot(attn.astype(BF16), wat_ref[...])
    hn_ref[...] = hn
    xn2 = _rms(hn, g2_ref[...]).astype(BF16)
    xn2_ref[...] = xn2
    qpt = _dot_nt(wqt_ref[...], xn2)
    for h in range(PEER_HEADS):
        for s in range(2):
            r0 = (h * 2 + s) * PEER_HALF
            blk = qpt[r0:r0 + PEER_HALF, :].astype(BF16)
            o0 = s * PEER_HEADS * N_KEYS + h * N_KEYS
            sc_ref[o0:o0 + N_KEYS, :] = _dot(sk_ref[h * 2 + s], blk)


def _outproj(hf, hb, gate, o, h, w, *, rows):
    tm = ROW_TILE
    row = lambda n: pl.BlockSpec((tm, n), lambda i: (i, 0))
    wide = N_HEADS * HEAD_PAD
    nsc = 2 * PEER_HEADS * N_KEYS
    return pl.pallas_call(
        _outproj_body,
        grid=(rows // tm,),
        in_specs=[row(RG_WIDTH), row(RG_WIDTH), row(RG_WIDTH), row(wide), row(D_MODEL),
                  _full((1, RG_WIDTH)), _full((1, wide)), _full((RG_WIDTH, D_MODEL)), _full((wide, D_MODEL)),
                  _full((1, D_MODEL)), _full((nsc, D_MODEL)), _full((2 * PEER_HEADS, N_KEYS, PEER_HALF))],
        out_specs=[row(D_MODEL), row(D_MODEL), pl.BlockSpec((nsc, tm), lambda i: (0, i))],
        out_shape=[jax.ShapeDtypeStruct((rows, D_MODEL), F32), jax.ShapeDtypeStruct((rows, D_MODEL), BF16),
                   jax.ShapeDtypeStruct((nsc, rows), F32)],
        compiler_params=_cparams("parallel"),
        name="outproj",
    )(hf, hb, gate, o, h, w["grg"], w["gat"], w["wout_rg"], w["wout_at"], w["g2"], w["wqt"], w["subk"])


def _sort_network(n):
    pairs, p = [], 1
    while p < n:
        k = p
        while k >= 1:
            for j in range(k % p, n - k, 2 * k):
                for i in range(min(k, n - j - k)):
                    if (i + j) // (2 * p) == (i + j + k) // (2 * p):
                        pairs.append((i + j, i + j + k))
            k //= 2
        p *= 2
    return pairs


def _top_rows(x, count):
    n = x.shape[0] // SUBLANES
    cols = [x[k * SUBLANES:(k + 1) * SUBLANES, :] for k in range(n)]
    for a, b in _sort_network(n):
        cols[a], cols[b] = jnp.maximum(cols[a], cols[b]), jnp.minimum(cols[a], cols[b])
    cols.append(jnp.full_like(cols[0], NEG_INF))
    rows = []
    for r in range(count):
        m = jnp.max(cols[0], axis=0, keepdims=True)
        rows.append(m)
        depth = min(n, count - 1 - r)
        hit = cols[0] == m
        for k in range(depth):
            cols[k] = jnp.where(hit, cols[k + 1], cols[k])
    return rows


def _route_body(sc_ref, th_ref, e1_ref, e2_ref):
    t = sc_ref.shape[1]
    half = PEER_HEADS * N_KEYS
    jrow = lax.broadcasted_iota(jnp.int32, (SUBLANES, t), 0)

    def bc(row):
        return jnp.broadcast_to(row, (SUBLANES, t))

    def head(h, carry):
        r0 = pl.multiple_of(h * N_KEYS, N_KEYS)
        s1 = sc_ref[pl.ds(r0, N_KEYS), :]
        s2 = sc_ref[pl.ds(half + r0, N_KEYS), :]
        a = _top_rows(s1, PEER_TOPK + 1)
        b = _top_rows(s2, PEER_TOPK + 1)
        b_lo = jnp.concatenate(b[:SUBLANES], axis=0)
        b_hi = jnp.concatenate(b[SUBLANES:2 * SUBLANES], axis=0)
        a_hi = jnp.concatenate(a[SUBLANES:2 * SUBLANES], axis=0)
        pieces = [bc(a[0]) + b_lo, bc(a[0]) + b_hi, bc(a[1]) + b_lo]
        for i in range(2, SUBLANES):
            pieces.append(jnp.where(jrow < (PEER_TOPK + 1) // (i + 1), bc(a[i]) + b_lo, NEG_INF))
        pieces.append(a_hi + bc(b[0]))
        ends = jnp.concatenate([a[0] + b[PEER_TOPK], a[PEER_TOPK] + b[0]]
                               + [jnp.full((1, t), NEG_INF, F32)] * (SUBLANES - 2), axis=0)
        pieces.append(ends)
        cand = jnp.concatenate(pieces, axis=0)
        x = cand
        for _ in range(PEER_TOPK - 1):
            x = jnp.where(x == jnp.max(x, axis=0, keepdims=True), NEG_INF, x)
        c16 = jnp.max(x, axis=0, keepdims=True)
        c17 = jnp.max(jnp.where(x == c16, NEG_INF, x), axis=0, keepdims=True)
        thr = 0.5 * (c16 + c17)
        top = a[0] + b[0]
        z = jnp.sum(jnp.where(cand >= thr, jnp.exp(cand - top), 0.0), axis=0, keepdims=True)
        th_ref[pl.ds(r0, N_KEYS), :] = jnp.exp((thr - s1) - b[0])
        e1_ref[pl.ds(r0, N_KEYS), :] = jnp.exp(s1 - a[0]) / z
        e2_ref[pl.ds(r0, N_KEYS), :] = jnp.exp(s2 - b[0])
        return carry

    lax.fori_loop(0, PEER_HEADS, head, 0)


def _route(sc, *, rows):
    t = ROUTE_TOKENS
    half = PEER_HEADS * N_KEYS
    out = pl.BlockSpec((half, t), lambda i: (0, i))
    return pl.pallas_call(
        _route_body,
        grid=(rows // t,),
        in_specs=[pl.BlockSpec((2 * half, t), lambda i: (0, i))],
        out_specs=[out, out, out],
        out_shape=[jax.ShapeDtypeStruct((half, rows), F32)] * 3,
        compiler_params=_cparams("parallel"),
        name="peer_route",
    )(sc)


def _peer_body(xn_ref, h_ref, th_ref, e1_ref, e2_ref, u_ref, vt_ref, out_ref, acc_s, hid_s, w_s):
    j = pl.program_id(1)
    tt = xn_ref.shape[0]

    @pl.when(j == 0)
    def _():
        acc_s[...] = jnp.zeros_like(acc_s)

    hid_s[...] = _dot_nt(u_ref[...], xn_ref[...])

    i1_base = pl.multiple_of(j * PEER_I1_BLOCK, PEER_I1_BLOCK)
    for il in range(PEER_I1_BLOCK):
        es = slice(il * N_KEYS, (il + 1) * N_KEYS)
        for lt in range(tt // LANES):
            ln = slice(lt * LANES, (lt + 1) * LANES)
            g = jnp.zeros((N_KEYS, LANES), F32)
            for h in range(PEER_HEADS):
                hs = slice(h * N_KEYS, (h + 1) * N_KEYS)
                th = th_ref[pl.ds(h * N_KEYS + i1_base, PEER_I1_BLOCK), ln][il:il + 1, :]
                e1 = e1_ref[pl.ds(h * N_KEYS + i1_base, PEER_I1_BLOCK), ln][il:il + 1, :]
                e2 = e2_ref[hs, ln]
                g = g + jnp.where(e2 >= th, e2, 0.0) * e1
            w_s[es, ln] = (g * _gelu_tanh(hid_s[es, ln])).astype(BF16)
    acc_s[...] += _dot(vt_ref[...], w_s[...])

    @pl.when(j == pl.num_programs(1) - 1)
    def _():
        out_ref[...] = h_ref[...] + acc_s[...].T


def _peer(xn2, hn, th, e1, e2, u, vt, *, rows):
    tt = PEER_TOKENS
    eb = PEER_I1_BLOCK * N_KEYS
    half = PEER_HEADS * N_KEYS
    tok = pl.BlockSpec((tt, D_MODEL), lambda i, j: (i, 0))
    tab = pl.BlockSpec((half, tt), lambda i, j: (0, i))
    return pl.pallas_call(
        _peer_body,
        grid=(rows // tt, N_EXPERTS // eb),
        in_specs=[tok, tok, tab, tab, tab,
                  pl.BlockSpec((eb, D_MODEL), lambda i, j: (j, 0)),
                  pl.BlockSpec((D_MODEL, eb), lambda i, j: (0, j))],
        out_specs=pl.BlockSpec((tt, D_MODEL), lambda i, j: (i, 0)),
        out_shape=jax.ShapeDtypeStruct((rows, D_MODEL), F32),
        scratch_shapes=[pltpu.VMEM((D_MODEL, tt), F32), pltpu.VMEM((eb, tt), F32), pltpu.VMEM((eb, tt), BF16)],
        compiler_params=_cparams("parallel", "arbitrary"),
        name="peer_dense",
    )(xn2, hn, th, e1, e2, u, vt)


def _block_diag(w):
    eye = jnp.eye(RG_HEADS, dtype=w.dtype)
    return jnp.einsum("hij,hg->higj", w, eye).reshape(RG_WIDTH, RG_WIDTH)


def _head_pad(w, lo, hi):
    k = w.shape[0]
    part = w[:, :, lo:hi]
    return jnp.pad(part, ((0, 0), (0, 0), (0, HEAD_PAD - (hi - lo)))).reshape(k, N_HEADS * HEAD_PAD)


def _rope_partner(x1, x2):
    z64 = jnp.zeros(x1.shape[:-1] + (QK_NOPE,), x1.dtype)
    z32 = jnp.zeros(x1.shape[:-1] + (HEAD_PAD - QK_DIM,), x1.dtype)
    return jnp.concatenate([z64, x2, x1, z32], axis=-1)


def _layer_weights(i, norm1_g, w_in, conv_w, conv_b, rg_wa, rg_ba, rg_wi, rg_bi, rg_lambda, q_norm_g, w_uq,
                   kv_norm_g, w_ukv, q_head_g, k_head_g, out_g_rg, out_g_attn, w_out, norm2_g, peer_wq,
                   peer_subkeys, peer_u, peer_v):
    half = QK_ROPE // 2
    wq3 = w_uq[i].reshape(Q_LORA, N_HEADS, QK_DIM)
    wkv3 = w_ukv[i].reshape(KV_LORA, N_HEADS, QK_NOPE + V_DIM)
    eye = jnp.eye(QK_ROPE, dtype=F32)
    ek = jnp.concatenate([jnp.zeros((QK_ROPE, QK_NOPE), F32), eye,
                          jnp.zeros((QK_ROPE, HEAD_PAD - QK_DIM), F32)], axis=-1)
    eks = _rope_partner(eye[:, :half], eye[:, half:])
    place = lambda e: jnp.pad(jnp.tile(e, (1, N_HEADS)), ((0, LANES - QK_ROPE), (0, 0))).astype(BF16)

    def head_gain(g):
        main = jnp.pad(g, (0, HEAD_PAD - QK_DIM)).reshape(1, HEAD_PAD)
        swap = _rope_partner(g[QK_NOPE:QK_NOPE + half], g[QK_NOPE + half:]).reshape(1, HEAD_PAD)
        return main, swap

    gq, gqs = head_gain(q_head_g[i])
    gk, gks = head_gain(k_head_g[i])
    wout = w_out[i]
    wout_at = jnp.pad(wout[RG_WIDTH:].reshape(N_HEADS, V_DIM, D_MODEL),
                      ((0, 0), (0, HEAD_PAD - V_DIM), (0, 0))).reshape(N_HEADS * HEAD_PAD, D_MODEL)
    gat = jnp.pad(out_g_attn[i].reshape(N_HEADS, V_DIM), ((0, 0), (0, HEAD_PAD - V_DIM))).reshape(1, -1)
    return dict(
        g1=norm1_g[i].reshape(1, -1),
        win=jnp.pad(w_in[i], ((0, 0), (0, P_IN_PAD - P_IN))).astype(BF16),
        qg=q_norm_g[i].reshape(1, -1),
        wq=_head_pad(wq3, 0, QK_DIM).astype(BF16),
        wqs=_rope_partner(wq3[:, :, QK_NOPE:QK_NOPE + half], wq3[:, :, QK_NOPE + half:])
        .reshape(Q_LORA, -1).astype(BF16),
        kvg=kv_norm_g[i].reshape(1, -1),
        wk=_head_pad(wkv3, 0, QK_NOPE).astype(BF16),
        wv=_head_pad(wkv3, QK_NOPE, QK_NOPE + V_DIM).astype(BF16),
        ek=place(ek), eks=place(eks), gq=gq, gqs=gqs, gk=gk, gks=gks,
        conv_w=conv_w[i], conv_b=conv_b[i].reshape(1, -1),
        wgate_f=jnp.concatenate([_block_diag(rg_wa[i, 0]), _block_diag(rg_wi[i, 0])], axis=1).astype(BF16),
        wgate_b=jnp.concatenate([_block_diag(rg_wa[i, 1]), _block_diag(rg_wi[i, 1])], axis=1).astype(BF16),
        gate_bias=jnp.stack([rg_ba[i, 0].reshape(-1), rg_bi[i, 0].reshape(-1),
                             rg_ba[i, 1].reshape(-1), rg_bi[i, 1].reshape(-1)]),
        lam=rg_lambda[i],
        grg=out_g_rg[i].reshape(1, -1), gat=gat,
        wout_rg=wout[:RG_WIDTH].astype(BF16), wout_at=wout_at.astype(BF16),
        g2=norm2_g[i].reshape(1, -1),
        wqt=peer_wq[i].T.astype(BF16),
        subk=peer_subkeys[i].reshape(2 * PEER_HEADS, N_KEYS, PEER_HALF).astype(BF16),
        u=peer_u[i].astype(BF16),
        vt=peer_v[i].T.astype(BF16),
    )


def _rope_tables(length):
    pos = jnp.arange(length, dtype=F32)
    inv = ROPE_THETA ** (-jnp.arange(0, QK_ROPE, 2, dtype=F32) / QK_ROPE)
    ang = pos[:, None] * inv[None, :]
    cos, sin = jnp.cos(ang), jnp.sin(ang)
    ones = jnp.ones((length, QK_NOPE), F32)
    zpad = jnp.zeros((length, HEAD_PAD - QK_DIM), F32)
    cos_t = jnp.concatenate([ones, cos, cos, zpad], axis=-1)
    sin_t = jnp.concatenate([jnp.zeros((length, QK_NOPE), F32), -sin, sin, zpad], axis=-1)
    return cos_t, sin_t


def _padded_len(length):
    tile = math.lcm(ROW_TILE, ATTN_Q_TILE, PEER_TOKENS)
    return -(-length // tile) * tile


def _encode(x, meta_tokens, layers):
    batch, n_tok, _ = x.shape
    seq_len = n_tok + N_META
    lp = _padded_len(seq_len)
    rows = batch * lp
    meta = jnp.broadcast_to(meta_tokens[None].astype(x.dtype), (batch, N_META, D_MODEL))
    h = jnp.concatenate([meta, x, jnp.zeros((batch, lp - seq_len, D_MODEL), x.dtype)], axis=1)
    h = h.reshape(rows, D_MODEL)
    cos_t, sin_t = _rope_tables(lp)
    wide = N_HEADS * HEAD_PAD
    for w in layers:
        xr, gate, q, k, v = _inproj(h, cos_t, sin_t, w, rows=rows, pos_tiles=lp // ROW_TILE)
        hf, hb = _scan(xr, w, batch=batch, padded_len=lp, seq_len=seq_len)
        kt = k.reshape(batch, lp, N_HEADS, HEAD_PAD).transpose(0, 2, 3, 1)
        o = _attention(q.reshape(batch, lp, wide), kt, v.reshape(batch, lp, wide),
                       batch=batch, padded_len=lp, seq_len=seq_len)
        hn, xn2, sc = _outproj(hf, hb, gate, o.reshape(rows, wide), h, w, rows=rows)
        th, e1, e2 = _route(sc, rows=rows)
        h = _peer(xn2, hn, th, e1, e2, w["u"], w["vt"], rows=rows)
    return h.reshape(batch, lp, D_MODEL)[:, N_META:seq_len]


def kernel(x_prompt, x_sample, meta_tokens, norm1_g, w_in, conv_w, conv_b, rg_wa, rg_ba, rg_wi, rg_bi, rg_lambda, q_norm_g, w_uq, kv_norm_g, w_ukv, q_head_g, k_head_g, out_g_rg, out_g_attn, w_out, norm2_g, peer_wq, peer_subkeys, peer_u, peer_v):
    params = (norm1_g, w_in, conv_w, conv_b, rg_wa, rg_ba, rg_wi, rg_bi, rg_lambda, q_norm_g, w_uq, kv_norm_g,
              w_ukv, q_head_g, k_head_g, out_g_rg, out_g_attn, w_out, norm2_g, peer_wq, peer_subkeys, peer_u,
              peer_v)
    layers = [_layer_weights(i, *params) for i in range(norm1_g.shape[0])]
    y_prompt = _encode(x_prompt, meta_tokens, layers)
    y_sample = _encode(x_sample, meta_tokens, layers)
    return (y_prompt, y_sample)
```

```python
import functools
import math

import jax
import jax.numpy as jnp
from jax import lax
from jax.experimental import pallas as pl
from jax.experimental.pallas import tpu as pltpu

F32 = jnp.float32
BF16 = jnp.bfloat16

D_MODEL = 1024
N_META = 16
EPS = 1e-6
RG_WIDTH = 512
RG_HEADS = 8
RG_BLOCK = 64
CONV_W = 4
RG_C = 8.0
N_HEADS = 8
QK_NOPE = 64
QK_ROPE = 32
QK_DIM = 96
V_DIM = 64
Q_LORA = 256
KV_LORA = 128
ROPE_THETA = 10000.0
P_IN = 2 * RG_WIDTH + Q_LORA + KV_LORA + QK_ROPE
PEER_HEADS = 8
N_KEYS = 128
N_EXPERTS = N_KEYS * N_KEYS
PEER_TOPK = 16
PEER_HALF = 128

LANES = 128
SUBLANES = 8
HEAD_PAD = LANES
P_IN_PAD = 12 * LANES
ROW_TILE = 512
SCAN_CHUNK = 256
ATTN_Q_TILE = 512
KV_CHUNK = 1024
PEER_TOKENS = 512
ROUTE_TOKENS = 256
PEER_I1_BLOCK = 8
VMEM_LIMIT = 52 * 1024 * 1024
NEG_INF = float("-inf")
MASK_VALUE = -1e30
LOG2E = 1.4426950408889634


def _cparams(*sem):
    return pltpu.CompilerParams(dimension_semantics=sem, vmem_limit_bytes=VMEM_LIMIT)


def _rms(x, g):
    return x * lax.rsqrt(jnp.mean(x * x, axis=-1, keepdims=True) + EPS) * g


def _gelu_tanh(x):
    return x * (0.5 * (1.0 + jnp.tanh(0.7978845608028654 * (x + 0.044715 * (x * x * x)))))


def _sigmoid(x):
    return 1.0 / (1.0 + jnp.exp(-x))


def _neg_expm1(y):
    u = jnp.exp(y)
    um1 = u - 1.0
    tiny = um1 == 0.0
    r = jnp.where(tiny, y, um1 * y / jnp.where(tiny, 1.0, jnp.log(u)))
    return -jnp.where(um1 == -1.0, -1.0, r)


def _dot(a, b):
    return jnp.dot(a, b, preferred_element_type=F32)


def _dot_nt(a, b):
    return lax.dot_general(a, b, (((1,), (1,)), ((), ())), preferred_element_type=F32)


def _full(shape):
    n = len(shape)
    return pl.BlockSpec(shape, lambda *_: (0,) * n)


def _inproj_body(x_ref, cos_ref, sin_ref, g1_ref, win_ref, qg_ref, wq_ref, wqs_ref, kvg_ref,
                 wk_ref, ek_ref, eks_ref, wv_ref, gq_ref, gqs_ref, gk_ref, gks_ref,
                 xr_ref, gate_ref, q_ref, k_ref, v_ref):
    xn = _rms(x_ref[...], g1_ref[...])
    p = _dot(xn.astype(BF16), win_ref[...])
    xr_ref[...] = p[:, :RG_WIDTH]
    gate_ref[...] = p[:, RG_WIDTH:2 * RG_WIDTH]
    o2 = 2 * RG_WIDTH
    qcn = _rms(p[:, o2:o2 + Q_LORA], qg_ref[...]).astype(BF16)
    kvn = _rms(p[:, o2 + Q_LORA:o2 + Q_LORA + KV_LORA], kvg_ref[...]).astype(BF16)
    kpe = p[:, o2 + Q_LORA + KV_LORA:].astype(BF16)
    q = _dot(qcn, wq_ref[...])
    qs = _dot(qcn, wqs_ref[...])
    k = _dot(kvn, wk_ref[...]) + _dot(kpe, ek_ref[...])
    ks = _dot(kpe, eks_ref[...])
    v_ref[...] = _dot(kvn, wv_ref[...]).astype(BF16)
    c = cos_ref[...]
    s = sin_ref[...]
    qscale = QK_DIM ** -0.5 * LOG2E
    for h in range(N_HEADS):
        sl = slice(h * HEAD_PAD, (h + 1) * HEAD_PAD)
        qh = q[:, sl]
        rq = lax.rsqrt(jnp.sum(qh * qh, axis=-1, keepdims=True) * (1.0 / QK_DIM) + EPS)
        q_ref[:, sl] = ((qh * gq_ref[...] * c + qs[:, sl] * gqs_ref[...] * s) * (rq * qscale)).astype(BF16)
        kh = k[:, sl]
        rk = lax.rsqrt(jnp.sum(kh * kh, axis=-1, keepdims=True) * (1.0 / QK_DIM) + EPS)
        k_ref[:, sl] = ((kh * gk_ref[...] * c + ks[:, sl] * gks_ref[...] * s) * rk).astype(BF16)


def _inproj(h, cos_t, sin_t, w, *, rows, pos_tiles):
    tm = ROW_TILE
    row = lambda n: pl.BlockSpec((tm, n), lambda i: (i, 0))
    pos = pl.BlockSpec((tm, HEAD_PAD), lambda i: (i % pos_tiles, 0))
    wide = N_HEADS * HEAD_PAD
    return pl.pallas_call(
        _inproj_body,
        grid=(rows // tm,),
        in_specs=[row(D_MODEL), pos, pos, _full((1, D_MODEL)), _full((D_MODEL, P_IN_PAD)),
                  _full((1, Q_LORA)), _full((Q_LORA, wide)), _full((Q_LORA, wide)),
                  _full((1, KV_LORA)), _full((KV_LORA, wide)), _full((LANES, wide)), _full((LANES, wide)),
                  _full((KV_LORA, wide)),
                  _full((1, HEAD_PAD)), _full((1, HEAD_PAD)), _full((1, HEAD_PAD)), _full((1, HEAD_PAD))],
        out_specs=[row(RG_WIDTH), row(RG_WIDTH), row(wide), row(wide), row(wide)],
        out_shape=[jax.ShapeDtypeStruct((rows, RG_WIDTH), F32), jax.ShapeDtypeStruct((rows, RG_WIDTH), F32),
                   jax.ShapeDtypeStruct((rows, wide), BF16), jax.ShapeDtypeStruct((rows, wide), BF16),
                   jax.ShapeDtypeStruct((rows, wide), BF16)],
        compiler_params=_cparams("parallel"),
        name="inproj",
    )(h, cos_t, sin_t, w["g1"], w["win"], w["qg"], w["wq"], w["wqs"], w["kvg"], w["wk"], w["ek"], w["eks"],
      w["wv"], w["gq"], w["gqs"], w["gk"], w["gks"])


def _scan_body(xm_f, xp_f, xn_f, xm_b, xp_b, xn_b, cw_ref, cb_ref, wf_ref, wb_ref, bias_ref, lam_ref,
               hf_ref, hb_ref, ext_s, a_s, b_s, cf_s, cb_s, *, seq_len, chunk, n_chunks):
    i = pl.program_id(1)
    C = chunk
    groups = C // SUBLANES

    @pl.when(i == 0)
    def _():
        cf_s[...] = jnp.zeros_like(cf_s)
        cb_s[...] = jnp.zeros_like(cb_s)

    rows8 = lax.broadcasted_iota(jnp.int32, (SUBLANES, RG_WIDTH), 0)
    rows_c = lax.broadcasted_iota(jnp.int32, (C, RG_WIDTH), 0)
    rmod = rows_c & (SUBLANES - 1)

    def masked(x, pos):
        return jnp.where((pos >= 0) & (pos < seq_len), x, 0.0)

    def affine_terms(xm, xp, xn, j, w_ref, d):
        base = j * C
        ext_s[0:SUBLANES, :] = masked(xp[...], base - SUBLANES + rows8)
        ext_s[SUBLANES:SUBLANES + C, :] = masked(xm[...], base + rows_c)
        ext_s[SUBLANES + C:, :] = masked(xn[...], base + C + rows8)
        xc = cb_ref[...]
        for t in range(CONV_W):
            xc = xc + cw_ref[t:t + 1, :] * ext_s[SUBLANES - 2 + t:SUBLANES - 2 + t + C, :]
        gm = _dot(xc.astype(BF16), w_ref[...])
        r = _sigmoid(gm[:, :RG_WIDTH] + bias_ref[2 * d:2 * d + 1, :])
        ig = _sigmoid(gm[:, RG_WIDTH:] + bias_ref[2 * d + 1:2 * d + 2, :])
        z = -lam_ref[d:d + 1, :]
        softplus = jnp.maximum(z, 0.0) + jnp.log1p(jnp.exp(-jnp.abs(z)))
        log_a = (-RG_C) * r * softplus
        a = jnp.exp(log_a)
        b = jnp.sqrt(_neg_expm1(2.0 * log_a)) * (ig * xc)
        b = jnp.where(base + rows_c < seq_len, b, 0.0)
        return a, b

    a, b = affine_terms(xm_f, xp_f, xn_f, i, wf_ref, 0)
    for d in (1, 2, 4):
        m = rmod >= d
        a_sh = pltpu.roll(a, d, 0)
        b_sh = pltpu.roll(b, d, 0)
        b = jnp.where(m, a * b_sh + b, b)
        a = jnp.where(m, a * a_sh, a)
    a_s[...] = a
    b_s[...] = b

    def fwd_group(g, carry):
        r0 = pl.multiple_of(g * SUBLANES, SUBLANES)
        hrow = a_s[pl.ds(r0, SUBLANES), :] * carry + b_s[pl.ds(r0, SUBLANES), :]
        hf_ref[pl.ds(r0, SUBLANES), :] = hrow
        return jnp.broadcast_to(hrow[SUBLANES - 1:SUBLANES, :], (SUBLANES, RG_WIDTH))

    cf_s[...] = lax.fori_loop(0, groups, fwd_group, cf_s[...])

    a, b = affine_terms(xm_b, xp_b, xn_b, n_chunks - 1 - i, wb_ref, 1)
    for d in (1, 2, 4):
        m = rmod < SUBLANES - d
        a_sh = pltpu.roll(a, C - d, 0)
        b_sh = pltpu.roll(b, C - d, 0)
        b = jnp.where(m, a * b_sh + b, b)
        a = jnp.where(m, a * a_sh, a)
    a_s[...] = a
    b_s[...] = b

    def bwd_group(g, carry):
        r0 = pl.multiple_of((groups - 1 - g) * SUBLANES, SUBLANES)
        hrow = a_s[pl.ds(r0, SUBLANES), :] * carry + b_s[pl.ds(r0, SUBLANES), :]
        hb_ref[pl.ds(r0, SUBLANES), :] = hrow
        return jnp.broadcast_to(hrow[0:1, :], (SUBLANES, RG_WIDTH))

    cb_s[...] = lax.fori_loop(0, groups, bwd_group, cb_s[...])


def _scan(xr, w, *, batch, padded_len, seq_len):
    C = SCAN_CHUNK
    n = padded_len // C
    per8 = C // SUBLANES
    last8 = batch * padded_len // SUBLANES - 1

    def chunk_f(b, i):
        return b * n + i

    def chunk_b(b, i):
        return b * n + (n - 1 - i)

    def specs(chunk_of):
        main = pl.BlockSpec((C, RG_WIDTH), lambda b, i: (chunk_of(b, i), 0))
        prev = pl.BlockSpec((SUBLANES, RG_WIDTH), lambda b, i: (jnp.maximum(chunk_of(b, i) * per8 - 1, 0), 0))
        nxt = pl.BlockSpec((SUBLANES, RG_WIDTH),
                           lambda b, i: (jnp.minimum((chunk_of(b, i) + 1) * per8, last8), 0))
        return [main, prev, nxt]

    rows = batch * padded_len
    return pl.pallas_call(
        functools.partial(_scan_body, seq_len=seq_len, chunk=C, n_chunks=n),
        grid=(batch, n),
        in_specs=specs(chunk_f) + specs(chunk_b) + [
            _full((CONV_W, RG_WIDTH)), _full((1, RG_WIDTH)),
            _full((RG_WIDTH, 2 * RG_WIDTH)), _full((RG_WIDTH, 2 * RG_WIDTH)),
            _full((4, RG_WIDTH)), _full((2, RG_WIDTH))],
        out_specs=[pl.BlockSpec((C, RG_WIDTH), lambda b, i: (chunk_f(b, i), 0)),
                   pl.BlockSpec((C, RG_WIDTH), lambda b, i: (chunk_b(b, i), 0))],
        out_shape=[jax.ShapeDtypeStruct((rows, RG_WIDTH), F32), jax.ShapeDtypeStruct((rows, RG_WIDTH), F32)],
        scratch_shapes=[pltpu.VMEM((C + 2 * SUBLANES, RG_WIDTH), F32), pltpu.VMEM((C, RG_WIDTH), F32),
                        pltpu.VMEM((C, RG_WIDTH), F32), pltpu.VMEM((SUBLANES, RG_WIDTH), F32),
                        pltpu.VMEM((SUBLANES, RG_WIDTH), F32)],
        compiler_params=_cparams("arbitrary", "arbitrary"),
        name="rglru_scan",
    )(xr, xr, xr, xr, xr, xr, w["conv_w"], w["conv_b"], w["wgate_f"], w["wgate_b"], w["gate_bias"], w["lam"])


def _attn_body(q_ref, kt_ref, v_ref, o_ref, s_a, s_b, m_s, l_s, acc_s, *, seq_len, n_full, rem):
    tq = q_ref.shape[1]
    m_s[...] = jnp.full_like(m_s, NEG_INF)
    l_s[...] = jnp.zeros_like(l_s)
    acc_s[...] = jnp.zeros_like(acc_s)

    def scores(dst, start, width):
        dst[:, :width] = _dot(q_ref[0], kt_ref[0, 0, :, pl.ds(start, width)])

    def update(src, start, width, mask):
        s = src[:, :width]
        if mask:
            kpos = start + lax.broadcasted_iota(jnp.int32, (tq, width), 1)
            s = jnp.where(kpos < seq_len, s, MASK_VALUE)
        m = m_s[...]
        m_new = jnp.maximum(m, jnp.max(s, axis=-1, keepdims=True))
        alpha = jnp.exp2(m - m_new)
        p = jnp.exp2(s - m_new)
        m_s[...] = m_new
        l_s[...] = alpha * l_s[...] + jnp.sum(p, axis=-1, keepdims=True)
        acc_s[...] = alpha * acc_s[...] + _dot(p.astype(BF16), v_ref[0, pl.ds(start, width), :])

    if rem:
        scores(s_b, n_full * KV_CHUNK, rem)
    if n_full:
        scores(s_a, 0, KV_CHUNK)
    if rem:
        update(s_b, n_full * KV_CHUNK, rem, True)
    pairs = max((n_full - 1) // 2, 0)
    tail = [(c * KV_CHUNK, KV_CHUNK, False) for c in range(2 * pairs, n_full)]

    def pair(i, carry):
        c0 = pl.multiple_of(2 * i * KV_CHUNK, KV_CHUNK)
        scores(s_b, c0 + KV_CHUNK, KV_CHUNK)
        update(s_a, c0, KV_CHUNK, False)
        scores(s_a, c0 + 2 * KV_CHUNK, KV_CHUNK)
        update(s_b, c0 + KV_CHUNK, KV_CHUNK, False)
        return carry

    lax.fori_loop(0, pairs, pair, 0)
    bufs = (s_a, s_b)
    for n, (start, width, mask) in enumerate(tail):
        if n + 1 < len(tail):
            scores(bufs[(n + 1) % 2], tail[n + 1][0], tail[n + 1][1])
        update(bufs[n % 2], start, width, mask)
    o_ref[0] = acc_s[...] / l_s[...]


def _attention(q, kt, v, *, batch, padded_len, seq_len):
    tq = ATTN_Q_TILE
    n_full = seq_len // KV_CHUNK
    rem = -(-(seq_len - n_full * KV_CHUNK) // LANES) * LANES
    wide = N_HEADS * HEAD_PAD
    return pl.pallas_call(
        functools.partial(_attn_body, seq_len=seq_len, n_full=n_full, rem=rem),
        grid=(batch, N_HEADS, padded_len // tq),
        in_specs=[pl.BlockSpec((1, tq, HEAD_PAD), lambda b, h, i: (b, i, h)),
                  pl.BlockSpec((1, 1, HEAD_PAD, padded_len), lambda b, h, i: (b, h, 0, 0)),
                  pl.BlockSpec((1, padded_len, HEAD_PAD), lambda b, h, i: (b, 0, h))],
        out_specs=pl.BlockSpec((1, tq, HEAD_PAD), lambda b, h, i: (b, i, h)),
        out_shape=jax.ShapeDtypeStruct((batch, padded_len, wide), F32),
        scratch_shapes=[pltpu.VMEM((tq, KV_CHUNK), F32), pltpu.VMEM((tq, KV_CHUNK), F32),
                        pltpu.VMEM((tq, 1), F32), pltpu.VMEM((tq, 1), F32), pltpu.VMEM((tq, HEAD_PAD), F32)],
        compiler_params=_cparams("parallel", "parallel", "arbitrary"),
        name="attention",
    )(q, kt, v)


def _outproj_body(hf_ref, hb_ref, gate_ref, o_ref, h_ref, grg_ref, gat_ref, wrg_ref, wat_ref, g2_ref,
                  wqt_ref, sk_ref, hn_ref, xn2_ref, sc_ref):
    rg = (hf_ref[...] + hb_ref[...]) * _gelu_tanh(gate_ref[...])
    rgn = _rms(rg, grg_ref[...])
    o = o_ref[...]
    ms = jnp.sum(o * o, axis=-1, keepdims=True) * (1.0 / (N_HEADS * V_DIM))
    attn = o * lax.rsqrt(ms + EPS) * gat_ref[...]
    hn = h_ref[...] + _dot(rgn.astype(BF16), wrg_ref[...]) + _dot(attn.astype(BF16), wat_ref[...])
    hn_ref[...] = hn
    xn2 = _rms(hn, g2_ref[...]).astype(BF16)
    xn2_ref[...] = xn2
    qpt = _dot_nt(wqt_ref[...], xn2)
    for h in range(PEER_HEADS):
        for s in range(2):
            r0 = (h * 2 + s) * PEER_HALF
            blk = qpt[r0:r0 + PEER_HALF, :].astype(BF16)
            o0 = s * PEER_HEADS * N_KEYS + h * N_KEYS
            sc_ref[o0:o0 + N_KEYS, :] = _dot(sk_ref[h * 2 + s], blk)


def _outproj(hf, hb, gate, o, h, w, *, rows):
    tm = ROW_TILE
    row = lambda n: pl.BlockSpec((tm, n), lambda i: (i, 0))
    wide = N_HEADS * HEAD_PAD
    nsc = 2 * PEER_HEADS * N_KEYS
    return pl.pallas_call(
        _outproj_body,
        grid=(rows // tm,),
        in_specs=[row(RG_WIDTH), row(RG_WIDTH), row(RG_WIDTH), row(wide), row(D_MODEL),
                  _full((1, RG_WIDTH)), _full((1, wide)), _full((RG_WIDTH, D_MODEL)), _full((wide, D_MODEL)),
                  _full((1, D_MODEL)), _full((nsc, D_MODEL)), _full((2 * PEER_HEADS, N_KEYS, PEER_HALF))],
        out_specs=[row(D_MODEL), row(D_MODEL), pl.BlockSpec((nsc, tm), lambda i: (0, i))],
        out_shape=[jax.ShapeDtypeStruct((rows, D_MODEL), F32), jax.ShapeDtypeStruct((rows, D_MODEL), BF16),
                   jax.ShapeDtypeStruct((nsc, rows), F32)],
        compiler_params=_cparams("parallel"),
        name="outproj",
    )(hf, hb, gate, o, h, w["grg"], w["gat"], w["wout_rg"], w["wout_at"], w["g2"], w["wqt"], w["subk"])


def _sort_network(n):
    pairs, p = [], 1
    while p < n:
        k = p
        while k >= 1:
            for j in range(k % p, n - k, 2 * k):
                for i in range(min(k, n - j - k)):
                    if (i + j) // (2 * p) == (i + j + k) // (2 * p):
                        pairs.append((i + j, i + j + k))
            k //= 2
        p *= 2
    return pairs


def _top_rows(x, count):
    n = x.shape[0] // SUBLANES
    cols = [x[k * SUBLANES:(k + 1) * SUBLANES, :] for k in range(n)]
    for a, b in _sort_network(n):
        cols[a], cols[b] = jnp.maximum(cols[a], cols[b]), jnp.minimum(cols[a], cols[b])
    cols.append(jnp.full_like(cols[0], NEG_INF))
    rows = []
    for r in range(count):
        m = jnp.max(cols[0], axis=0, keepdims=True)
        rows.append(m)
        depth = min(n, count - 1 - r)
        hit = cols[0] == m
        for k in range(depth):
            cols[k] = jnp.where(hit, cols[k + 1], cols[k])
    return rows


def _route_body(sc_ref, th_ref, e1_ref, e2_ref):
    t = sc_ref.shape[1]
    half = PEER_HEADS * N_KEYS
    jrow = lax.broadcasted_iota(jnp.int32, (SUBLANES, t), 0)

    def bc(row):
        return jnp.broadcast_to(row, (SUBLANES, t))

    def head(h, carry):
        r0 = pl.multiple_of(h * N_KEYS, N_KEYS)
        s1 = sc_ref[pl.ds(r0, N_KEYS), :]
        s2 = sc_ref[pl.ds(half + r0, N_KEYS), :]
        a = _top_rows(s1, PEER_TOPK + 1)
        b = _top_rows(s2, PEER_TOPK + 1)
        b_lo = jnp.concatenate(b[:SUBLANES], axis=0)
        b_hi = jnp.concatenate(b[SUBLANES:2 * SUBLANES], axis=0)
        a_hi = jnp.concatenate(a[SUBLANES:2 * SUBLANES], axis=0)
        pieces = [bc(a[0]) + b_lo, bc(a[0]) + b_hi, bc(a[1]) + b_lo]
        for i in range(2, SUBLANES):
            pieces.append(jnp.where(jrow < (PEER_TOPK + 1) // (i + 1), bc(a[i]) + b_lo, NEG_INF))
        pieces.append(a_hi + bc(b[0]))
        ends = jnp.concatenate([a[0] + b[PEER_TOPK], a[PEER_TOPK] + b[0]]
                               + [jnp.full((1, t), NEG_INF, F32)] * (SUBLANES - 2), axis=0)
        pieces.append(ends)
        cand = jnp.concatenate(pieces, axis=0)
        x = cand
        for _ in range(PEER_TOPK - 1):
            x = jnp.where(x == jnp.max(x, axis=0, keepdims=True), NEG_INF, x)
        c16 = jnp.max(x, axis=0, keepdims=True)
        c17 = jnp.max(jnp.where(x == c16, NEG_INF, x), axis=0, keepdims=True)
        thr = 0.5 * (c16 + c17)
        top = a[0] + b[0]
        z = jnp.sum(jnp.where(cand >= thr, jnp.exp(cand - top), 0.0), axis=0, keepdims=True)
        th_ref[pl.ds(r0, N_KEYS), :] = jnp.exp((thr - s1) - b[0])
        e1_ref[pl.ds(r0, N_KEYS), :] = jnp.exp(s1 - a[0]) / z
        e2_ref[pl.ds(r0, N_KEYS), :] = jnp.exp(s2 - b[0])
        return carry

    lax.fori_loop(0, PEER_HEADS, head, 0)


def _route(sc, *, rows):
    t = ROUTE_TOKENS
    half = PEER_HEADS * N_KEYS
    out = pl.BlockSpec((half, t), lambda i: (0, i))
    return pl.pallas_call(
        _route_body,
        grid=(rows // t,),
        in_specs=[pl.BlockSpec((2 * half, t), lambda i: (0, i))],
        out_specs=[out, out, out],
        out_shape=[jax.ShapeDtypeStruct((half, rows), F32)] * 3,
        compiler_params=_cparams("parallel"),
        name="peer_route",
    )(sc)


def _peer_body(xn_ref, h_ref, th_ref, e1_ref, e2_ref, u_first, u_next, vt_ref, out_ref, acc_s, hid_a, hid_b, w_s):
    j = pl.program_id(1)
    tt = xn_ref.shape[0]
    half_t = tt // 2

    @pl.when(j == 0)
    def _():
        acc_s[...] = jnp.zeros_like(acc_s)
        hid_a[...] = _dot_nt(u_first[...], xn_ref[...])

    i1_base = pl.multiple_of(j * PEER_I1_BLOCK, PEER_I1_BLOCK)

    def step(hid_cur, hid_nxt):
        for il in range(PEER_I1_BLOCK):
            es = slice(il * N_KEYS, (il + 1) * N_KEYS)
            for lt in range(tt // LANES):
                ln = slice(lt * LANES, (lt + 1) * LANES)
                g = jnp.zeros((N_KEYS, LANES), F32)
                for h in range(PEER_HEADS):
                    hs = slice(h * N_KEYS, (h + 1) * N_KEYS)
                    th = th_ref[pl.ds(h * N_KEYS + i1_base, PEER_I1_BLOCK), ln][il:il + 1, :]
                    e1 = e1_ref[pl.ds(h * N_KEYS + i1_base, PEER_I1_BLOCK), ln][il:il + 1, :]
                    e2 = e2_ref[hs, ln]
                    g = g + jnp.where(e2 >= th, e2, 0.0) * e1
                w_s[es, ln] = (g * _gelu_tanh(hid_cur[es, ln])).astype(BF16)
            if il in (1, 5):
                tok = slice((il // 4) * half_t, (il // 4 + 1) * half_t)
                hid_nxt[:, tok] = _dot_nt(u_next[...], xn_ref[tok, :])
        acc_s[...] += _dot(vt_ref[...], w_s[...])

    pl.when(j % 2 == 0)(functools.partial(step, hid_a, hid_b))
    pl.when(j % 2 == 1)(functools.partial(step, hid_b, hid_a))

    @pl.when(j == pl.num_programs(1) - 1)
    def _():
        out_ref[...] = h_ref[...] + acc_s[...].T


def _peer(xn2, hn, th, e1, e2, u, vt, *, rows):
    tt = PEER_TOKENS
    eb = PEER_I1_BLOCK * N_KEYS
    half = PEER_HEADS * N_KEYS
    tok = pl.BlockSpec((tt, D_MODEL), lambda i, j: (i, 0))
    tab = pl.BlockSpec((half, tt), lambda i, j: (0, i))
    nb = N_EXPERTS // eb
    return pl.pallas_call(
        _peer_body,
        grid=(rows // tt, nb),
        in_specs=[tok, tok, tab, tab, tab,
                  pl.BlockSpec((eb, D_MODEL), lambda i, j: (0, 0)),
                  pl.BlockSpec((eb, D_MODEL), lambda i, j: (jnp.minimum(j + 1, nb - 1), 0)),
                  pl.BlockSpec((D_MODEL, eb), lambda i, j: (0, j))],
        out_specs=pl.BlockSpec((tt, D_MODEL), lambda i, j: (i, 0)),
        out_shape=jax.ShapeDtypeStruct((rows, D_MODEL), F32),
        scratch_shapes=[pltpu.VMEM((D_MODEL, tt), F32), pltpu.VMEM((eb, tt), F32), pltpu.VMEM((eb, tt), F32),
                        pltpu.VMEM((eb, tt), BF16)],
        compiler_params=_cparams("parallel", "arbitrary"),
        name="peer_dense",
    )(xn2, hn, th, e1, e2, u, u, vt)


def _block_diag(w):
    eye = jnp.eye(RG_HEADS, dtype=w.dtype)
    return jnp.einsum("hij,hg->higj", w, eye).reshape(RG_WIDTH, RG_WIDTH)


def _head_pad(w, lo, hi):
    k = w.shape[0]
    part = w[:, :, lo:hi]
    return jnp.pad(part, ((0, 0), (0, 0), (0, HEAD_PAD - (hi - lo)))).reshape(k, N_HEADS * HEAD_PAD)


def _rope_partner(x1, x2):
    z64 = jnp.zeros(x1.shape[:-1] + (QK_NOPE,), x1.dtype)
    z32 = jnp.zeros(x1.shape[:-1] + (HEAD_PAD - QK_DIM,), x1.dtype)
    return jnp.concatenate([z64, x2, x1, z32], axis=-1)


def _layer_weights(i, norm1_g, w_in, conv_w, conv_b, rg_wa, rg_ba, rg_wi, rg_bi, rg_lambda, q_norm_g, w_uq,
                   kv_norm_g, w_ukv, q_head_g, k_head_g, out_g_rg, out_g_attn, w_out, norm2_g, peer_wq,
                   peer_subkeys, peer_u, peer_v):
    half = QK_ROPE // 2
    wq3 = w_uq[i].reshape(Q_LORA, N_HEADS, QK_DIM)
    wkv3 = w_ukv[i].reshape(KV_LORA, N_HEADS, QK_NOPE + V_DIM)
    eye = jnp.eye(QK_ROPE, dtype=F32)
    ek = jnp.concatenate([jnp.zeros((QK_ROPE, QK_NOPE), F32), eye,
                          jnp.zeros((QK_ROPE, HEAD_PAD - QK_DIM), F32)], axis=-1)
    eks = _rope_partner(eye[:, :half], eye[:, half:])
    place = lambda e: jnp.pad(jnp.tile(e, (1, N_HEADS)), ((0, LANES - QK_ROPE), (0, 0))).astype(BF16)

    def head_gain(g):
        main = jnp.pad(g, (0, HEAD_PAD - QK_DIM)).reshape(1, HEAD_PAD)
        swap = _rope_partner(g[QK_NOPE:QK_NOPE + half], g[QK_NOPE + half:]).reshape(1, HEAD_PAD)
        return main, swap

    gq, gqs = head_gain(q_head_g[i])
    gk, gks = head_gain(k_head_g[i])
    wout = w_out[i]
    wout_at = jnp.pad(wout[RG_WIDTH:].reshape(N_HEADS, V_DIM, D_MODEL),
                      ((0, 0), (0, HEAD_PAD - V_DIM), (0, 0))).reshape(N_HEADS * HEAD_PAD, D_MODEL)
    gat = jnp.pad(out_g_attn[i].reshape(N_HEADS, V_DIM), ((0, 0), (0, HEAD_PAD - V_DIM))).reshape(1, -1)
    return dict(
        g1=norm1_g[i].reshape(1, -1),
        win=jnp.pad(w_in[i], ((0, 0), (0, P_IN_PAD - P_IN))).astype(BF16),
        qg=q_norm_g[i].reshape(1, -1),
        wq=_head_pad(wq3, 0, QK_DIM).astype(BF16),
        wqs=_rope_partner(wq3[:, :, QK_NOPE:QK_NOPE + half], wq3[:, :, QK_NOPE + half:])
        .reshape(Q_LORA, -1).astype(BF16),
        kvg=kv_norm_g[i].reshape(1, -1),
        wk=_head_pad(wkv3, 0, QK_NOPE).astype(BF16),
        wv=_head_pad(wkv3, QK_NOPE, QK_NOPE + V_DIM).astype(BF16),
        ek=place(ek), eks=place(eks), gq=gq, gqs=gqs, gk=gk, gks=gks,
        conv_w=conv_w[i], conv_b=conv_b[i].reshape(1, -1),
        wgate_f=jnp.concatenate([_block_diag(rg_wa[i, 0]), _block_diag(rg_wi[i, 0])], axis=1).astype(BF16),
        wgate_b=jnp.concatenate([_block_diag(rg_wa[i, 1]), _block_diag(rg_wi[i, 1])], axis=1).astype(BF16),
        gate_bias=jnp.stack([rg_ba[i, 0].reshape(-1), rg_bi[i, 0].reshape(-1),
                             rg_ba[i, 1].reshape(-1), rg_bi[i, 1].reshape(-1)]),
        lam=rg_lambda[i],
        grg=out_g_rg[i].reshape(1, -1), gat=gat,
        wout_rg=wout[:RG_WIDTH].astype(BF16), wout_at=wout_at.astype(BF16),
        g2=norm2_g[i].reshape(1, -1),
        wqt=peer_wq[i].T.astype(BF16),
        subk=peer_subkeys[i].reshape(2 * PEER_HEADS, N_KEYS, PEER_HALF).astype(BF16),
        u=peer_u[i].astype(BF16),
        vt=peer_v[i].T.astype(BF16),
    )


def _rope_tables(length):
    pos = jnp.arange(length, dtype=F32)
    inv = ROPE_THETA ** (-jnp.arange(0, QK_ROPE, 2, dtype=F32) / QK_ROPE)
    ang = pos[:, None] * inv[None, :]
    cos, sin = jnp.cos(ang), jnp.sin(ang)
    ones = jnp.ones((length, QK_NOPE), F32)
    zpad = jnp.zeros((length, HEAD_PAD - QK_DIM), F32)
    cos_t = jnp.concatenate([ones, cos, cos, zpad], axis=-1)
    sin_t = jnp.concatenate([jnp.zeros((length, QK_NOPE), F32), -sin, sin, zpad], axis=-1)
    return cos_t, sin_t


def _padded_len(length):
    tile = math.lcm(ROW_TILE, SCAN_CHUNK, ATTN_Q_TILE, PEER_TOKENS)
    return -(-length // tile) * tile


def _encode(x, meta_tokens, layers):
    batch, n_tok, _ = x.shape
    seq_len = n_tok + N_META
    lp = _padded_len(seq_len)
    rows = batch * lp
    meta = jnp.broadcast_to(meta_tokens[None].astype(x.dtype), (batch, N_META, D_MODEL))
    h = jnp.concatenate([meta, x, jnp.zeros((batch, lp - seq_len, D_MODEL), x.dtype)], axis=1)
    h = h.reshape(rows, D_MODEL)
    cos_t, sin_t = _rope_tables(lp)
    wide = N_HEADS * HEAD_PAD
    for w in layers:
        xr, gate, q, k, v = _inproj(h, cos_t, sin_t, w, rows=rows, pos_tiles=lp // ROW_TILE)
        hf, hb = _scan(xr, w, batch=batch, padded_len=lp, seq_len=seq_len)
        kt = k.reshape(batch, lp, N_HEADS, HEAD_PAD).transpose(0, 2, 3, 1)
        o = _attention(q.reshape(batch, lp, wide), kt, v.reshape(batch, lp, wide),
                       batch=batch, padded_len=lp, seq_len=seq_len)
        hn, xn2, sc = _outproj(hf, hb, gate, o.reshape(rows, wide), h, w, rows=rows)
        th, e1, e2 = _route(sc, rows=rows)
        h = _peer(xn2, hn, th, e1, e2, w["u"], w["vt"], rows=rows)
    return h.reshape(batch, lp, D_MODEL)[:, N_META:seq_len]


def kernel(x_prompt, x_sample, meta_tokens, norm1_g, w_in, conv_w, conv_b, rg_wa, rg_ba, rg_wi, rg_bi, rg_lambda, q_norm_g, w_uq, kv_norm_g, w_ukv, q_head_g, k_head_g, out_g_rg, out_g_attn, w_out, norm2_g, peer_wq, peer_subkeys, peer_u, peer_v):
    params = (norm1_g, w_in, conv_w, conv_b, rg_wa, rg_ba, rg_wi, rg_bi, rg_lambda, q_norm_g, w_uq, kv_norm_g,
              w_ukv, q_head_g, k_head_g, out_g_rg, out_g_attn, w_out, norm2_g, peer_wq, peer_subkeys, peer_u,
              peer_v)
    layers = [_layer_weights(i, *params) for i in range(norm1_g.shape[0])]
    y_prompt = _encode(x_prompt, meta_tokens, layers)
    y_sample = _encode(x_sample, meta_tokens, layers)
    return (y_prompt, y_sample)
```

```python
import functools
import math

import jax
import jax.numpy as jnp
from jax import lax
from jax.experimental import pallas as pl
from jax.experimental.pallas import tpu as pltpu

F32 = jnp.float32
BF16 = jnp.bfloat16

D_MODEL = 1024
N_META = 16
EPS = 1e-6
RG_WIDTH = 512
RG_HEADS = 8
RG_BLOCK = 64
CONV_W = 4
RG_C = 8.0
N_HEADS = 8
QK_NOPE = 64
QK_ROPE = 32
QK_DIM = 96
V_DIM = 64
Q_LORA = 256
KV_LORA = 128
ROPE_THETA = 10000.0
P_IN = 2 * RG_WIDTH + Q_LORA + KV_LORA + QK_ROPE
PEER_HEADS = 8
N_KEYS = 128
N_EXPERTS = N_KEYS * N_KEYS
PEER_TOPK = 16
PEER_HALF = 128

LANES = 128
SUBLANES = 8
HEAD_PAD = LANES
P_IN_PAD = 12 * LANES
ROW_TILE = 512
SCAN_CHUNK = 256
ATTN_Q_TILE = 512
KV_CHUNK = 1024
PEER_TOKENS = 512
ROUTE_TOKENS = 256
PEER_I1_BLOCK = 16
VMEM_LIMIT = 52 * 1024 * 1024
NEG_INF = float("-inf")
MASK_VALUE = -1e30
LOG2E = 1.4426950408889634


def _cparams(*sem):
    return pltpu.CompilerParams(dimension_semantics=sem, vmem_limit_bytes=VMEM_LIMIT)


def _rms(x, g):
    return x * lax.rsqrt(jnp.mean(x * x, axis=-1, keepdims=True) + EPS) * g


def _gelu_tanh(x):
    return x * (0.5 * (1.0 + jnp.tanh(0.7978845608028654 * (x + 0.044715 * (x * x * x)))))


def _sigmoid(x):
    return 1.0 / (1.0 + jnp.exp(-x))


def _neg_expm1(y):
    u = jnp.exp(y)
    um1 = u - 1.0
    tiny = um1 == 0.0
    r = jnp.where(tiny, y, um1 * y / jnp.where(tiny, 1.0, jnp.log(u)))
    return -jnp.where(um1 == -1.0, -1.0, r)


def _dot(a, b):
    return jnp.dot(a, b, preferred_element_type=F32)


def _dot_nt(a, b):
    return lax.dot_general(a, b, (((1,), (1,)), ((), ())), preferred_element_type=F32)


def _full(shape):
    n = len(shape)
    return pl.BlockSpec(shape, lambda *_: (0,) * n)


def _inproj_body(x_ref, cos_ref, sin_ref, g1_ref, win_ref, qg_ref, wq_ref, wqs_ref, kvg_ref,
                 wk_ref, ek_ref, eks_ref, wv_ref, gq_ref, gqs_ref, gk_ref, gks_ref,
                 xr_ref, gate_ref, q_ref, k_ref, v_ref):
    xn = _rms(x_ref[...], g1_ref[...])
    p = _dot(xn.astype(BF16), win_ref[...])
    xr_ref[...] = p[:, :RG_WIDTH]
    gate_ref[...] = p[:, RG_WIDTH:2 * RG_WIDTH]
    o2 = 2 * RG_WIDTH
    qcn = _rms(p[:, o2:o2 + Q_LORA], qg_ref[...]).astype(BF16)
    kvn = _rms(p[:, o2 + Q_LORA:o2 + Q_LORA + KV_LORA], kvg_ref[...]).astype(BF16)
    kpe = p[:, o2 + Q_LORA + KV_LORA:].astype(BF16)
    q = _dot(qcn, wq_ref[...])
    qs = _dot(qcn, wqs_ref[...])
    k = _dot(kvn, wk_ref[...]) + _dot(kpe, ek_ref[...])
    ks = _dot(kpe, eks_ref[...])
    v_ref[...] = _dot(kvn, wv_ref[...]).astype(BF16)
    c = cos_ref[...]
    s = sin_ref[...]
    qscale = QK_DIM ** -0.5 * LOG2E
    for h in range(N_HEADS):
        sl = slice(h * HEAD_PAD, (h + 1) * HEAD_PAD)
        qh = q[:, sl]
        rq = lax.rsqrt(jnp.sum(qh * qh, axis=-1, keepdims=True) * (1.0 / QK_DIM) + EPS)
        q_ref[:, sl] = ((qh * gq_ref[...] * c + qs[:, sl] * gqs_ref[...] * s) * (rq * qscale)).astype(BF16)
        kh = k[:, sl]
        rk = lax.rsqrt(jnp.sum(kh * kh, axis=-1, keepdims=True) * (1.0 / QK_DIM) + EPS)
        k_ref[:, sl] = ((kh * gk_ref[...] * c + ks[:, sl] * gks_ref[...] * s) * rk).astype(BF16)


def _inproj(h, cos_t, sin_t, w, *, rows, pos_tiles):
    tm = ROW_TILE
    row = lambda n: pl.BlockSpec((tm, n), lambda i: (i, 0))
    pos = pl.BlockSpec((tm, HEAD_PAD), lambda i: (i % pos_tiles, 0))
    wide = N_HEADS * HEAD_PAD
    return pl.pallas_call(
        _inproj_body,
        grid=(rows // tm,),
        in_specs=[row(D_MODEL), pos, pos, _full((1, D_MODEL)), _full((D_MODEL, P_IN_PAD)),
                  _full((1, Q_LORA)), _full((Q_LORA, wide)), _full((Q_LORA, wide)),
                  _full((1, KV_LORA)), _full((KV_LORA, wide)), _full((LANES, wide)), _full((LANES, wide)),
                  _full((KV_LORA, wide)),
                  _full((1, HEAD_PAD)), _full((1, HEAD_PAD)), _full((1, HEAD_PAD)), _full((1, HEAD_PAD))],
        out_specs=[row(RG_WIDTH), row(RG_WIDTH), row(wide), row(wide), row(wide)],
        out_shape=[jax.ShapeDtypeStruct((rows, RG_WIDTH), F32), jax.ShapeDtypeStruct((rows, RG_WIDTH), F32),
                   jax.ShapeDtypeStruct((rows, wide), BF16), jax.ShapeDtypeStruct((rows, wide), BF16),
                   jax.ShapeDtypeStruct((rows, wide), BF16)],
        compiler_params=_cparams("parallel"),
        name="inproj",
    )(h, cos_t, sin_t, w["g1"], w["win"], w["qg"], w["wq"], w["wqs"], w["kvg"], w["wk"], w["ek"], w["eks"],
      w["wv"], w["gq"], w["gqs"], w["gk"], w["gks"])


def _scan_body(xm_f, xp_f, xn_f, xm_b, xp_b, xn_b, cw_ref, cb_ref, wf_ref, wb_ref, bias_ref, lam_ref,
               hf_ref, hb_ref, ext_s, a_s, b_s, cf_s, cb_s, *, seq_len, chunk, n_chunks):
    i = pl.program_id(1)
    C = chunk
    groups = C // SUBLANES

    @pl.when(i == 0)
    def _():
        cf_s[...] = jnp.zeros_like(cf_s)
        cb_s[...] = jnp.zeros_like(cb_s)

    rows8 = lax.broadcasted_iota(jnp.int32, (SUBLANES, RG_WIDTH), 0)
    rows_c = lax.broadcasted_iota(jnp.int32, (C, RG_WIDTH), 0)
    rmod = rows_c & (SUBLANES - 1)

    def masked(x, pos):
        return jnp.where((pos >= 0) & (pos < seq_len), x, 0.0)

    def affine_terms(xm, xp, xn, j, w_ref, d):
        base = j * C
        ext_s[0:SUBLANES, :] = masked(xp[...], base - SUBLANES + rows8)
        ext_s[SUBLANES:SUBLANES + C, :] = masked(xm[...], base + rows_c)
        ext_s[SUBLANES + C:, :] = masked(xn[...], base + C + rows8)
        xc = cb_ref[...]
        for t in range(CONV_W):
            xc = xc + cw_ref[t:t + 1, :] * ext_s[SUBLANES - 2 + t:SUBLANES - 2 + t + C, :]
        gm = _dot(xc.astype(BF16), w_ref[...])
        r = _sigmoid(gm[:, :RG_WIDTH] + bias_ref[2 * d:2 * d + 1, :])
        ig = _sigmoid(gm[:, RG_WIDTH:] + bias_ref[2 * d + 1:2 * d + 2, :])
        z = -lam_ref[d:d + 1, :]
        softplus = jnp.maximum(z, 0.0) + jnp.log1p(jnp.exp(-jnp.abs(z)))
        log_a = (-RG_C) * r * softplus
        a = jnp.exp(log_a)
        b = jnp.sqrt(_neg_expm1(2.0 * log_a)) * (ig * xc)
        b = jnp.where(base + rows_c < seq_len, b, 0.0)
        return a, b

    def rotate_in_groups(x, shift):
        return pltpu.roll(x.reshape(groups, SUBLANES, RG_WIDTH), shift, 1).reshape(C, RG_WIDTH)

    a, b = affine_terms(xm_f, xp_f, xn_f, i, wf_ref, 0)
    for d in (1, 2, 4):
        m = rmod >= d
        a_sh = rotate_in_groups(a, d)
        b_sh = rotate_in_groups(b, d)
        b = jnp.where(m, a * b_sh + b, b)
        a = jnp.where(m, a * a_sh, a)
    a_s[...] = a
    b_s[...] = b

    def fwd_group(g, carry):
        r0 = pl.multiple_of(g * SUBLANES, SUBLANES)
        hrow = a_s[pl.ds(r0, SUBLANES), :] * carry + b_s[pl.ds(r0, SUBLANES), :]
        hf_ref[pl.ds(r0, SUBLANES), :] = hrow
        return jnp.broadcast_to(hrow[SUBLANES - 1:SUBLANES, :], (SUBLANES, RG_WIDTH))

    cf_s[...] = lax.fori_loop(0, groups, fwd_group, cf_s[...])

    a, b = affine_terms(xm_b, xp_b, xn_b, n_chunks - 1 - i, wb_ref, 1)
    for d in (1, 2, 4):
        m = rmod < SUBLANES - d
        a_sh = rotate_in_groups(a, SUBLANES - d)
        b_sh = rotate_in_groups(b, SUBLANES - d)
        b = jnp.where(m, a * b_sh + b, b)
        a = jnp.where(m, a * a_sh, a)
    a_s[...] = a
    b_s[...] = b

    def bwd_group(g, carry):
        r0 = pl.multiple_of((groups - 1 - g) * SUBLANES, SUBLANES)
        hrow = a_s[pl.ds(r0, SUBLANES), :] * carry + b_s[pl.ds(r0, SUBLANES), :]
        hb_ref[pl.ds(r0, SUBLANES), :] = hrow
        return jnp.broadcast_to(hrow[0:1, :], (SUBLANES, RG_WIDTH))

    cb_s[...] = lax.fori_loop(0, groups, bwd_group, cb_s[...])


def _scan(xr, w, *, batch, padded_len, seq_len):
    C = SCAN_CHUNK
    n = padded_len // C
    per8 = C // SUBLANES
    last8 = batch * padded_len // SUBLANES - 1

    def chunk_f(b, i):
        return b * n + i

    def chunk_b(b, i):
        return b * n + (n - 1 - i)

    def specs(chunk_of):
        main = pl.BlockSpec((C, RG_WIDTH), lambda b, i: (chunk_of(b, i), 0))
        prev = pl.BlockSpec((SUBLANES, RG_WIDTH), lambda b, i: (jnp.maximum(chunk_of(b, i) * per8 - 1, 0), 0))
        nxt = pl.BlockSpec((SUBLANES, RG_WIDTH),
                           lambda b, i: (jnp.minimum((chunk_of(b, i) + 1) * per8, last8), 0))
        return [main, prev, nxt]

    rows = batch * padded_len
    return pl.pallas_call(
        functools.partial(_scan_body, seq_len=seq_len, chunk=C, n_chunks=n),
        grid=(batch, n),
        in_specs=specs(chunk_f) + specs(chunk_b) + [
            _full((CONV_W, RG_WIDTH)), _full((1, RG_WIDTH)),
            _full((RG_WIDTH, 2 * RG_WIDTH)), _full((RG_WIDTH, 2 * RG_WIDTH)),
            _full((4, RG_WIDTH)), _full((2, RG_WIDTH))],
        out_specs=[pl.BlockSpec((C, RG_WIDTH), lambda b, i: (chunk_f(b, i), 0)),
                   pl.BlockSpec((C, RG_WIDTH), lambda b, i: (chunk_b(b, i), 0))],
        out_shape=[jax.ShapeDtypeStruct((rows, RG_WIDTH), F32), jax.ShapeDtypeStruct((rows, RG_WIDTH), F32)],
        scratch_shapes=[pltpu.VMEM((C + 2 * SUBLANES, RG_WIDTH), F32), pltpu.VMEM((C, RG_WIDTH), F32),
                        pltpu.VMEM((C, RG_WIDTH), F32), pltpu.VMEM((SUBLANES, RG_WIDTH), F32),
                        pltpu.VMEM((SUBLANES, RG_WIDTH), F32)],
        compiler_params=_cparams("arbitrary", "arbitrary"),
        name="rglru_scan",
    )(xr, xr, xr, xr, xr, xr, w["conv_w"], w["conv_b"], w["wgate_f"], w["wgate_b"], w["gate_bias"], w["lam"])


def _attn_body(q_ref, kt_ref, v_ref, o_ref, s_a, s_b, m_s, l_s, acc_s, *, seq_len, n_full, rem):
    tq = q_ref.shape[1]
    m_s[...] = jnp.full_like(m_s, NEG_INF)
    l_s[...] = jnp.zeros_like(l_s)
    acc_s[...] = jnp.zeros_like(acc_s)

    def scores(dst, start, width):
        dst[:, :width] = _dot(q_ref[0], kt_ref[0, 0, :, pl.ds(start, width)])

    def update(src, start, width, mask):
        s = src[:, :width]
        if mask:
            kpos = start + lax.broadcasted_iota(jnp.int32, (tq, width), 1)
            s = jnp.where(kpos < seq_len, s, MASK_VALUE)
        m = m_s[...]
        m_new = jnp.maximum(m, jnp.max(s, axis=-1, keepdims=True))
        alpha = jnp.exp2(m - m_new)
        p = jnp.exp2(s - m_new)
        m_s[...] = m_new
        l_s[...] = alpha * l_s[...] + jnp.sum(p, axis=-1, keepdims=True)
        acc_s[...] = alpha * acc_s[...] + _dot(p.astype(BF16), v_ref[0, pl.ds(start, width), :])

    if rem:
        scores(s_b, n_full * KV_CHUNK, rem)
    if n_full:
        scores(s_a, 0, KV_CHUNK)
    if rem:
        update(s_b, n_full * KV_CHUNK, rem, True)
    pairs = max((n_full - 1) // 2, 0)
    tail = [(c * KV_CHUNK, KV_CHUNK, False) for c in range(2 * pairs, n_full)]

    def pair(i, carry):
        c0 = pl.multiple_of(2 * i * KV_CHUNK, KV_CHUNK)
        scores(s_b, c0 + KV_CHUNK, KV_CHUNK)
        update(s_a, c0, KV_CHUNK, False)
        scores(s_a, c0 + 2 * KV_CHUNK, KV_CHUNK)
        update(s_b, c0 + KV_CHUNK, KV_CHUNK, False)
        return carry

    lax.fori_loop(0, pairs, pair, 0)
    bufs = (s_a, s_b)
    for n, (start, width, mask) in enumerate(tail):
        if n + 1 < len(tail):
            scores(bufs[(n + 1) % 2], tail[n + 1][0], tail[n + 1][1])
        update(bufs[n % 2], start, width, mask)
    o_ref[0] = acc_s[...] / l_s[...]


def _attention(q, kt, v, *, batch, padded_len, seq_len):
    tq = ATTN_Q_TILE
    n_full = seq_len // KV_CHUNK
    rem = -(-(seq_len - n_full * KV_CHUNK) // LANES) * LANES
    wide = N_HEADS * HEAD_PAD
    return pl.pallas_call(
        functools.partial(_attn_body, seq_len=seq_len, n_full=n_full, rem=rem),
        grid=(batch, N_HEADS, padded_len // tq),
        in_specs=[pl.BlockSpec((1, tq, HEAD_PAD), lambda b, h, i: (b, i, h)),
                  pl.BlockSpec((1, 1, HEAD_PAD, padded_len), lambda b, h, i: (b, h, 0, 0)),
                  pl.BlockSpec((1, padded_len, HEAD_PAD), lambda b, h, i: (b, 0, h))],
        out_specs=pl.BlockSpec((1, tq, HEAD_PAD), lambda b, h, i: (b, i, h)),
        out_shape=jax.ShapeDtypeStruct((batch, padded_len, wide), F32),
        scratch_shapes=[pltpu.VMEM((tq, KV_CHUNK), F32), pltpu.VMEM((tq, KV_CHUNK), F32),
                        pltpu.VMEM((tq, 1), F32), pltpu.VMEM((tq, 1), F32), pltpu.VMEM((tq, HEAD_PAD), F32)],
        compiler_params=_cparams("parallel", "parallel", "arbitrary"),
        name="attention",
    )(q, kt, v)


def _outproj_body(hf_ref, hb_ref, gate_ref, o_ref, h_ref, grg_ref, gat_ref, wrg_ref, wat_ref, g2_ref,
                  wqt_ref, sk_ref, hn_ref, xn2_ref, sc_ref):
    rg = (hf_ref[...] + hb_ref[...]) * _gelu_tanh(gate_ref[...])
    rgn = _rms(rg, grg_ref[...])
    o = o_ref[...]
    ms = jnp.sum(o * o, axis=-1, keepdims=True) * (1.0 / (N_HEADS * V_DIM))
    attn = o * lax.rsqrt(ms + EPS) * gat_ref[...]
    hn = h_ref[...] + _dot(rgn.astype(BF16), wrg_ref[...]) + _dot(attn.astype(BF16), wat_ref[...])
    hn_ref[...] = hn
    xn2 = _rms(hn, g2_ref[...]).astype(BF16)
    xn2_ref[...] = xn2
    qpt = _dot_nt(wqt_ref[...], xn2)
    for h in range(PEER_HEADS):
        for s in range(2):
            r0 = (h * 2 + s) * PEER_HALF
            blk = qpt[r0:r0 + PEER_HALF, :].astype(BF16)
            o0 = s * PEER_HEADS * N_KEYS + h * N_KEYS
            sc_ref[o0:o0 + N_KEYS, :] = _dot(sk_ref[h * 2 + s], blk)


def _outproj(hf, hb, gate, o, h, w, *, rows):
    tm = ROW_TILE
    row = lambda n: pl.BlockSpec((tm, n), lambda i: (i, 0))
    wide = N_HEADS * HEAD_PAD
    nsc = 2 * PEER_HEADS * N_KEYS
    return pl.pallas_call(
        _outproj_body,
        grid=(rows // tm,),
        in_specs=[row(RG_WIDTH), row(RG_WIDTH), row(RG_WIDTH), row(wide), row(D_MODEL),
                  _full((1, RG_WIDTH)), _full((1, wide)), _full((RG_WIDTH, D_MODEL)), _full((wide, D_MODEL)),
                  _full((1, D_MODEL)), _full((nsc, D_MODEL)), _full((2 * PEER_HEADS, N_KEYS, PEER_HALF))],
        out_specs=[row(D_MODEL), row(D_MODEL), pl.BlockSpec((nsc, tm), lambda i: (0, i))],
        out_shape=[jax.ShapeDtypeStruct((rows, D_MODEL), F32), jax.ShapeDtypeStruct((rows, D_MODEL), BF16),
                   jax.ShapeDtypeStruct((nsc, rows), F32)],
        compiler_params=_cparams("parallel"),
        name="outproj",
    )(hf, hb, gate, o, h, w["grg"], w["gat"], w["wout_rg"], w["wout_at"], w["g2"], w["wqt"], w["subk"])


def _sort_network(n):
    pairs, p = [], 1
    while p < n:
        k = p
        while k >= 1:
            for j in range(k % p, n - k, 2 * k):
                for i in range(min(k, n - j - k)):
                    if (i + j) // (2 * p) == (i + j + k) // (2 * p):
                        pairs.append((i + j, i + j + k))
            k //= 2
        p *= 2
    return pairs


def _top_rows(x, count):
    n = x.shape[0] // SUBLANES
    cols = [x[k * SUBLANES:(k + 1) * SUBLANES, :] for k in range(n)]
    for a, b in _sort_network(n):
        cols[a], cols[b] = jnp.maximum(cols[a], cols[b]), jnp.minimum(cols[a], cols[b])
    cols.append(jnp.full_like(cols[0], NEG_INF))
    rows = []
    for r in range(count):
        m = jnp.max(cols[0], axis=0, keepdims=True)
        rows.append(m)
        depth = min(n, count - 1 - r)
        hit = cols[0] == m
        for k in range(depth):
            cols[k] = jnp.where(hit, cols[k + 1], cols[k])
    return rows


def _route_body(sc_ref, th_ref, e1_ref, e2_ref):
    t = sc_ref.shape[1]
    half = PEER_HEADS * N_KEYS
    jrow = lax.broadcasted_iota(jnp.int32, (SUBLANES, t), 0)

    def bc(row):
        return jnp.broadcast_to(row, (SUBLANES, t))

    def head(h, carry):
        r0 = pl.multiple_of(h * N_KEYS, N_KEYS)
        s1 = sc_ref[pl.ds(r0, N_KEYS), :]
        s2 = sc_ref[pl.ds(half + r0, N_KEYS), :]
        a = _top_rows(s1, PEER_TOPK + 1)
        b = _top_rows(s2, PEER_TOPK + 1)
        b_lo = jnp.concatenate(b[:SUBLANES], axis=0)
        b_hi = jnp.concatenate(b[SUBLANES:2 * SUBLANES], axis=0)
        a_hi = jnp.concatenate(a[SUBLANES:2 * SUBLANES], axis=0)
        pieces = [bc(a[0]) + b_lo, bc(a[0]) + b_hi, bc(a[1]) + b_lo]
        for i in range(2, SUBLANES):
            pieces.append(jnp.where(jrow < (PEER_TOPK + 1) // (i + 1), bc(a[i]) + b_lo, NEG_INF))
        pieces.append(a_hi + bc(b[0]))
        ends = jnp.concatenate([a[0] + b[PEER_TOPK], a[PEER_TOPK] + b[0]]
                               + [jnp.full((1, t), NEG_INF, F32)] * (SUBLANES - 2), axis=0)
        pieces.append(ends)
        cand = jnp.concatenate(pieces, axis=0)
        x = cand
        for _ in range(PEER_TOPK - 1):
            x = jnp.where(x == jnp.max(x, axis=0, keepdims=True), NEG_INF, x)
        c16 = jnp.max(x, axis=0, keepdims=True)
        c17 = jnp.max(jnp.where(x == c16, NEG_INF, x), axis=0, keepdims=True)
        thr = 0.5 * (c16 + c17)
        top = a[0] + b[0]
        z = jnp.sum(jnp.where(cand >= thr, jnp.exp(cand - top), 0.0), axis=0, keepdims=True)
        th_ref[pl.ds(r0, N_KEYS), :] = jnp.exp((thr - s1) - b[0])
        e1_ref[pl.ds(r0, N_KEYS), :] = jnp.exp(s1 - a[0]) / z
        e2_ref[pl.ds(r0, N_KEYS), :] = jnp.exp(s2 - b[0])
        return carry

    lax.fori_loop(0, PEER_HEADS, head, 0)


def _route(sc, *, rows):
    t = ROUTE_TOKENS
    half = PEER_HEADS * N_KEYS
    out = pl.BlockSpec((half, t), lambda i: (0, i))
    return pl.pallas_call(
        _route_body,
        grid=(rows // t,),
        in_specs=[pl.BlockSpec((2 * half, t), lambda i: (0, i))],
        out_specs=[out, out, out],
        out_shape=[jax.ShapeDtypeStruct((half, rows), F32)] * 3,
        compiler_params=_cparams("parallel"),
        name="peer_route",
    )(sc)


def _peer_body(xn_ref, h_ref, th_ref, e1_ref, e2_ref, u_ref, vt_ref, out_ref, acc_s, hid_s, w_s):
    j = pl.program_id(1)
    tt = xn_ref.shape[0]

    @pl.when(j == 0)
    def _():
        acc_s[...] = jnp.zeros_like(acc_s)

    hid_s[...] = _dot_nt(u_ref[...], xn_ref[...])

    i1_base = pl.multiple_of(j * PEER_I1_BLOCK, PEER_I1_BLOCK)
    for il in range(PEER_I1_BLOCK):
        es = slice(il * N_KEYS, (il + 1) * N_KEYS)
        for lt in range(tt // LANES):
            ln = slice(lt * LANES, (lt + 1) * LANES)
            g = jnp.zeros((N_KEYS, LANES), F32)
            for h in range(PEER_HEADS):
                hs = slice(h * N_KEYS, (h + 1) * N_KEYS)
                th = th_ref[pl.ds(h * N_KEYS + i1_base, PEER_I1_BLOCK), ln][il:il + 1, :]
                e1 = e1_ref[pl.ds(h * N_KEYS + i1_base, PEER_I1_BLOCK), ln][il:il + 1, :]
                e2 = e2_ref[hs, ln]
                g = g + jnp.where(e2 >= th, e2, 0.0) * e1
            w_s[es, ln] = (g * _gelu_tanh(hid_s[es, ln])).astype(BF16)
    acc_s[...] += _dot(vt_ref[...], w_s[...])

    @pl.when(j == pl.num_programs(1) - 1)
    def _():
        out_ref[...] = h_ref[...] + acc_s[...].T


def _peer(xn2, hn, th, e1, e2, u, vt, *, rows):
    tt = PEER_TOKENS
    eb = PEER_I1_BLOCK * N_KEYS
    half = PEER_HEADS * N_KEYS
    tok = pl.BlockSpec((tt, D_MODEL), lambda i, j: (i, 0))
    tab = pl.BlockSpec((half, tt), lambda i, j: (0, i))
    return pl.pallas_call(
        _peer_body,
        grid=(rows // tt, N_EXPERTS // eb),
        in_specs=[tok, tok, tab, tab, tab,
                  pl.BlockSpec((eb, D_MODEL), lambda i, j: (j, 0)),
                  pl.BlockSpec((D_MODEL, eb), lambda i, j: (0, j))],
        out_specs=pl.BlockSpec((tt, D_MODEL), lambda i, j: (i, 0)),
        out_shape=jax.ShapeDtypeStruct((rows, D_MODEL), F32),
        scratch_shapes=[pltpu.VMEM((D_MODEL, tt), F32), pltpu.VMEM((eb, tt), F32), pltpu.VMEM((eb, tt), BF16)],
        compiler_params=_cparams("parallel", "arbitrary"),
        name="peer_dense",
    )(xn2, hn, th, e1, e2, u, vt)


def _block_diag(w):
    eye = jnp.eye(RG_HEADS, dtype=w.dtype)
    return jnp.einsum("hij,hg->higj", w, eye).reshape(RG_WIDTH, RG_WIDTH)


def _head_pad(w, lo, hi):
    k = w.shape[0]
    part = w[:, :, lo:hi]
    return jnp.pad(part, ((0, 0), (0, 0), (0, HEAD_PAD - (hi - lo)))).reshape(k, N_HEADS * HEAD_PAD)


def _rope_partner(x1, x2):
    z64 = jnp.zeros(x1.shape[:-1] + (QK_NOPE,), x1.dtype)
    z32 = jnp.zeros(x1.shape[:-1] + (HEAD_PAD - QK_DIM,), x1.dtype)
    return jnp.concatenate([z64, x2, x1, z32], axis=-1)


def _layer_weights(i, norm1_g, w_in, conv_w, conv_b, rg_wa, rg_ba, rg_wi, rg_bi, rg_lambda, q_norm_g, w_uq,
                   kv_norm_g, w_ukv, q_head_g, k_head_g, out_g_rg, out_g_attn, w_out, norm2_g, peer_wq,
                   peer_subkeys, peer_u, peer_v):
    half = QK_ROPE // 2
    wq3 = w_uq[i].reshape(Q_LORA, N_HEADS, QK_DIM)
    wkv3 = w_ukv[i].reshape(KV_LORA, N_HEADS, QK_NOPE + V_DIM)
    eye = jnp.eye(QK_ROPE, dtype=F32)
    ek = jnp.concatenate([jnp.zeros((QK_ROPE, QK_NOPE), F32), eye,
                          jnp.zeros((QK_ROPE, HEAD_PAD - QK_DIM), F32)], axis=-1)
    eks = _rope_partner(eye[:, :half], eye[:, half:])
    place = lambda e: jnp.pad(jnp.tile(e, (1, N_HEADS)), ((0, LANES - QK_ROPE), (0, 0))).astype(BF16)

    def head_gain(g):
        main = jnp.pad(g, (0, HEAD_PAD - QK_DIM)).reshape(1, HEAD_PAD)
        swap = _rope_partner(g[QK_NOPE:QK_NOPE + half], g[QK_NOPE + half:]).reshape(1, HEAD_PAD)
        return main, swap

    gq, gqs = head_gain(q_head_g[i])
    gk, gks = head_gain(k_head_g[i])
    wout = w_out[i]
    wout_at = jnp.pad(wout[RG_WIDTH:].reshape(N_HEADS, V_DIM, D_MODEL),
                      ((0, 0), (0, HEAD_PAD - V_DIM), (0, 0))).reshape(N_HEADS * HEAD_PAD, D_MODEL)
    gat = jnp.pad(out_g_attn[i].reshape(N_HEADS, V_DIM), ((0, 0), (0, HEAD_PAD - V_DIM))).reshape(1, -1)
    return dict(
        g1=norm1_g[i].reshape(1, -1),
        win=jnp.pad(w_in[i], ((0, 0), (0, P_IN_PAD - P_IN))).astype(BF16),
        qg=q_norm_g[i].reshape(1, -1),
        wq=_head_pad(wq3, 0, QK_DIM).astype(BF16),
        wqs=_rope_partner(wq3[:, :, QK_NOPE:QK_NOPE + half], wq3[:, :, QK_NOPE + half:])
        .reshape(Q_LORA, -1).astype(BF16),
        kvg=kv_norm_g[i].reshape(1, -1),
        wk=_head_pad(wkv3, 0, QK_NOPE).astype(BF16),
        wv=_head_pad(wkv3, QK_NOPE, QK_NOPE + V_DIM).astype(BF16),
        ek=place(ek), eks=place(eks), gq=gq, gqs=gqs, gk=gk, gks=gks,
        conv_w=conv_w[i], conv_b=conv_b[i].reshape(1, -1),
        wgate_f=jnp.concatenate([_block_diag(rg_wa[i, 0]), _block_diag(rg_wi[i, 0])], axis=1).astype(BF16),
        wgate_b=jnp.concatenate([_block_diag(rg_wa[i, 1]), _block_diag(rg_wi[i, 1])], axis=1).astype(BF16),
        gate_bias=jnp.stack([rg_ba[i, 0].reshape(-1), rg_bi[i, 0].reshape(-1),
                             rg_ba[i, 1].reshape(-1), rg_bi[i, 1].reshape(-1)]),
        lam=rg_lambda[i],
        grg=out_g_rg[i].reshape(1, -1), gat=gat,
        wout_rg=wout[:RG_WIDTH].astype(BF16), wout_at=wout_at.astype(BF16),
        g2=norm2_g[i].reshape(1, -1),
        wqt=peer_wq[i].T.astype(BF16),
        subk=peer_subkeys[i].reshape(2 * PEER_HEADS, N_KEYS, PEER_HALF).astype(BF16),
        u=peer_u[i].astype(BF16),
        vt=peer_v[i].T.astype(BF16),
    )


def _rope_tables(length):
    pos = jnp.arange(length, dtype=F32)
    inv = ROPE_THETA ** (-jnp.arange(0, QK_ROPE, 2, dtype=F32) / QK_ROPE)
    ang = pos[:, None] * inv[None, :]
    cos, sin = jnp.cos(ang), jnp.sin(ang)
    ones = jnp.ones((length, QK_NOPE), F32)
    zpad = jnp.zeros((length, HEAD_PAD - QK_DIM), F32)
    cos_t = jnp.concatenate([ones, cos, cos, zpad], axis=-1)
    sin_t = jnp.concatenate([jnp.zeros((length, QK_NOPE), F32), -sin, sin, zpad], axis=-1)
    return cos_t, sin_t


def _padded_len(length):
    tile = math.lcm(ROW_TILE, SCAN_CHUNK, ATTN_Q_TILE, PEER_TOKENS)
    return -(-length // tile) * tile


def _encode(x, meta_tokens, layers):
    batch, n_tok, _ = x.shape
    seq_len = n_tok + N_META
    lp = _padded_len(seq_len)
    rows = batch * lp
    meta = jnp.broadcast_to(meta_tokens[None].astype(x.dtype), (batch, N_META, D_MODEL))
    h = jnp.concatenate([meta, x, jnp.zeros((batch, lp - seq_len, D_MODEL), x.dtype)], axis=1)
    h = h.reshape(rows, D_MODEL)
    cos_t, sin_t = _rope_tables(lp)
    wide = N_HEADS * HEAD_PAD
    for w in layers:
        xr, gate, q, k, v = _inproj(h, cos_t, sin_t, w, rows=rows, pos_tiles=lp // ROW_TILE)
        hf, hb = _scan(xr, w, batch=batch, padded_len=lp, seq_len=seq_len)
        kt = k.reshape(batch, lp, N_HEADS, HEAD_PAD).transpose(0, 2, 3, 1)
        o = _attention(q.reshape(batch, lp, wide), kt, v.reshape(batch, lp, wide),
                       batch=batch, padded_len=lp, seq_len=seq_len)
        hn, xn2, sc = _outproj(hf, hb, gate, o.reshape(rows, wide), h, w, rows=rows)
        th, e1, e2 = _route(sc, rows=rows)
        h = _peer(xn2, hn, th, e1, e2, w["u"], w["vt"], rows=rows)
    return h.reshape(batch, lp, D_MODEL)[:, N_META:seq_len]


def kernel(x_prompt, x_sample, meta_tokens, norm1_g, w_in, conv_w, conv_b, rg_wa, rg_ba, rg_wi, rg_bi, rg_lambda, q_norm_g, w_uq, kv_norm_g, w_ukv, q_head_g, k_head_g, out_g_rg, out_g_attn, w_out, norm2_g, peer_wq, peer_subkeys, peer_u, peer_v):
    params = (norm1_g, w_in, conv_w, conv_b, rg_wa, rg_ba, rg_wi, rg_bi, rg_lambda, q_norm_g, w_uq, kv_norm_g,
              w_ukv, q_head_g, k_head_g, out_g_rg, out_g_attn, w_out, norm2_g, peer_wq, peer_subkeys, peer_u,
              peer_v)
    layers = [_layer_weights(i, *params) for i in range(norm1_g.shape[0])]
    y_prompt = _encode(x_prompt, meta_tokens, layers)
    y_sample = _encode(x_sample, meta_tokens, layers)
    return (y_prompt, y_sample)
```

```python
import functools
import math

import jax
import jax.numpy as jnp
from jax import lax
from jax.experimental import pallas as pl
from jax.experimental.pallas import tpu as pltpu

F32 = jnp.float32
BF16 = jnp.bfloat16

D_MODEL = 1024
N_META = 16
EPS = 1e-6
RG_WIDTH = 512
RG_HEADS = 8
RG_BLOCK = 64
CONV_W = 4
RG_C = 8.0
N_HEADS = 8
QK_NOPE = 64
QK_ROPE = 32
QK_DIM = 96
V_DIM = 64
Q_LORA = 256
KV_LORA = 128
ROPE_THETA = 10000.0
P_IN = 2 * RG_WIDTH + Q_LORA + KV_LORA + QK_ROPE
PEER_HEADS = 8
N_KEYS = 128
N_EXPERTS = N_KEYS * N_KEYS
PEER_TOPK = 16
PEER_HALF = 128

LANES = 128
SUBLANES = 8
HEAD_PAD = LANES
P_IN_PAD = 12 * LANES
ROW_TILE = 512
SCAN_CHUNK = 256
ATTN_Q_TILE = 512
KV_CHUNK = 1024
PEER_TOKENS = 512
ROUTE_TOKENS = 256
PEER_I1_BLOCK = 16
VMEM_LIMIT = 52 * 1024 * 1024
NEG_INF = float("-inf")
MASK_VALUE = -1e30
LOG2E = 1.4426950408889634


def _cparams(*sem):
    return pltpu.CompilerParams(dimension_semantics=sem, vmem_limit_bytes=VMEM_LIMIT)


def _rms(x, g):
    return x * lax.rsqrt(jnp.mean(x * x, axis=-1, keepdims=True) + EPS) * g


def _gelu_tanh(x):
    return x * (0.5 * (1.0 + jnp.tanh(0.7978845608028654 * (x + 0.044715 * (x * x * x)))))


def _sigmoid(x):
    return 1.0 / (1.0 + jnp.exp(-x))


def _neg_expm1(y):
    u = jnp.exp(y)
    um1 = u - 1.0
    tiny = um1 == 0.0
    r = jnp.where(tiny, y, um1 * y / jnp.where(tiny, 1.0, jnp.log(u)))
    return -jnp.where(um1 == -1.0, -1.0, r)


def _dot(a, b):
    return jnp.dot(a, b, preferred_element_type=F32)


def _dot_nt(a, b):
    return lax.dot_general(a, b, (((1,), (1,)), ((), ())), preferred_element_type=F32)


def _full(shape):
    n = len(shape)
    return pl.BlockSpec(shape, lambda *_: (0,) * n)


def _inproj_body(x_ref, cos_ref, sin_ref, g1_ref, win_ref, qg_ref, wq_ref, wqs_ref, kvg_ref,
                 wk_ref, ek_ref, eks_ref, wv_ref, gq_ref, gqs_ref, gk_ref, gks_ref,
                 xr_ref, gate_ref, q_ref, k_ref, v_ref):
    xn = _rms(x_ref[...], g1_ref[...])
    p = _dot(xn.astype(BF16), win_ref[...])
    xr_ref[...] = p[:, :RG_WIDTH]
    gate_ref[...] = p[:, RG_WIDTH:2 * RG_WIDTH]
    o2 = 2 * RG_WIDTH
    qcn = _rms(p[:, o2:o2 + Q_LORA], qg_ref[...]).astype(BF16)
    kvn = _rms(p[:, o2 + Q_LORA:o2 + Q_LORA + KV_LORA], kvg_ref[...]).astype(BF16)
    kpe = p[:, o2 + Q_LORA + KV_LORA:].astype(BF16)
    q = _dot(qcn, wq_ref[...])
    qs = _dot(qcn, wqs_ref[...])
    k = _dot(kvn, wk_ref[...]) + _dot(kpe, ek_ref[...])
    ks = _dot(kpe, eks_ref[...])
    v = _dot(kvn, wv_ref[...]).astype(BF16)
    c = cos_ref[...]
    s = sin_ref[...]
    qscale = QK_DIM ** -0.5 * LOG2E
    for h in range(N_HEADS):
        sl = slice(h * HEAD_PAD, (h + 1) * HEAD_PAD)
        qh = q[:, sl]
        rq = lax.rsqrt(jnp.sum(qh * qh, axis=-1, keepdims=True) * (1.0 / QK_DIM) + EPS)
        q_ref[h] = ((qh * gq_ref[...] * c + qs[:, sl] * gqs_ref[...] * s) * (rq * qscale)).astype(BF16)
        kh = k[:, sl]
        rk = lax.rsqrt(jnp.sum(kh * kh, axis=-1, keepdims=True) * (1.0 / QK_DIM) + EPS)
        k_ref[h] = ((kh * gk_ref[...] * c + ks[:, sl] * gks_ref[...] * s) * rk).astype(BF16)
        v_ref[h] = v[:, sl]


def _inproj(h, cos_t, sin_t, w, *, rows, pos_tiles):
    tm = ROW_TILE
    row = lambda n: pl.BlockSpec((tm, n), lambda i: (i, 0))
    pos = pl.BlockSpec((tm, HEAD_PAD), lambda i: (i % pos_tiles, 0))
    wide = N_HEADS * HEAD_PAD
    heads = pl.BlockSpec((N_HEADS, tm, HEAD_PAD), lambda i: (0, i, 0))
    heads_shape = jax.ShapeDtypeStruct((N_HEADS, rows, HEAD_PAD), BF16)
    return pl.pallas_call(
        _inproj_body,
        grid=(rows // tm,),
        in_specs=[row(D_MODEL), pos, pos, _full((1, D_MODEL)), _full((D_MODEL, P_IN_PAD)),
                  _full((1, Q_LORA)), _full((Q_LORA, wide)), _full((Q_LORA, wide)),
                  _full((1, KV_LORA)), _full((KV_LORA, wide)), _full((LANES, wide)), _full((LANES, wide)),
                  _full((KV_LORA, wide)),
                  _full((1, HEAD_PAD)), _full((1, HEAD_PAD)), _full((1, HEAD_PAD)), _full((1, HEAD_PAD))],
        out_specs=[row(RG_WIDTH), row(RG_WIDTH), heads, heads, heads],
        out_shape=[jax.ShapeDtypeStruct((rows, RG_WIDTH), F32), jax.ShapeDtypeStruct((rows, RG_WIDTH), F32),
                   heads_shape, heads_shape, heads_shape],
        compiler_params=_cparams("parallel"),
        name="inproj",
    )(h, cos_t, sin_t, w["g1"], w["win"], w["qg"], w["wq"], w["wqs"], w["kvg"], w["wk"], w["ek"], w["eks"],
      w["wv"], w["gq"], w["gqs"], w["gk"], w["gks"])


def _scan_body(xm_f, xp_f, xn_f, xm_b, xp_b, xn_b, cw_ref, cb_ref, wf_ref, wb_ref, bias_ref, lam_ref,
               hf_ref, hb_ref, ext_s, a_s, b_s, cf_s, cb_s, *, seq_len, chunk, n_chunks):
    i = pl.program_id(1)
    C = chunk
    groups = C // SUBLANES

    @pl.when(i == 0)
    def _():
        cf_s[...] = jnp.zeros_like(cf_s)
        cb_s[...] = jnp.zeros_like(cb_s)

    rows8 = lax.broadcasted_iota(jnp.int32, (SUBLANES, RG_WIDTH), 0)
    rows_c = lax.broadcasted_iota(jnp.int32, (C, RG_WIDTH), 0)
    rmod = rows_c & (SUBLANES - 1)

    def masked(x, pos):
        return jnp.where((pos >= 0) & (pos < seq_len), x, 0.0)

    def affine_terms(xm, xp, xn, j, w_ref, d):
        base = j * C
        ext_s[0:SUBLANES, :] = masked(xp[...], base - SUBLANES + rows8)
        ext_s[SUBLANES:SUBLANES + C, :] = masked(xm[...], base + rows_c)
        ext_s[SUBLANES + C:, :] = masked(xn[...], base + C + rows8)
        xc = cb_ref[...]
        for t in range(CONV_W):
            xc = xc + cw_ref[t:t + 1, :] * ext_s[SUBLANES - 2 + t:SUBLANES - 2 + t + C, :]
        gm = _dot(xc.astype(BF16), w_ref[...])
        r = _sigmoid(gm[:, :RG_WIDTH] + bias_ref[2 * d:2 * d + 1, :])
        ig = _sigmoid(gm[:, RG_WIDTH:] + bias_ref[2 * d + 1:2 * d + 2, :])
        z = -lam_ref[d:d + 1, :]
        softplus = jnp.maximum(z, 0.0) + jnp.log1p(jnp.exp(-jnp.abs(z)))
        log_a = (-RG_C) * r * softplus
        a = jnp.exp(log_a)
        b = jnp.sqrt(_neg_expm1(2.0 * log_a)) * (ig * xc)
        b = jnp.where(base + rows_c < seq_len, b, 0.0)
        return a, b

    def rotate_in_groups(x, shift):
        return pltpu.roll(x.reshape(groups, SUBLANES, RG_WIDTH), shift, 1).reshape(C, RG_WIDTH)

    a, b = affine_terms(xm_f, xp_f, xn_f, i, wf_ref, 0)
    for d in (1, 2, 4):
        m = rmod >= d
        a_sh = rotate_in_groups(a, d)
        b_sh = rotate_in_groups(b, d)
        b = jnp.where(m, a * b_sh + b, b)
        a = jnp.where(m, a * a_sh, a)
    a_s[...] = a
    b_s[...] = b

    def fwd_group(g, carry):
        r0 = pl.multiple_of(g * SUBLANES, SUBLANES)
        hrow = a_s[pl.ds(r0, SUBLANES), :] * carry + b_s[pl.ds(r0, SUBLANES), :]
        hf_ref[pl.ds(r0, SUBLANES), :] = hrow
        return jnp.broadcast_to(hrow[SUBLANES - 1:SUBLANES, :], (SUBLANES, RG_WIDTH))

    cf_s[...] = lax.fori_loop(0, groups, fwd_group, cf_s[...])

    a, b = affine_terms(xm_b, xp_b, xn_b, n_chunks - 1 - i, wb_ref, 1)
    for d in (1, 2, 4):
        m = rmod < SUBLANES - d
        a_sh = rotate_in_groups(a, SUBLANES - d)
        b_sh = rotate_in_groups(b, SUBLANES - d)
        b = jnp.where(m, a * b_sh + b, b)
        a = jnp.where(m, a * a_sh, a)
    a_s[...] = a
    b_s[...] = b

    def bwd_group(g, carry):
        r0 = pl.multiple_of((groups - 1 - g) * SUBLANES, SUBLANES)
        hrow = a_s[pl.ds(r0, SUBLANES), :] * carry + b_s[pl.ds(r0, SUBLANES), :]
        hb_ref[pl.ds(r0, SUBLANES), :] = hrow
        return jnp.broadcast_to(hrow[0:1, :], (SUBLANES, RG_WIDTH))

    cb_s[...] = lax.fori_loop(0, groups, bwd_group, cb_s[...])


def _scan(xr, w, *, batch, padded_len, seq_len):
    C = SCAN_CHUNK
    n = padded_len // C
    per8 = C // SUBLANES
    last8 = batch * padded_len // SUBLANES - 1

    def chunk_f(b, i):
        return b * n + i

    def chunk_b(b, i):
        return b * n + (n - 1 - i)

    def specs(chunk_of):
        main = pl.BlockSpec((C, RG_WIDTH), lambda b, i: (chunk_of(b, i), 0))
        prev = pl.BlockSpec((SUBLANES, RG_WIDTH), lambda b, i: (jnp.maximum(chunk_of(b, i) * per8 - 1, 0), 0))
        nxt = pl.BlockSpec((SUBLANES, RG_WIDTH),
                           lambda b, i: (jnp.minimum((chunk_of(b, i) + 1) * per8, last8), 0))
        return [main, prev, nxt]

    rows = batch * padded_len
    return pl.pallas_call(
        functools.partial(_scan_body, seq_len=seq_len, chunk=C, n_chunks=n),
        grid=(batch, n),
        in_specs=specs(chunk_f) + specs(chunk_b) + [
            _full((CONV_W, RG_WIDTH)), _full((1, RG_WIDTH)),
            _full((RG_WIDTH, 2 * RG_WIDTH)), _full((RG_WIDTH, 2 * RG_WIDTH)),
            _full((4, RG_WIDTH)), _full((2, RG_WIDTH))],
        out_specs=[pl.BlockSpec((C, RG_WIDTH), lambda b, i: (chunk_f(b, i), 0)),
                   pl.BlockSpec((C, RG_WIDTH), lambda b, i: (chunk_b(b, i), 0))],
        out_shape=[jax.ShapeDtypeStruct((rows, RG_WIDTH), F32), jax.ShapeDtypeStruct((rows, RG_WIDTH), F32)],
        scratch_shapes=[pltpu.VMEM((C + 2 * SUBLANES, RG_WIDTH), F32), pltpu.VMEM((C, RG_WIDTH), F32),
                        pltpu.VMEM((C, RG_WIDTH), F32), pltpu.VMEM((SUBLANES, RG_WIDTH), F32),
                        pltpu.VMEM((SUBLANES, RG_WIDTH), F32)],
        compiler_params=_cparams("arbitrary", "arbitrary"),
        name="rglru_scan",
    )(xr, xr, xr, xr, xr, xr, w["conv_w"], w["conv_b"], w["wgate_f"], w["wgate_b"], w["gate_bias"], w["lam"])


def _attn_body(q_ref, kt_ref, v_ref, o_ref, s_a, s_b, m_s, l_s, acc_s, *, seq_len, n_full, rem):
    tq = q_ref.shape[2]
    m_s[...] = jnp.full_like(m_s, NEG_INF)
    l_s[...] = jnp.zeros_like(l_s)
    acc_s[...] = jnp.zeros_like(acc_s)

    def scores(dst, start, width):
        dst[:, :width] = _dot(q_ref[0, 0], kt_ref[0, 0, :, pl.ds(start, width)])

    def update(src, start, width, mask):
        s = src[:, :width]
        if mask:
            kpos = start + lax.broadcasted_iota(jnp.int32, (tq, width), 1)
            s = jnp.where(kpos < seq_len, s, MASK_VALUE)
        m = m_s[...]
        m_new = jnp.maximum(m, jnp.max(s, axis=-1, keepdims=True))
        alpha = jnp.exp2(m - m_new)
        p = jnp.exp2(s - m_new)
        m_s[...] = m_new
        l_s[...] = alpha * l_s[...] + jnp.sum(p, axis=-1, keepdims=True)
        acc_s[...] = alpha * acc_s[...] + _dot(p.astype(BF16), v_ref[0, 0, pl.ds(start, width), :])

    if rem:
        scores(s_b, n_full * KV_CHUNK, rem)
    if n_full:
        scores(s_a, 0, KV_CHUNK)
    if rem:
        update(s_b, n_full * KV_CHUNK, rem, True)
    pairs = max((n_full - 1) // 2, 0)
    tail = [(c * KV_CHUNK, KV_CHUNK, False) for c in range(2 * pairs, n_full)]

    def pair(i, carry):
        c0 = pl.multiple_of(2 * i * KV_CHUNK, KV_CHUNK)
        scores(s_b, c0 + KV_CHUNK, KV_CHUNK)
        update(s_a, c0, KV_CHUNK, False)
        scores(s_a, c0 + 2 * KV_CHUNK, KV_CHUNK)
        update(s_b, c0 + KV_CHUNK, KV_CHUNK, False)
        return carry

    lax.fori_loop(0, pairs, pair, 0)
    bufs = (s_a, s_b)
    for n, (start, width, mask) in enumerate(tail):
        if n + 1 < len(tail):
            scores(bufs[(n + 1) % 2], tail[n + 1][0], tail[n + 1][1])
        update(bufs[n % 2], start, width, mask)
    o_ref[0, 0] = acc_s[...] / l_s[...]


def _attention(q, kt, v, *, batch, padded_len, seq_len):
    tq = ATTN_Q_TILE
    n_full = seq_len // KV_CHUNK
    rem = -(-(seq_len - n_full * KV_CHUNK) // LANES) * LANES
    q_tile = pl.BlockSpec((1, 1, tq, HEAD_PAD), lambda b, h, i: (h, b, i, 0))
    return pl.pallas_call(
        functools.partial(_attn_body, seq_len=seq_len, n_full=n_full, rem=rem),
        grid=(batch, N_HEADS, padded_len // tq),
        in_specs=[q_tile,
                  pl.BlockSpec((1, 1, HEAD_PAD, padded_len), lambda b, h, i: (h, b, 0, 0)),
                  pl.BlockSpec((1, 1, padded_len, HEAD_PAD), lambda b, h, i: (h, b, 0, 0))],
        out_specs=q_tile,
        out_shape=jax.ShapeDtypeStruct((N_HEADS, batch, padded_len, HEAD_PAD), F32),
        scratch_shapes=[pltpu.VMEM((tq, KV_CHUNK), F32), pltpu.VMEM((tq, KV_CHUNK), F32),
                        pltpu.VMEM((tq, 1), F32), pltpu.VMEM((tq, 1), F32), pltpu.VMEM((tq, HEAD_PAD), F32)],
        compiler_params=_cparams("parallel", "parallel", "arbitrary"),
        name="attention",
    )(q, kt, v)


def _outproj_body(hf_ref, hb_ref, gate_ref, o_ref, h_ref, grg_ref, gat_ref, wrg_ref, wat_ref, g2_ref,
                  wqt_ref, sk_ref, hn_ref, xn2_ref, sc_ref):
    rg = (hf_ref[...] + hb_ref[...]) * _gelu_tanh(gate_ref[...])
    rgn = _rms(rg, grg_ref[...])
    o = jnp.concatenate([o_ref[hd] for hd in range(N_HEADS)], axis=1)
    ms = jnp.sum(o * o, axis=-1, keepdims=True) * (1.0 / (N_HEADS * V_DIM))
    attn = o * lax.rsqrt(ms + EPS) * gat_ref[...]
    hn = h_ref[...] + _dot(rgn.astype(BF16), wrg_ref[...]) + _dot(attn.astype(BF16), wat_ref[...])
    hn_ref[...] = hn
    xn2 = _rms(hn, g2_ref[...]).astype(BF16)
    xn2_ref[...] = xn2
    qpt = _dot_nt(wqt_ref[...], xn2)
    for h in range(PEER_HEADS):
        for s in range(2):
            r0 = (h * 2 + s) * PEER_HALF
            blk = qpt[r0:r0 + PEER_HALF, :].astype(BF16)
            o0 = s * PEER_HEADS * N_KEYS + h * N_KEYS
            sc_ref[o0:o0 + N_KEYS, :] = _dot(sk_ref[h * 2 + s], blk)


def _outproj(hf, hb, gate, o, h, w, *, rows):
    tm = ROW_TILE
    row = lambda n: pl.BlockSpec((tm, n), lambda i: (i, 0))
    wide = N_HEADS * HEAD_PAD
    nsc = 2 * PEER_HEADS * N_KEYS
    return pl.pallas_call(
        _outproj_body,
        grid=(rows // tm,),
        in_specs=[row(RG_WIDTH), row(RG_WIDTH), row(RG_WIDTH),
                  pl.BlockSpec((N_HEADS, tm, HEAD_PAD), lambda i: (0, i, 0)), row(D_MODEL),
                  _full((1, RG_WIDTH)), _full((1, wide)), _full((RG_WIDTH, D_MODEL)), _full((wide, D_MODEL)),
                  _full((1, D_MODEL)), _full((nsc, D_MODEL)), _full((2 * PEER_HEADS, N_KEYS, PEER_HALF))],
        out_specs=[row(D_MODEL), row(D_MODEL), pl.BlockSpec((nsc, tm), lambda i: (0, i))],
        out_shape=[jax.ShapeDtypeStruct((rows, D_MODEL), F32), jax.ShapeDtypeStruct((rows, D_MODEL), BF16),
                   jax.ShapeDtypeStruct((nsc, rows), F32)],
        compiler_params=_cparams("parallel"),
        name="outproj",
    )(hf, hb, gate, o, h, w["grg"], w["gat"], w["wout_rg"], w["wout_at"], w["g2"], w["wqt"], w["subk"])


def _sort_network(n):
    pairs, p = [], 1
    while p < n:
        k = p
        while k >= 1:
            for j in range(k % p, n - k, 2 * k):
                for i in range(min(k, n - j - k)):
                    if (i + j) // (2 * p) == (i + j + k) // (2 * p):
                        pairs.append((i + j, i + j + k))
            k //= 2
        p *= 2
    return pairs


def _top_rows(x, count):
    n = x.shape[0] // SUBLANES
    cols = [x[k * SUBLANES:(k + 1) * SUBLANES, :] for k in range(n)]
    for a, b in _sort_network(n):
        cols[a], cols[b] = jnp.maximum(cols[a], cols[b]), jnp.minimum(cols[a], cols[b])
    cols.append(jnp.full_like(cols[0], NEG_INF))
    rows = []
    for r in range(count):
        m = jnp.max(cols[0], axis=0, keepdims=True)
        rows.append(m)
        depth = min(n, count - 1 - r)
        hit = cols[0] == m
        for k in range(depth):
            cols[k] = jnp.where(hit, cols[k + 1], cols[k])
    return rows


def _route_body(sc_ref, th_ref, e1_ref, e2_ref):
    t = sc_ref.shape[1]
    half = PEER_HEADS * N_KEYS
    jrow = lax.broadcasted_iota(jnp.int32, (SUBLANES, t), 0)

    def bc(row):
        return jnp.broadcast_to(row, (SUBLANES, t))

    def head(h, carry):
        r0 = pl.multiple_of(h * N_KEYS, N_KEYS)
        s1 = sc_ref[pl.ds(r0, N_KEYS), :]
        s2 = sc_ref[pl.ds(half + r0, N_KEYS), :]
        a = _top_rows(s1, PEER_TOPK + 1)
        b = _top_rows(s2, PEER_TOPK + 1)
        b_lo = jnp.concatenate(b[:SUBLANES], axis=0)
        b_hi = jnp.concatenate(b[SUBLANES:2 * SUBLANES], axis=0)
        a_hi = jnp.concatenate(a[SUBLANES:2 * SUBLANES], axis=0)
        pieces = [bc(a[0]) + b_lo, bc(a[0]) + b_hi, bc(a[1]) + b_lo]
        for i in range(2, SUBLANES):
            pieces.append(jnp.where(jrow < (PEER_TOPK + 1) // (i + 1), bc(a[i]) + b_lo, NEG_INF))
        pieces.append(a_hi + bc(b[0]))
        ends = jnp.concatenate([a[0] + b[PEER_TOPK], a[PEER_TOPK] + b[0]]
                               + [jnp.full((1, t), NEG_INF, F32)] * (SUBLANES - 2), axis=0)
        pieces.append(ends)
        cand = jnp.concatenate(pieces, axis=0)
        x = cand
        for _ in range(PEER_TOPK - 1):
            x = jnp.where(x == jnp.max(x, axis=0, keepdims=True), NEG_INF, x)
        c16 = jnp.max(x, axis=0, keepdims=True)
        c17 = jnp.max(jnp.where(x == c16, NEG_INF, x), axis=0, keepdims=True)
        thr = 0.5 * (c16 + c17)
        top = a[0] + b[0]
        z = jnp.sum(jnp.where(cand >= thr, jnp.exp(cand - top), 0.0), axis=0, keepdims=True)
        th_ref[pl.ds(r0, N_KEYS), :] = jnp.exp((thr - s1) - b[0])
        e1_ref[pl.ds(r0, N_KEYS), :] = jnp.exp(s1 - a[0]) / z
        e2_ref[pl.ds(r0, N_KEYS), :] = jnp.exp(s2 - b[0])
        return carry

    lax.fori_loop(0, PEER_HEADS, head, 0)


def _route(sc, *, rows):
    t = ROUTE_TOKENS
    half = PEER_HEADS * N_KEYS
    out = pl.BlockSpec((half, t), lambda i: (0, i))
    return pl.pallas_call(
        _route_body,
        grid=(rows // t,),
        in_specs=[pl.BlockSpec((2 * half, t), lambda i: (0, i))],
        out_specs=[out, out, out],
        out_shape=[jax.ShapeDtypeStruct((half, rows), F32)] * 3,
        compiler_params=_cparams("parallel"),
        name="peer_route",
    )(sc)


def _peer_body(xn_ref, h_ref, th_ref, e1_ref, e2_ref, u_ref, vt_ref, out_ref, acc_s, hid_s, w_s):
    j = pl.program_id(1)
    tt = xn_ref.shape[0]

    @pl.when(j == 0)
    def _():
        acc_s[...] = jnp.zeros_like(acc_s)

    hid_s[...] = _dot_nt(u_ref[...], xn_ref[...])

    i1_base = pl.multiple_of(j * PEER_I1_BLOCK, PEER_I1_BLOCK)
    for il in range(PEER_I1_BLOCK):
        es = slice(il * N_KEYS, (il + 1) * N_KEYS)
        for lt in range(tt // LANES):
            ln = slice(lt * LANES, (lt + 1) * LANES)
            g = jnp.zeros((N_KEYS, LANES), F32)
            for h in range(PEER_HEADS):
                hs = slice(h * N_KEYS, (h + 1) * N_KEYS)
                th = th_ref[pl.ds(h * N_KEYS + i1_base, PEER_I1_BLOCK), ln][il:il + 1, :]
                e1 = e1_ref[pl.ds(h * N_KEYS + i1_base, PEER_I1_BLOCK), ln][il:il + 1, :]
                e2 = e2_ref[hs, ln]
                g = g + jnp.where(e2 >= th, e2, 0.0) * e1
            w_s[es, ln] = (g * _gelu_tanh(hid_s[es, ln])).astype(BF16)
    acc_s[...] += _dot(vt_ref[...], w_s[...])

    @pl.when(j == pl.num_programs(1) - 1)
    def _():
        out_ref[...] = h_ref[...] + acc_s[...].T


def _peer(xn2, hn, th, e1, e2, u, vt, *, rows):
    tt = PEER_TOKENS
    eb = PEER_I1_BLOCK * N_KEYS
    half = PEER_HEADS * N_KEYS
    tok = pl.BlockSpec((tt, D_MODEL), lambda i, j: (i, 0))
    tab = pl.BlockSpec((half, tt), lambda i, j: (0, i))
    return pl.pallas_call(
        _peer_body,
        grid=(rows // tt, N_EXPERTS // eb),
        in_specs=[tok, tok, tab, tab, tab,
                  pl.BlockSpec((eb, D_MODEL), lambda i, j: (j, 0)),
                  pl.BlockSpec((D_MODEL, eb), lambda i, j: (0, j))],
        out_specs=pl.BlockSpec((tt, D_MODEL), lambda i, j: (i, 0)),
        out_shape=jax.ShapeDtypeStruct((rows, D_MODEL), F32),
        scratch_shapes=[pltpu.VMEM((D_MODEL, tt), F32), pltpu.VMEM((eb, tt), F32), pltpu.VMEM((eb, tt), BF16)],
        compiler_params=_cparams("parallel", "arbitrary"),
        name="peer_dense",
    )(xn2, hn, th, e1, e2, u, vt)


def _block_diag(w):
    eye = jnp.eye(RG_HEADS, dtype=w.dtype)
    return jnp.einsum("hij,hg->higj", w, eye).reshape(RG_WIDTH, RG_WIDTH)


def _head_pad(w, lo, hi):
    k = w.shape[0]
    part = w[:, :, lo:hi]
    return jnp.pad(part, ((0, 0), (0, 0), (0, HEAD_PAD - (hi - lo)))).reshape(k, N_HEADS * HEAD_PAD)


def _rope_partner(x1, x2):
    z64 = jnp.zeros(x1.shape[:-1] + (QK_NOPE,), x1.dtype)
    z32 = jnp.zeros(x1.shape[:-1] + (HEAD_PAD - QK_DIM,), x1.dtype)
    return jnp.concatenate([z64, x2, x1, z32], axis=-1)


def _layer_weights(i, norm1_g, w_in, conv_w, conv_b, rg_wa, rg_ba, rg_wi, rg_bi, rg_lambda, q_norm_g, w_uq,
                   kv_norm_g, w_ukv, q_head_g, k_head_g, out_g_rg, out_g_attn, w_out, norm2_g, peer_wq,
                   peer_subkeys, peer_u, peer_v):
    half = QK_ROPE // 2
    wq3 = w_uq[i].reshape(Q_LORA, N_HEADS, QK_DIM)
    wkv3 = w_ukv[i].reshape(KV_LORA, N_HEADS, QK_NOPE + V_DIM)
    eye = jnp.eye(QK_ROPE, dtype=F32)
    ek = jnp.concatenate([jnp.zeros((QK_ROPE, QK_NOPE), F32), eye,
                          jnp.zeros((QK_ROPE, HEAD_PAD - QK_DIM), F32)], axis=-1)
    eks = _rope_partner(eye[:, :half], eye[:, half:])
    place = lambda e: jnp.pad(jnp.tile(e, (1, N_HEADS)), ((0, LANES - QK_ROPE), (0, 0))).astype(BF16)

    def head_gain(g):
        main = jnp.pad(g, (0, HEAD_PAD - QK_DIM)).reshape(1, HEAD_PAD)
        swap = _rope_partner(g[QK_NOPE:QK_NOPE + half], g[QK_NOPE + half:]).reshape(1, HEAD_PAD)
        return main, swap

    gq, gqs = head_gain(q_head_g[i])
    gk, gks = head_gain(k_head_g[i])
    wout = w_out[i]
    wout_at = jnp.pad(wout[RG_WIDTH:].reshape(N_HEADS, V_DIM, D_MODEL),
                      ((0, 0), (0, HEAD_PAD - V_DIM), (0, 0))).reshape(N_HEADS * HEAD_PAD, D_MODEL)
    gat = jnp.pad(out_g_attn[i].reshape(N_HEADS, V_DIM), ((0, 0), (0, HEAD_PAD - V_DIM))).reshape(1, -1)
    return dict(
        g1=norm1_g[i].reshape(1, -1),
        win=jnp.pad(w_in[i], ((0, 0), (0, P_IN_PAD - P_IN))).astype(BF16),
        qg=q_norm_g[i].reshape(1, -1),
        wq=_head_pad(wq3, 0, QK_DIM).astype(BF16),
        wqs=_rope_partner(wq3[:, :, QK_NOPE:QK_NOPE + half], wq3[:, :, QK_NOPE + half:])
        .reshape(Q_LORA, -1).astype(BF16),
        kvg=kv_norm_g[i].reshape(1, -1),
        wk=_head_pad(wkv3, 0, QK_NOPE).astype(BF16),
        wv=_head_pad(wkv3, QK_NOPE, QK_NOPE + V_DIM).astype(BF16),
        ek=place(ek), eks=place(eks), gq=gq, gqs=gqs, gk=gk, gks=gks,
        conv_w=conv_w[i], conv_b=conv_b[i].reshape(1, -1),
        wgate_f=jnp.concatenate([_block_diag(rg_wa[i, 0]), _block_diag(rg_wi[i, 0])], axis=1).astype(BF16),
        wgate_b=jnp.concatenate([_block_diag(rg_wa[i, 1]), _block_diag(rg_wi[i, 1])], axis=1).astype(BF16),
        gate_bias=jnp.stack([rg_ba[i, 0].reshape(-1), rg_bi[i, 0].reshape(-1),
                             rg_ba[i, 1].reshape(-1), rg_bi[i, 1].reshape(-1)]),
        lam=rg_lambda[i],
        grg=out_g_rg[i].reshape(1, -1), gat=gat,
        wout_rg=wout[:RG_WIDTH].astype(BF16), wout_at=wout_at.astype(BF16),
        g2=norm2_g[i].reshape(1, -1),
        wqt=peer_wq[i].T.astype(BF16),
        subk=peer_subkeys[i].reshape(2 * PEER_HEADS, N_KEYS, PEER_HALF).astype(BF16),
        u=peer_u[i].astype(BF16),
        vt=peer_v[i].T.astype(BF16),
    )


def _rope_tables(length):
    pos = jnp.arange(length, dtype=F32)
    inv = ROPE_THETA ** (-jnp.arange(0, QK_ROPE, 2, dtype=F32) / QK_ROPE)
    ang = pos[:, None] * inv[None, :]
    cos, sin = jnp.cos(ang), jnp.sin(ang)
    ones = jnp.ones((length, QK_NOPE), F32)
    zpad = jnp.zeros((length, HEAD_PAD - QK_DIM), F32)
    cos_t = jnp.concatenate([ones, cos, cos, zpad], axis=-1)
    sin_t = jnp.concatenate([jnp.zeros((length, QK_NOPE), F32), -sin, sin, zpad], axis=-1)
    return cos_t, sin_t


def _padded_len(length):
    tile = math.lcm(ROW_TILE, SCAN_CHUNK, ATTN_Q_TILE, PEER_TOKENS)
    return -(-length // tile) * tile


def _encode(x, meta_tokens, layers):
    batch, n_tok, _ = x.shape
    seq_len = n_tok + N_META
    lp = _padded_len(seq_len)
    rows = batch * lp
    meta = jnp.broadcast_to(meta_tokens[None].astype(x.dtype), (batch, N_META, D_MODEL))
    h = jnp.concatenate([meta, x, jnp.zeros((batch, lp - seq_len, D_MODEL), x.dtype)], axis=1)
    h = h.reshape(rows, D_MODEL)
    cos_t, sin_t = _rope_tables(lp)
    for w in layers:
        xr, gate, q, k, v = _inproj(h, cos_t, sin_t, w, rows=rows, pos_tiles=lp // ROW_TILE)
        hf, hb = _scan(xr, w, batch=batch, padded_len=lp, seq_len=seq_len)
        per_seq = (N_HEADS, batch, lp, HEAD_PAD)
        kt = k.reshape(per_seq).transpose(0, 1, 3, 2)
        o = _attention(q.reshape(per_seq), kt, v.reshape(per_seq), batch=batch, padded_len=lp, seq_len=seq_len)
        hn, xn2, sc = _outproj(hf, hb, gate, o.reshape(N_HEADS, rows, HEAD_PAD), h, w, rows=rows)
        th, e1, e2 = _route(sc, rows=rows)
        h = _peer(xn2, hn, th, e1, e2, w["u"], w["vt"], rows=rows)
    return h.reshape(batch, lp, D_MODEL)[:, N_META:seq_len]


def kernel(x_prompt, x_sample, meta_tokens, norm1_g, w_in, conv_w, conv_b, rg_wa, rg_ba, rg_wi, rg_bi, rg_lambda, q_norm_g, w_uq, kv_norm_g, w_ukv, q_head_g, k_head_g, out_g_rg, out_g_attn, w_out, norm2_g, peer_wq, peer_subkeys, peer_u, peer_v):
    params = (norm1_g, w_in, conv_w, conv_b, rg_wa, rg_ba, rg_wi, rg_bi, rg_lambda, q_norm_g, w_uq, kv_norm_g,
              w_ukv, q_head_g, k_head_g, out_g_rg, out_g_attn, w_out, norm2_g, peer_wq, peer_subkeys, peer_u,
              peer_v)
    layers = [_layer_weights(i, *params) for i in range(norm1_g.shape[0])]
    y_prompt = _encode(x_prompt, meta_tokens, layers)
    y_sample = _encode(x_sample, meta_tokens, layers)
    return (y_prompt, y_sample)
```

```python
import functools
import math

import jax
import jax.numpy as jnp
from jax import lax
from jax.experimental import pallas as pl
from jax.experimental.pallas import tpu as pltpu

F32 = jnp.float32
BF16 = jnp.bfloat16

D_MODEL = 1024
N_META = 16
EPS = 1e-6
RG_WIDTH = 512
RG_HEADS = 8
RG_BLOCK = 64
CONV_W = 4
RG_C = 8.0
N_HEADS = 8
QK_NOPE = 64
QK_ROPE = 32
QK_DIM = 96
V_DIM = 64
Q_LORA = 256
KV_LORA = 128
ROPE_THETA = 10000.0
P_IN = 2 * RG_WIDTH + Q_LORA + KV_LORA + QK_ROPE
PEER_HEADS = 8
N_KEYS = 128
N_EXPERTS = N_KEYS * N_KEYS
PEER_TOPK = 16
PEER_HALF = 128

LANES = 128
SUBLANES = 8
HEAD_PAD = LANES
P_IN_PAD = 12 * LANES
ROW_TILE = 512
ATTN_Q_TILE = 528
SCAN_CHUNK = ATTN_Q_TILE // 2
KV_CHUNK = 1024
PEER_TOKENS = 512
ROUTE_TOKENS = 256
PEER_I1_BLOCK = 16
VMEM_LIMIT = 52 * 1024 * 1024
NEG_INF = float("-inf")
MASK_VALUE = -1e30
LOG2E = 1.4426950408889634


def _cparams(*sem):
    return pltpu.CompilerParams(dimension_semantics=sem, vmem_limit_bytes=VMEM_LIMIT)


def _rms(x, g):
    return x * lax.rsqrt(jnp.mean(x * x, axis=-1, keepdims=True) + EPS) * g


def _gelu_tanh(x):
    return x * (0.5 * (1.0 + jnp.tanh(0.7978845608028654 * (x + 0.044715 * (x * x * x)))))


def _sigmoid(x):
    return 1.0 / (1.0 + jnp.exp(-x))


def _neg_expm1(y):
    u = jnp.exp(y)
    um1 = u - 1.0
    tiny = um1 == 0.0
    r = jnp.where(tiny, y, um1 * y / jnp.where(tiny, 1.0, jnp.log(u)))
    return -jnp.where(um1 == -1.0, -1.0, r)


def _dot(a, b):
    return jnp.dot(a, b, preferred_element_type=F32)


def _dot_nt(a, b):
    return lax.dot_general(a, b, (((1,), (1,)), ((), ())), preferred_element_type=F32)


def _full(shape):
    n = len(shape)
    return pl.BlockSpec(shape, lambda *_: (0,) * n)


def _inproj_body(x_ref, cos_ref, sin_ref, g1_ref, win_ref, qg_ref, wq_ref, wqs_ref, kvg_ref,
                 wk_ref, ek_ref, eks_ref, wv_ref, gq_ref, gqs_ref, gk_ref, gks_ref,
                 xr_ref, gate_ref, q_ref, k_ref, v_ref):
    xn = _rms(x_ref[...], g1_ref[...])
    p = _dot(xn.astype(BF16), win_ref[...])
    xr_ref[...] = p[:, :RG_WIDTH]
    gate_ref[...] = p[:, RG_WIDTH:2 * RG_WIDTH]
    o2 = 2 * RG_WIDTH
    qcn = _rms(p[:, o2:o2 + Q_LORA], qg_ref[...]).astype(BF16)
    kvn = _rms(p[:, o2 + Q_LORA:o2 + Q_LORA + KV_LORA], kvg_ref[...]).astype(BF16)
    kpe = p[:, o2 + Q_LORA + KV_LORA:].astype(BF16)
    q = _dot(qcn, wq_ref[...])
    qs = _dot(qcn, wqs_ref[...])
    k = _dot(kvn, wk_ref[...]) + _dot(kpe, ek_ref[...])
    ks = _dot(kpe, eks_ref[...])
    v = _dot(kvn, wv_ref[...]).astype(BF16)
    c = cos_ref[...]
    s = sin_ref[...]
    qscale = QK_DIM ** -0.5 * LOG2E
    for h in range(N_HEADS):
        sl = slice(h * HEAD_PAD, (h + 1) * HEAD_PAD)
        qh = q[:, sl]
        rq = lax.rsqrt(jnp.sum(qh * qh, axis=-1, keepdims=True) * (1.0 / QK_DIM) + EPS)
        q_ref[h] = ((qh * gq_ref[...] * c + qs[:, sl] * gqs_ref[...] * s) * (rq * qscale)).astype(BF16)
        kh = k[:, sl]
        rk = lax.rsqrt(jnp.sum(kh * kh, axis=-1, keepdims=True) * (1.0 / QK_DIM) + EPS)
        k_ref[h] = ((kh * gk_ref[...] * c + ks[:, sl] * gks_ref[...] * s) * rk).astype(BF16)
        v_ref[h] = v[:, sl]


def _inproj(h, cos_t, sin_t, w, *, rows):
    tm = ROW_TILE
    row = lambda n: pl.BlockSpec((tm, n), lambda i: (i, 0))
    pos = row(HEAD_PAD)
    wide = N_HEADS * HEAD_PAD
    heads = pl.BlockSpec((N_HEADS, tm, HEAD_PAD), lambda i: (0, i, 0))
    heads_shape = jax.ShapeDtypeStruct((N_HEADS, rows, HEAD_PAD), BF16)
    return pl.pallas_call(
        _inproj_body,
        grid=(rows // tm,),
        in_specs=[row(D_MODEL), pos, pos, _full((1, D_MODEL)), _full((D_MODEL, P_IN_PAD)),
                  _full((1, Q_LORA)), _full((Q_LORA, wide)), _full((Q_LORA, wide)),
                  _full((1, KV_LORA)), _full((KV_LORA, wide)), _full((LANES, wide)), _full((LANES, wide)),
                  _full((KV_LORA, wide)),
                  _full((1, HEAD_PAD)), _full((1, HEAD_PAD)), _full((1, HEAD_PAD)), _full((1, HEAD_PAD))],
        out_specs=[row(RG_WIDTH), row(RG_WIDTH), heads, heads, heads],
        out_shape=[jax.ShapeDtypeStruct((rows, RG_WIDTH), F32), jax.ShapeDtypeStruct((rows, RG_WIDTH), F32),
                   heads_shape, heads_shape, heads_shape],
        compiler_params=_cparams("parallel"),
        name="inproj",
    )(h, cos_t, sin_t, w["g1"], w["win"], w["qg"], w["wq"], w["wqs"], w["kvg"], w["wk"], w["ek"], w["eks"],
      w["wv"], w["gq"], w["gqs"], w["gk"], w["gks"])


def _scan_body(xm_f, xp_f, xn_f, xm_b, xp_b, xn_b, cw_ref, cb_ref, wf_ref, wb_ref, bias_ref, lam_ref,
               hf_ref, hb_ref, ext_s, a_s, b_s, cf_s, cb_s, *, seq_len, chunk, n_chunks):
    i = pl.program_id(1)
    C = chunk
    groups = C // SUBLANES

    @pl.when(i == 0)
    def _():
        cf_s[...] = jnp.zeros_like(cf_s)
        cb_s[...] = jnp.zeros_like(cb_s)

    rows8 = lax.broadcasted_iota(jnp.int32, (SUBLANES, RG_WIDTH), 0)
    rows_c = lax.broadcasted_iota(jnp.int32, (C, RG_WIDTH), 0)
    rmod = rows_c & (SUBLANES - 1)

    def masked(x, pos):
        return jnp.where((pos >= 0) & (pos < seq_len), x, 0.0)

    def affine_terms(xm, xp, xn, j, w_ref, d):
        base = j * C
        ext_s[0:SUBLANES, :] = masked(xp[...], base - SUBLANES + rows8)
        ext_s[SUBLANES:SUBLANES + C, :] = masked(xm[...], base + rows_c)
        ext_s[SUBLANES + C:, :] = masked(xn[...], base + C + rows8)
        xc = cb_ref[...]
        for t in range(CONV_W):
            xc = xc + cw_ref[t:t + 1, :] * ext_s[SUBLANES - 2 + t:SUBLANES - 2 + t + C, :]
        gm = _dot(xc.astype(BF16), w_ref[...])
        r = _sigmoid(gm[:, :RG_WIDTH] + bias_ref[2 * d:2 * d + 1, :])
        ig = _sigmoid(gm[:, RG_WIDTH:] + bias_ref[2 * d + 1:2 * d + 2, :])
        z = -lam_ref[d:d + 1, :]
        softplus = jnp.maximum(z, 0.0) + jnp.log1p(jnp.exp(-jnp.abs(z)))
        log_a = (-RG_C) * r * softplus
        a = jnp.exp(log_a)
        b = jnp.sqrt(_neg_expm1(2.0 * log_a)) * (ig * xc)
        b = jnp.where(base + rows_c < seq_len, b, 0.0)
        return a, b

    def rotate_in_groups(x, shift):
        return pltpu.roll(x.reshape(groups, SUBLANES, RG_WIDTH), shift, 1).reshape(C, RG_WIDTH)

    a, b = affine_terms(xm_f, xp_f, xn_f, i, wf_ref, 0)
    for d in (1, 2, 4):
        m = rmod >= d
        a_sh = rotate_in_groups(a, d)
        b_sh = rotate_in_groups(b, d)
        b = jnp.where(m, a * b_sh + b, b)
        a = jnp.where(m, a * a_sh, a)
    a_s[...] = a
    b_s[...] = b

    def fwd_group(g, carry):
        r0 = pl.multiple_of(g * SUBLANES, SUBLANES)
        hrow = a_s[pl.ds(r0, SUBLANES), :] * carry + b_s[pl.ds(r0, SUBLANES), :]
        hf_ref[pl.ds(r0, SUBLANES), :] = hrow
        return jnp.broadcast_to(hrow[SUBLANES - 1:SUBLANES, :], (SUBLANES, RG_WIDTH))

    cf_s[...] = lax.fori_loop(0, groups, fwd_group, cf_s[...])

    a, b = affine_terms(xm_b, xp_b, xn_b, n_chunks - 1 - i, wb_ref, 1)
    for d in (1, 2, 4):
        m = rmod < SUBLANES - d
        a_sh = rotate_in_groups(a, SUBLANES - d)
        b_sh = rotate_in_groups(b, SUBLANES - d)
        b = jnp.where(m, a * b_sh + b, b)
        a = jnp.where(m, a * a_sh, a)
    a_s[...] = a
    b_s[...] = b

    def bwd_group(g, carry):
        r0 = pl.multiple_of((groups - 1 - g) * SUBLANES, SUBLANES)
        hrow = a_s[pl.ds(r0, SUBLANES), :] * carry + b_s[pl.ds(r0, SUBLANES), :]
        hb_ref[pl.ds(r0, SUBLANES), :] = hrow
        return jnp.broadcast_to(hrow[0:1, :], (SUBLANES, RG_WIDTH))

    cb_s[...] = lax.fori_loop(0, groups, bwd_group, cb_s[...])


def _scan(xr, w, *, batch, padded_len, seq_len):
    C = SCAN_CHUNK
    n = padded_len // C
    per8 = C // SUBLANES
    last8 = batch * padded_len // SUBLANES - 1

    def chunk_f(b, i):
        return b * n + i

    def chunk_b(b, i):
        return b * n + (n - 1 - i)

    def specs(chunk_of):
        main = pl.BlockSpec((C, RG_WIDTH), lambda b, i: (chunk_of(b, i), 0))
        prev = pl.BlockSpec((SUBLANES, RG_WIDTH), lambda b, i: (jnp.maximum(chunk_of(b, i) * per8 - 1, 0), 0))
        nxt = pl.BlockSpec((SUBLANES, RG_WIDTH),
                           lambda b, i: (jnp.minimum((chunk_of(b, i) + 1) * per8, last8), 0))
        return [main, prev, nxt]

    rows = batch * padded_len
    return pl.pallas_call(
        functools.partial(_scan_body, seq_len=seq_len, chunk=C, n_chunks=n),
        grid=(batch, n),
        in_specs=specs(chunk_f) + specs(chunk_b) + [
            _full((CONV_W, RG_WIDTH)), _full((1, RG_WIDTH)),
            _full((RG_WIDTH, 2 * RG_WIDTH)), _full((RG_WIDTH, 2 * RG_WIDTH)),
            _full((4, RG_WIDTH)), _full((2, RG_WIDTH))],
        out_specs=[pl.BlockSpec((C, RG_WIDTH), lambda b, i: (chunk_f(b, i), 0)),
                   pl.BlockSpec((C, RG_WIDTH), lambda b, i: (chunk_b(b, i), 0))],
        out_shape=[jax.ShapeDtypeStruct((rows, RG_WIDTH), F32), jax.ShapeDtypeStruct((rows, RG_WIDTH), F32)],
        scratch_shapes=[pltpu.VMEM((C + 2 * SUBLANES, RG_WIDTH), F32), pltpu.VMEM((C, RG_WIDTH), F32),
                        pltpu.VMEM((C, RG_WIDTH), F32), pltpu.VMEM((SUBLANES, RG_WIDTH), F32),
                        pltpu.VMEM((SUBLANES, RG_WIDTH), F32)],
        compiler_params=_cparams("arbitrary", "arbitrary"),
        name="rglru_scan",
    )(xr, xr, xr, xr, xr, xr, w["conv_w"], w["conv_b"], w["wgate_f"], w["wgate_b"], w["gate_bias"], w["lam"])


def _attn_body(q_ref, kt_ref, v_ref, o_ref, s_a, s_b, m_s, l_s, acc_s, *, seq_len, n_full, rem):
    tq = q_ref.shape[2]
    m_s[...] = jnp.full_like(m_s, NEG_INF)
    l_s[...] = jnp.zeros_like(l_s)
    acc_s[...] = jnp.zeros_like(acc_s)

    def scores(dst, start, width):
        dst[:, :width] = _dot(q_ref[0, 0], kt_ref[0, 0, :, pl.ds(start, width)])

    def update(src, start, width, mask):
        s = src[:, :width]
        if mask:
            kpos = start + lax.broadcasted_iota(jnp.int32, (tq, width), 1)
            s = jnp.where(kpos < seq_len, s, MASK_VALUE)
        m = m_s[...]
        m_new = jnp.maximum(m, jnp.max(s, axis=-1, keepdims=True))
        alpha = jnp.exp2(m - m_new)
        p = jnp.exp2(s - m_new)
        m_s[...] = m_new
        l_s[...] = alpha * l_s[...] + jnp.sum(p, axis=-1, keepdims=True)
        acc_s[...] = alpha * acc_s[...] + _dot(p.astype(BF16), v_ref[0, 0, pl.ds(start, width), :])

    if rem:
        scores(s_b, n_full * KV_CHUNK, rem)
    if n_full:
        scores(s_a, 0, KV_CHUNK)
    if rem:
        update(s_b, n_full * KV_CHUNK, rem, True)
    pairs = max((n_full - 1) // 2, 0)
    tail = [(c * KV_CHUNK, KV_CHUNK, False) for c in range(2 * pairs, n_full)]

    def pair(i, carry):
        c0 = pl.multiple_of(2 * i * KV_CHUNK, KV_CHUNK)
        scores(s_b, c0 + KV_CHUNK, KV_CHUNK)
        update(s_a, c0, KV_CHUNK, False)
        scores(s_a, c0 + 2 * KV_CHUNK, KV_CHUNK)
        update(s_b, c0 + KV_CHUNK, KV_CHUNK, False)
        return carry

    lax.fori_loop(0, pairs, pair, 0)
    bufs = (s_a, s_b)
    for n, (start, width, mask) in enumerate(tail):
        if n + 1 < len(tail):
            scores(bufs[(n + 1) % 2], tail[n + 1][0], tail[n + 1][1])
        update(bufs[n % 2], start, width, mask)
    o_ref[0, 0] = acc_s[...] / l_s[...]


def _attention(q, kt, v, *, batch, padded_len, seq_len):
    tq = ATTN_Q_TILE
    n_full = seq_len // KV_CHUNK
    rem = -(-(seq_len - n_full * KV_CHUNK) // LANES) * LANES
    q_tile = pl.BlockSpec((1, 1, tq, HEAD_PAD), lambda b, h, i: (h, b, i, 0))
    return pl.pallas_call(
        functools.partial(_attn_body, seq_len=seq_len, n_full=n_full, rem=rem),
        grid=(batch, N_HEADS, padded_len // tq),
        in_specs=[q_tile,
                  pl.BlockSpec((1, 1, HEAD_PAD, padded_len), lambda b, h, i: (h, b, 0, 0)),
                  pl.BlockSpec((1, 1, padded_len, HEAD_PAD), lambda b, h, i: (h, b, 0, 0))],
        out_specs=q_tile,
        out_shape=jax.ShapeDtypeStruct((N_HEADS, batch, padded_len, HEAD_PAD), F32),
        scratch_shapes=[pltpu.VMEM((tq, KV_CHUNK), F32), pltpu.VMEM((tq, KV_CHUNK), F32),
                        pltpu.VMEM((tq, 1), F32), pltpu.VMEM((tq, 1), F32), pltpu.VMEM((tq, HEAD_PAD), F32)],
        compiler_params=_cparams("parallel", "parallel", "arbitrary"),
        name="attention",
    )(q, kt, v)


def _outproj_body(hf_ref, hb_ref, gate_ref, o_ref, h_ref, grg_ref, gat_ref, wrg_ref, wat_ref, g2_ref,
                  wqt_ref, sk_ref, hn_ref, xn2_ref, sc_ref):
    rg = (hf_ref[...] + hb_ref[...]) * _gelu_tanh(gate_ref[...])
    rgn = _rms(rg, grg_ref[...])
    o = jnp.concatenate([o_ref[hd] for hd in range(N_HEADS)], axis=1)
    ms = jnp.sum(o * o, axis=-1, keepdims=True) * (1.0 / (N_HEADS * V_DIM))
    attn = o * lax.rsqrt(ms + EPS) * gat_ref[...]
    hn = h_ref[...] + _dot(rgn.astype(BF16), wrg_ref[...]) + _dot(attn.astype(BF16), wat_ref[...])
    hn_ref[...] = hn
    xn2 = _rms(hn, g2_ref[...]).astype(BF16)
    xn2_ref[...] = xn2
    qpt = _dot_nt(wqt_ref[...], xn2)
    for h in range(PEER_HEADS):
        for s in range(2):
            r0 = (h * 2 + s) * PEER_HALF
            blk = qpt[r0:r0 + PEER_HALF, :].astype(BF16)
            o0 = s * PEER_HEADS * N_KEYS + h * N_KEYS
            sc_ref[o0:o0 + N_KEYS, :] = _dot(sk_ref[h * 2 + s], blk)


def _outproj(hf, hb, gate, o, h, w, *, rows):
    tm = ROW_TILE
    row = lambda n: pl.BlockSpec((tm, n), lambda i: (i, 0))
    wide = N_HEADS * HEAD_PAD
    nsc = 2 * PEER_HEADS * N_KEYS
    return pl.pallas_call(
        _outproj_body,
        grid=(rows // tm,),
        in_specs=[row(RG_WIDTH), row(RG_WIDTH), row(RG_WIDTH),
                  pl.BlockSpec((N_HEADS, tm, HEAD_PAD), lambda i: (0, i, 0)), row(D_MODEL),
                  _full((1, RG_WIDTH)), _full((1, wide)), _full((RG_WIDTH, D_MODEL)), _full((wide, D_MODEL)),
                  _full((1, D_MODEL)), _full((nsc, D_MODEL)), _full((2 * PEER_HEADS, N_KEYS, PEER_HALF))],
        out_specs=[row(D_MODEL), row(D_MODEL), pl.BlockSpec((nsc, tm), lambda i: (0, i))],
        out_shape=[jax.ShapeDtypeStruct((rows, D_MODEL), F32), jax.ShapeDtypeStruct((rows, D_MODEL), BF16),
                   jax.ShapeDtypeStruct((nsc, rows), F32)],
        compiler_params=_cparams("parallel"),
        name="outproj",
    )(hf, hb, gate, o, h, w["grg"], w["gat"], w["wout_rg"], w["wout_at"], w["g2"], w["wqt"], w["subk"])


def _sort_network(n):
    pairs, p = [], 1
    while p < n:
        k = p
        while k >= 1:
            for j in range(k % p, n - k, 2 * k):
                for i in range(min(k, n - j - k)):
                    if (i + j) // (2 * p) == (i + j + k) // (2 * p):
                        pairs.append((i + j, i + j + k))
            k //= 2
        p *= 2
    return pairs


def _top_rows(x, count):
    n = x.shape[0] // SUBLANES
    cols = [x[k * SUBLANES:(k + 1) * SUBLANES, :] for k in range(n)]
    for a, b in _sort_network(n):
        cols[a], cols[b] = jnp.maximum(cols[a], cols[b]), jnp.minimum(cols[a], cols[b])
    cols.append(jnp.full_like(cols[0], NEG_INF))
    rows = []
    for r in range(count):
        m = jnp.max(cols[0], axis=0, keepdims=True)
        rows.append(m)
        depth = min(n, count - 1 - r)
        hit = cols[0] == m
        for k in range(depth):
            cols[k] = jnp.where(hit, cols[k + 1], cols[k])
    return rows


def _route_body(sc_ref, th_ref, e1_ref, e2_ref):
    t = sc_ref.shape[1]
    half = PEER_HEADS * N_KEYS
    jrow = lax.broadcasted_iota(jnp.int32, (SUBLANES, t), 0)

    def bc(row):
        return jnp.broadcast_to(row, (SUBLANES, t))

    def head(h, carry):
        r0 = pl.multiple_of(h * N_KEYS, N_KEYS)
        s1 = sc_ref[pl.ds(r0, N_KEYS), :]
        s2 = sc_ref[pl.ds(half + r0, N_KEYS), :]
        a = _top_rows(s1, PEER_TOPK + 1)
        b = _top_rows(s2, PEER_TOPK + 1)
        b_lo = jnp.concatenate(b[:SUBLANES], axis=0)
        b_hi = jnp.concatenate(b[SUBLANES:2 * SUBLANES], axis=0)
        a_hi = jnp.concatenate(a[SUBLANES:2 * SUBLANES], axis=0)
        pieces = [bc(a[0]) + b_lo, bc(a[0]) + b_hi, bc(a[1]) + b_lo]
        for i in range(2, SUBLANES):
            pieces.append(jnp.where(jrow < (PEER_TOPK + 1) // (i + 1), bc(a[i]) + b_lo, NEG_INF))
        pieces.append(a_hi + bc(b[0]))
        ends = jnp.concatenate([a[0] + b[PEER_TOPK], a[PEER_TOPK] + b[0]]
                               + [jnp.full((1, t), NEG_INF, F32)] * (SUBLANES - 2), axis=0)
        pieces.append(ends)
        cand = jnp.concatenate(pieces, axis=0)
        x = cand
        for _ in range(PEER_TOPK - 1):
            x = jnp.where(x == jnp.max(x, axis=0, keepdims=True), NEG_INF, x)
        c16 = jnp.max(x, axis=0, keepdims=True)
        c17 = jnp.max(jnp.where(x == c16, NEG_INF, x), axis=0, keepdims=True)
        thr = 0.5 * (c16 + c17)
        top = a[0] + b[0]
        z = jnp.sum(jnp.where(cand >= thr, jnp.exp(cand - top), 0.0), axis=0, keepdims=True)
        th_ref[pl.ds(r0, N_KEYS), :] = jnp.exp((thr - s1) - b[0])
        e1_ref[pl.ds(r0, N_KEYS), :] = jnp.exp(s1 - a[0]) / z
        e2_ref[pl.ds(r0, N_KEYS), :] = jnp.exp(s2 - b[0])
        return carry

    lax.fori_loop(0, PEER_HEADS, head, 0)


def _route(sc, *, rows):
    t = ROUTE_TOKENS
    half = PEER_HEADS * N_KEYS
    out = pl.BlockSpec((half, t), lambda i: (0, i))
    return pl.pallas_call(
        _route_body,
        grid=(rows // t,),
        in_specs=[pl.BlockSpec((2 * half, t), lambda i: (0, i))],
        out_specs=[out, out, out],
        out_shape=[jax.ShapeDtypeStruct((half, rows), F32)] * 3,
        compiler_params=_cparams("parallel"),
        name="peer_route",
    )(sc)


def _peer_body(xn_ref, h_ref, th_ref, e1_ref, e2_ref, u_ref, vt_ref, out_ref, acc_s, hid_s, w_s):
    j = pl.program_id(1)
    tt = xn_ref.shape[0]

    @pl.when(j == 0)
    def _():
        acc_s[...] = jnp.zeros_like(acc_s)

    hid_s[...] = _dot_nt(u_ref[...], xn_ref[...])

    i1_base = pl.multiple_of(j * PEER_I1_BLOCK, PEER_I1_BLOCK)
    for il in range(PEER_I1_BLOCK):
        es = slice(il * N_KEYS, (il + 1) * N_KEYS)
        for lt in range(tt // LANES):
            ln = slice(lt * LANES, (lt + 1) * LANES)
            g = jnp.zeros((N_KEYS, LANES), F32)
            for h in range(PEER_HEADS):
                hs = slice(h * N_KEYS, (h + 1) * N_KEYS)
                th = th_ref[pl.ds(h * N_KEYS + i1_base, PEER_I1_BLOCK), ln][il:il + 1, :]
                e1 = e1_ref[pl.ds(h * N_KEYS + i1_base, PEER_I1_BLOCK), ln][il:il + 1, :]
                e2 = e2_ref[hs, ln]
                g = g + jnp.where(e2 >= th, e2, 0.0) * e1
            w_s[es, ln] = (g * _gelu_tanh(hid_s[es, ln])).astype(BF16)
    acc_s[...] += _dot(vt_ref[...], w_s[...])

    @pl.when(j == pl.num_programs(1) - 1)
    def _():
        out_ref[...] = h_ref[...] + acc_s[...].T


def _peer(xn2, hn, th, e1, e2, u, vt, *, rows):
    tt = PEER_TOKENS
    eb = PEER_I1_BLOCK * N_KEYS
    half = PEER_HEADS * N_KEYS
    tok = pl.BlockSpec((tt, D_MODEL), lambda i, j: (i, 0))
    tab = pl.BlockSpec((half, tt), lambda i, j: (0, i))
    return pl.pallas_call(
        _peer_body,
        grid=(rows // tt, N_EXPERTS // eb),
        in_specs=[tok, tok, tab, tab, tab,
                  pl.BlockSpec((eb, D_MODEL), lambda i, j: (j, 0)),
                  pl.BlockSpec((D_MODEL, eb), lambda i, j: (0, j))],
        out_specs=pl.BlockSpec((tt, D_MODEL), lambda i, j: (i, 0)),
        out_shape=jax.ShapeDtypeStruct((rows, D_MODEL), F32),
        scratch_shapes=[pltpu.VMEM((D_MODEL, tt), F32), pltpu.VMEM((eb, tt), F32), pltpu.VMEM((eb, tt), BF16)],
        compiler_params=_cparams("parallel", "arbitrary"),
        name="peer_dense",
    )(xn2, hn, th, e1, e2, u, vt)


def _block_diag(w):
    eye = jnp.eye(RG_HEADS, dtype=w.dtype)
    return jnp.einsum("hij,hg->higj", w, eye).reshape(RG_WIDTH, RG_WIDTH)


def _head_pad(w, lo, hi):
    k = w.shape[0]
    part = w[:, :, lo:hi]
    return jnp.pad(part, ((0, 0), (0, 0), (0, HEAD_PAD - (hi - lo)))).reshape(k, N_HEADS * HEAD_PAD)


def _rope_partner(x1, x2):
    z64 = jnp.zeros(x1.shape[:-1] + (QK_NOPE,), x1.dtype)
    z32 = jnp.zeros(x1.shape[:-1] + (HEAD_PAD - QK_DIM,), x1.dtype)
    return jnp.concatenate([z64, x2, x1, z32], axis=-1)


def _layer_weights(i, norm1_g, w_in, conv_w, conv_b, rg_wa, rg_ba, rg_wi, rg_bi, rg_lambda, q_norm_g, w_uq,
                   kv_norm_g, w_ukv, q_head_g, k_head_g, out_g_rg, out_g_attn, w_out, norm2_g, peer_wq,
                   peer_subkeys, peer_u, peer_v):
    half = QK_ROPE // 2
    wq3 = w_uq[i].reshape(Q_LORA, N_HEADS, QK_DIM)
    wkv3 = w_ukv[i].reshape(KV_LORA, N_HEADS, QK_NOPE + V_DIM)
    eye = jnp.eye(QK_ROPE, dtype=F32)
    ek = jnp.concatenate([jnp.zeros((QK_ROPE, QK_NOPE), F32), eye,
                          jnp.zeros((QK_ROPE, HEAD_PAD - QK_DIM), F32)], axis=-1)
    eks = _rope_partner(eye[:, :half], eye[:, half:])
    place = lambda e: jnp.pad(jnp.tile(e, (1, N_HEADS)), ((0, LANES - QK_ROPE), (0, 0))).astype(BF16)

    def head_gain(g):
        main = jnp.pad(g, (0, HEAD_PAD - QK_DIM)).reshape(1, HEAD_PAD)
        swap = _rope_partner(g[QK_NOPE:QK_NOPE + half], g[QK_NOPE + half:]).reshape(1, HEAD_PAD)
        return main, swap

    gq, gqs = head_gain(q_head_g[i])
    gk, gks = head_gain(k_head_g[i])
    wout = w_out[i]
    wout_at = jnp.pad(wout[RG_WIDTH:].reshape(N_HEADS, V_DIM, D_MODEL),
                      ((0, 0), (0, HEAD_PAD - V_DIM), (0, 0))).reshape(N_HEADS * HEAD_PAD, D_MODEL)
    gat = jnp.pad(out_g_attn[i].reshape(N_HEADS, V_DIM), ((0, 0), (0, HEAD_PAD - V_DIM))).reshape(1, -1)
    return dict(
        g1=norm1_g[i].reshape(1, -1),
        win=jnp.pad(w_in[i], ((0, 0), (0, P_IN_PAD - P_IN))).astype(BF16),
        qg=q_norm_g[i].reshape(1, -1),
        wq=_head_pad(wq3, 0, QK_DIM).astype(BF16),
        wqs=_rope_partner(wq3[:, :, QK_NOPE:QK_NOPE + half], wq3[:, :, QK_NOPE + half:])
        .reshape(Q_LORA, -1).astype(BF16),
        kvg=kv_norm_g[i].reshape(1, -1),
        wk=_head_pad(wkv3, 0, QK_NOPE).astype(BF16),
        wv=_head_pad(wkv3, QK_NOPE, QK_NOPE + V_DIM).astype(BF16),
        ek=place(ek), eks=place(eks), gq=gq, gqs=gqs, gk=gk, gks=gks,
        conv_w=conv_w[i], conv_b=conv_b[i].reshape(1, -1),
        wgate_f=jnp.concatenate([_block_diag(rg_wa[i, 0]), _block_diag(rg_wi[i, 0])], axis=1).astype(BF16),
        wgate_b=jnp.concatenate([_block_diag(rg_wa[i, 1]), _block_diag(rg_wi[i, 1])], axis=1).astype(BF16),
        gate_bias=jnp.stack([rg_ba[i, 0].reshape(-1), rg_bi[i, 0].reshape(-1),
                             rg_ba[i, 1].reshape(-1), rg_bi[i, 1].reshape(-1)]),
        lam=rg_lambda[i],
        grg=out_g_rg[i].reshape(1, -1), gat=gat,
        wout_rg=wout[:RG_WIDTH].astype(BF16), wout_at=wout_at.astype(BF16),
        g2=norm2_g[i].reshape(1, -1),
        wqt=peer_wq[i].T.astype(BF16),
        subk=peer_subkeys[i].reshape(2 * PEER_HEADS, N_KEYS, PEER_HALF).astype(BF16),
        u=peer_u[i].astype(BF16),
        vt=peer_v[i].T.astype(BF16),
    )


def _rope_tables(length):
    pos = jnp.arange(length, dtype=F32)
    inv = ROPE_THETA ** (-jnp.arange(0, QK_ROPE, 2, dtype=F32) / QK_ROPE)
    ang = pos[:, None] * inv[None, :]
    cos, sin = jnp.cos(ang), jnp.sin(ang)
    ones = jnp.ones((length, QK_NOPE), F32)
    zpad = jnp.zeros((length, HEAD_PAD - QK_DIM), F32)
    cos_t = jnp.concatenate([ones, cos, cos, zpad], axis=-1)
    sin_t = jnp.concatenate([jnp.zeros((length, QK_NOPE), F32), -sin, sin, zpad], axis=-1)
    return cos_t, sin_t


def _padded_len(batch, length):
    lp = -(-length // ATTN_Q_TILE) * ATTN_Q_TILE
    row_tile = math.lcm(ROW_TILE, PEER_TOKENS, ROUTE_TOKENS)
    while (batch * lp) % row_tile:
        lp += ATTN_Q_TILE
    return lp


def _encode(x, meta_tokens, layers):
    batch, n_tok, _ = x.shape
    seq_len = n_tok + N_META
    lp = _padded_len(batch, seq_len)
    rows = batch * lp
    meta = jnp.broadcast_to(meta_tokens[None].astype(x.dtype), (batch, N_META, D_MODEL))
    h = jnp.concatenate([meta, x, jnp.zeros((batch, lp - seq_len, D_MODEL), x.dtype)], axis=1)
    h = h.reshape(rows, D_MODEL)
    cos_t, sin_t = (jnp.tile(t, (batch, 1)) for t in _rope_tables(lp))
    for w in layers:
        xr, gate, q, k, v = _inproj(h, cos_t, sin_t, w, rows=rows)
        hf, hb = _scan(xr, w, batch=batch, padded_len=lp, seq_len=seq_len)
        per_seq = (N_HEADS, batch, lp, HEAD_PAD)
        kt = k.reshape(per_seq).transpose(0, 1, 3, 2)
        o = _attention(q.reshape(per_seq), kt, v.reshape(per_seq), batch=batch, padded_len=lp, seq_len=seq_len)
        hn, xn2, sc = _outproj(hf, hb, gate, o.reshape(N_HEADS, rows, HEAD_PAD), h, w, rows=rows)
        th, e1, e2 = _route(sc, rows=rows)
        h = _peer(xn2, hn, th, e1, e2, w["u"], w["vt"], rows=rows)
    return h.reshape(batch, lp, D_MODEL)[:, N_META:seq_len]


def kernel(x_prompt, x_sample, meta_tokens, norm1_g, w_in, conv_w, conv_b, rg_wa, rg_ba, rg_wi, rg_bi, rg_lambda, q_norm_g, w_uq, kv_norm_g, w_ukv, q_head_g, k_head_g, out_g_rg, out_g_attn, w_out, norm2_g, peer_wq, peer_subkeys, peer_u, peer_v):
    params = (norm1_g, w_in, conv_w, conv_b, rg_wa, rg_ba, rg_wi, rg_bi, rg_lambda, q_norm_g, w_uq, kv_norm_g,
              w_ukv, q_head_g, k_head_g, out_g_rg, out_g_attn, w_out, norm2_g, peer_wq, peer_subkeys, peer_u,
              peer_v)
    layers = [_layer_weights(i, *params) for i in range(norm1_g.shape[0])]
    y_prompt = _encode(x_prompt, meta_tokens, layers)
    y_sample = _encode(x_sample, meta_tokens, layers)
    return (y_prompt, y_sample)
```

```python
import functools
import math

import jax
import jax.numpy as jnp
from jax import lax
from jax.experimental import pallas as pl
from jax.experimental.pallas import tpu as pltpu

F32 = jnp.float32
BF16 = jnp.bfloat16

D_MODEL = 1024
N_META = 16
EPS = 1e-6
RG_WIDTH = 512
RG_HEADS = 8
RG_BLOCK = 64
CONV_W = 4
RG_C = 8.0
N_HEADS = 8
QK_NOPE = 64
QK_ROPE = 32
QK_DIM = 96
V_DIM = 64
Q_LORA = 256
KV_LORA = 128
ROPE_THETA = 10000.0
P_IN = 2 * RG_WIDTH + Q_LORA + KV_LORA + QK_ROPE
PEER_HEADS = 8
N_KEYS = 128
N_EXPERTS = N_KEYS * N_KEYS
PEER_TOPK = 16
PEER_HALF = 128

LANES = 128
SUBLANES = 8
HEAD_PAD = LANES
P_IN_PAD = 12 * LANES
ROW_TILE = 512
ATTN_Q_TILE = 1056
SCAN_CHUNK = ATTN_Q_TILE // 4
KV_CHUNK = 1024
PEER_TOKENS = 512
ROUTE_TOKENS = 256
PEER_I1_BLOCK = 16
VMEM_LIMIT = 52 * 1024 * 1024
NEG_INF = float("-inf")
MASK_VALUE = -1e30
LOG2E = 1.4426950408889634


def _cparams(*sem):
    return pltpu.CompilerParams(dimension_semantics=sem, vmem_limit_bytes=VMEM_LIMIT)


def _rms(x, g):
    return x * lax.rsqrt(jnp.mean(x * x, axis=-1, keepdims=True) + EPS) * g


def _gelu_tanh(x):
    return x * (0.5 * (1.0 + jnp.tanh(0.7978845608028654 * (x + 0.044715 * (x * x * x)))))


def _sigmoid(x):
    return 1.0 / (1.0 + jnp.exp(-x))


def _neg_expm1(y):
    u = jnp.exp(y)
    um1 = u - 1.0
    tiny = um1 == 0.0
    r = jnp.where(tiny, y, um1 * y / jnp.where(tiny, 1.0, jnp.log(u)))
    return -jnp.where(um1 == -1.0, -1.0, r)


def _dot(a, b):
    return jnp.dot(a, b, preferred_element_type=F32)


def _dot_nt(a, b):
    return lax.dot_general(a, b, (((1,), (1,)), ((), ())), preferred_element_type=F32)


def _full(shape):
    n = len(shape)
    return pl.BlockSpec(shape, lambda *_: (0,) * n)


def _inproj_body(x_ref, cos_ref, sin_ref, g1_ref, win_ref, qg_ref, wq_ref, wqs_ref, kvg_ref,
                 wk_ref, ek_ref, eks_ref, wv_ref, gq_ref, gqs_ref, gk_ref, gks_ref,
                 xr_ref, gate_ref, q_ref, k_ref, v_ref):
    xn = _rms(x_ref[...], g1_ref[...])
    p = _dot(xn.astype(BF16), win_ref[...])
    xr_ref[...] = p[:, :RG_WIDTH]
    gate_ref[...] = p[:, RG_WIDTH:2 * RG_WIDTH]
    o2 = 2 * RG_WIDTH
    qcn = _rms(p[:, o2:o2 + Q_LORA], qg_ref[...]).astype(BF16)
    kvn = _rms(p[:, o2 + Q_LORA:o2 + Q_LORA + KV_LORA], kvg_ref[...]).astype(BF16)
    kpe = p[:, o2 + Q_LORA + KV_LORA:].astype(BF16)
    q = _dot(qcn, wq_ref[...])
    qs = _dot(qcn, wqs_ref[...])
    k = _dot(kvn, wk_ref[...]) + _dot(kpe, ek_ref[...])
    ks = _dot(kpe, eks_ref[...])
    v = _dot(kvn, wv_ref[...]).astype(BF16)
    c = cos_ref[...]
    s = sin_ref[...]
    qscale = QK_DIM ** -0.5 * LOG2E
    for h in range(N_HEADS):
        sl = slice(h * HEAD_PAD, (h + 1) * HEAD_PAD)
        qh = q[:, sl]
        rq = lax.rsqrt(jnp.sum(qh * qh, axis=-1, keepdims=True) * (1.0 / QK_DIM) + EPS)
        q_ref[h] = ((qh * gq_ref[...] * c + qs[:, sl] * gqs_ref[...] * s) * (rq * qscale)).astype(BF16)
        kh = k[:, sl]
        rk = lax.rsqrt(jnp.sum(kh * kh, axis=-1, keepdims=True) * (1.0 / QK_DIM) + EPS)
        k_ref[h] = ((kh * gk_ref[...] * c + ks[:, sl] * gks_ref[...] * s) * rk).astype(BF16)
        v_ref[h] = v[:, sl]


def _inproj(h, cos_t, sin_t, w, *, rows):
    tm = ROW_TILE
    row = lambda n: pl.BlockSpec((tm, n), lambda i: (i, 0))
    pos = row(HEAD_PAD)
    wide = N_HEADS * HEAD_PAD
    heads = pl.BlockSpec((N_HEADS, tm, HEAD_PAD), lambda i: (0, i, 0))
    heads_shape = jax.ShapeDtypeStruct((N_HEADS, rows, HEAD_PAD), BF16)
    return pl.pallas_call(
        _inproj_body,
        grid=(rows // tm,),
        in_specs=[row(D_MODEL), pos, pos, _full((1, D_MODEL)), _full((D_MODEL, P_IN_PAD)),
                  _full((1, Q_LORA)), _full((Q_LORA, wide)), _full((Q_LORA, wide)),
                  _full((1, KV_LORA)), _full((KV_LORA, wide)), _full((LANES, wide)), _full((LANES, wide)),
                  _full((KV_LORA, wide)),
                  _full((1, HEAD_PAD)), _full((1, HEAD_PAD)), _full((1, HEAD_PAD)), _full((1, HEAD_PAD))],
        out_specs=[row(RG_WIDTH), row(RG_WIDTH), heads, heads, heads],
        out_shape=[jax.ShapeDtypeStruct((rows, RG_WIDTH), F32), jax.ShapeDtypeStruct((rows, RG_WIDTH), F32),
                   heads_shape, heads_shape, heads_shape],
        compiler_params=_cparams("parallel"),
        name="inproj",
    )(h, cos_t, sin_t, w["g1"], w["win"], w["qg"], w["wq"], w["wqs"], w["kvg"], w["wk"], w["ek"], w["eks"],
      w["wv"], w["gq"], w["gqs"], w["gk"], w["gks"])


def _scan_body(xm_f, xp_f, xn_f, xm_b, xp_b, xn_b, cw_ref, cb_ref, wf_ref, wb_ref, bias_ref, lam_ref,
               hf_ref, hb_ref, ext_s, a_s, b_s, cf_s, cb_s, *, seq_len, chunk, n_chunks):
    i = pl.program_id(1)
    C = chunk
    groups = C // SUBLANES

    @pl.when(i == 0)
    def _():
        cf_s[...] = jnp.zeros_like(cf_s)
        cb_s[...] = jnp.zeros_like(cb_s)

    rows8 = lax.broadcasted_iota(jnp.int32, (SUBLANES, RG_WIDTH), 0)
    rows_c = lax.broadcasted_iota(jnp.int32, (C, RG_WIDTH), 0)
    rmod = rows_c & (SUBLANES - 1)

    def masked(x, pos):
        return jnp.where((pos >= 0) & (pos < seq_len), x, 0.0)

    def affine_terms(xm, xp, xn, j, w_ref, d):
        base = j * C
        ext_s[0:SUBLANES, :] = masked(xp[...], base - SUBLANES + rows8)
        ext_s[SUBLANES:SUBLANES + C, :] = masked(xm[...], base + rows_c)
        ext_s[SUBLANES + C:, :] = masked(xn[...], base + C + rows8)
        xc = cb_ref[...]
        for t in range(CONV_W):
            xc = xc + cw_ref[t:t + 1, :] * ext_s[SUBLANES - 2 + t:SUBLANES - 2 + t + C, :]
        gm = _dot(xc.astype(BF16), w_ref[...])
        r = _sigmoid(gm[:, :RG_WIDTH] + bias_ref[2 * d:2 * d + 1, :])
        ig = _sigmoid(gm[:, RG_WIDTH:] + bias_ref[2 * d + 1:2 * d + 2, :])
        z = -lam_ref[d:d + 1, :]
        softplus = jnp.maximum(z, 0.0) + jnp.log1p(jnp.exp(-jnp.abs(z)))
        log_a = (-RG_C) * r * softplus
        a = jnp.exp(log_a)
        b = jnp.sqrt(_neg_expm1(2.0 * log_a)) * (ig * xc)
        b = jnp.where(base + rows_c < seq_len, b, 0.0)
        return a, b

    def rotate_in_groups(x, shift):
        return pltpu.roll(x.reshape(groups, SUBLANES, RG_WIDTH), shift, 1).reshape(C, RG_WIDTH)

    a, b = affine_terms(xm_f, xp_f, xn_f, i, wf_ref, 0)
    for d in (1, 2, 4):
        m = rmod >= d
        a_sh = rotate_in_groups(a, d)
        b_sh = rotate_in_groups(b, d)
        b = jnp.where(m, a * b_sh + b, b)
        a = jnp.where(m, a * a_sh, a)
    a_s[...] = a
    b_s[...] = b

    def fwd_group(g, carry):
        r0 = pl.multiple_of(g * SUBLANES, SUBLANES)
        hrow = a_s[pl.ds(r0, SUBLANES), :] * carry + b_s[pl.ds(r0, SUBLANES), :]
        hf_ref[pl.ds(r0, SUBLANES), :] = hrow
        return jnp.broadcast_to(hrow[SUBLANES - 1:SUBLANES, :], (SUBLANES, RG_WIDTH))

    cf_s[...] = lax.fori_loop(0, groups, fwd_group, cf_s[...])

    a, b = affine_terms(xm_b, xp_b, xn_b, n_chunks - 1 - i, wb_ref, 1)
    for d in (1, 2, 4):
        m = rmod < SUBLANES - d
        a_sh = rotate_in_groups(a, SUBLANES - d)
        b_sh = rotate_in_groups(b, SUBLANES - d)
        b = jnp.where(m, a * b_sh + b, b)
        a = jnp.where(m, a * a_sh, a)
    a_s[...] = a
    b_s[...] = b

    def bwd_group(g, carry):
        r0 = pl.multiple_of((groups - 1 - g) * SUBLANES, SUBLANES)
        hrow = a_s[pl.ds(r0, SUBLANES), :] * carry + b_s[pl.ds(r0, SUBLANES), :]
        hb_ref[pl.ds(r0, SUBLANES), :] = hrow
        return jnp.broadcast_to(hrow[0:1, :], (SUBLANES, RG_WIDTH))

    cb_s[...] = lax.fori_loop(0, groups, bwd_group, cb_s[...])


def _scan(xr, w, *, batch, padded_len, seq_len):
    C = SCAN_CHUNK
    n = padded_len // C
    per8 = C // SUBLANES
    last8 = batch * padded_len // SUBLANES - 1

    def chunk_f(b, i):
        return b * n + i

    def chunk_b(b, i):
        return b * n + (n - 1 - i)

    def specs(chunk_of):
        main = pl.BlockSpec((C, RG_WIDTH), lambda b, i: (chunk_of(b, i), 0))
        prev = pl.BlockSpec((SUBLANES, RG_WIDTH), lambda b, i: (jnp.maximum(chunk_of(b, i) * per8 - 1, 0), 0))
        nxt = pl.BlockSpec((SUBLANES, RG_WIDTH),
                           lambda b, i: (jnp.minimum((chunk_of(b, i) + 1) * per8, last8), 0))
        return [main, prev, nxt]

    rows = batch * padded_len
    return pl.pallas_call(
        functools.partial(_scan_body, seq_len=seq_len, chunk=C, n_chunks=n),
        grid=(batch, n),
        in_specs=specs(chunk_f) + specs(chunk_b) + [
            _full((CONV_W, RG_WIDTH)), _full((1, RG_WIDTH)),
            _full((RG_WIDTH, 2 * RG_WIDTH)), _full((RG_WIDTH, 2 * RG_WIDTH)),
            _full((4, RG_WIDTH)), _full((2, RG_WIDTH))],
        out_specs=[pl.BlockSpec((C, RG_WIDTH), lambda b, i: (chunk_f(b, i), 0)),
                   pl.BlockSpec((C, RG_WIDTH), lambda b, i: (chunk_b(b, i), 0))],
        out_shape=[jax.ShapeDtypeStruct((rows, RG_WIDTH), F32), jax.ShapeDtypeStruct((rows, RG_WIDTH), F32)],
        scratch_shapes=[pltpu.VMEM((C + 2 * SUBLANES, RG_WIDTH), F32), pltpu.VMEM((C, RG_WIDTH), F32),
                        pltpu.VMEM((C, RG_WIDTH), F32), pltpu.VMEM((SUBLANES, RG_WIDTH), F32),
                        pltpu.VMEM((SUBLANES, RG_WIDTH), F32)],
        compiler_params=_cparams("arbitrary", "arbitrary"),
        name="rglru_scan",
    )(xr, xr, xr, xr, xr, xr, w["conv_w"], w["conv_b"], w["wgate_f"], w["wgate_b"], w["gate_bias"], w["lam"])


def _attn_body(q_ref, kt_ref, v_ref, o_ref, s_a, s_b, m_s, l_s, acc_s, *, seq_len, n_full, rem):
    tq = q_ref.shape[2]
    m_s[...] = jnp.full_like(m_s, NEG_INF)
    l_s[...] = jnp.zeros_like(l_s)
    acc_s[...] = jnp.zeros_like(acc_s)

    def scores(dst, start, width):
        dst[:, :width] = _dot(q_ref[0, 0], kt_ref[0, 0, :, pl.ds(start, width)])

    def update(src, start, width, mask):
        s = src[:, :width]
        if mask:
            kpos = start + lax.broadcasted_iota(jnp.int32, (tq, width), 1)
            s = jnp.where(kpos < seq_len, s, MASK_VALUE)
        m = m_s[...]
        m_new = jnp.maximum(m, jnp.max(s, axis=-1, keepdims=True))
        alpha = jnp.exp2(m - m_new)
        p = jnp.exp2(s - m_new)
        m_s[...] = m_new
        l_s[...] = alpha * l_s[...] + jnp.sum(p, axis=-1, keepdims=True)
        acc_s[...] = alpha * acc_s[...] + _dot(p.astype(BF16), v_ref[0, 0, pl.ds(start, width), :])

    if rem:
        scores(s_b, n_full * KV_CHUNK, rem)
    if n_full:
        scores(s_a, 0, KV_CHUNK)
    if rem:
        update(s_b, n_full * KV_CHUNK, rem, True)
    pairs = max((n_full - 1) // 2, 0)
    tail = [(c * KV_CHUNK, KV_CHUNK, False) for c in range(2 * pairs, n_full)]

    def pair(i, carry):
        c0 = pl.multiple_of(2 * i * KV_CHUNK, KV_CHUNK)
        scores(s_b, c0 + KV_CHUNK, KV_CHUNK)
        update(s_a, c0, KV_CHUNK, False)
        scores(s_a, c0 + 2 * KV_CHUNK, KV_CHUNK)
        update(s_b, c0 + KV_CHUNK, KV_CHUNK, False)
        return carry

    lax.fori_loop(0, pairs, pair, 0)
    bufs = (s_a, s_b)
    for n, (start, width, mask) in enumerate(tail):
        if n + 1 < len(tail):
            scores(bufs[(n + 1) % 2], tail[n + 1][0], tail[n + 1][1])
        update(bufs[n % 2], start, width, mask)
    o_ref[0, 0] = acc_s[...] / l_s[...]


def _attention(q, kt, v, *, batch, padded_len, seq_len):
    tq = ATTN_Q_TILE
    n_full = seq_len // KV_CHUNK
    rem = -(-(seq_len - n_full * KV_CHUNK) // LANES) * LANES
    q_tile = pl.BlockSpec((1, 1, tq, HEAD_PAD), lambda b, h, i: (h, b, i, 0))
    return pl.pallas_call(
        functools.partial(_attn_body, seq_len=seq_len, n_full=n_full, rem=rem),
        grid=(batch, N_HEADS, padded_len // tq),
        in_specs=[q_tile,
                  pl.BlockSpec((1, 1, HEAD_PAD, padded_len), lambda b, h, i: (h, b, 0, 0)),
                  pl.BlockSpec((1, 1, padded_len, HEAD_PAD), lambda b, h, i: (h, b, 0, 0))],
        out_specs=q_tile,
        out_shape=jax.ShapeDtypeStruct((N_HEADS, batch, padded_len, HEAD_PAD), F32),
        scratch_shapes=[pltpu.VMEM((tq, KV_CHUNK), F32), pltpu.VMEM((tq, KV_CHUNK), F32),
                        pltpu.VMEM((tq, 1), F32), pltpu.VMEM((tq, 1), F32), pltpu.VMEM((tq, HEAD_PAD), F32)],
        compiler_params=_cparams("parallel", "parallel", "arbitrary"),
        name="attention",
    )(q, kt, v)


def _outproj_body(hf_ref, hb_ref, gate_ref, o_ref, h_ref, grg_ref, gat_ref, wrg_ref, wat_ref, g2_ref,
                  wqt_ref, sk_ref, hn_ref, xn2_ref, sc_ref):
    rg = (hf_ref[...] + hb_ref[...]) * _gelu_tanh(gate_ref[...])
    rgn = _rms(rg, grg_ref[...])
    o = jnp.concatenate([o_ref[hd] for hd in range(N_HEADS)], axis=1)
    ms = jnp.sum(o * o, axis=-1, keepdims=True) * (1.0 / (N_HEADS * V_DIM))
    attn = o * lax.rsqrt(ms + EPS) * gat_ref[...]
    hn = h_ref[...] + _dot(rgn.astype(BF16), wrg_ref[...]) + _dot(attn.astype(BF16), wat_ref[...])
    hn_ref[...] = hn
    xn2 = _rms(hn, g2_ref[...]).astype(BF16)
    xn2_ref[...] = xn2
    qpt = _dot_nt(wqt_ref[...], xn2)
    for h in range(PEER_HEADS):
        for s in range(2):
            r0 = (h * 2 + s) * PEER_HALF
            blk = qpt[r0:r0 + PEER_HALF, :].astype(BF16)
            o0 = s * PEER_HEADS * N_KEYS + h * N_KEYS
            sc_ref[o0:o0 + N_KEYS, :] = _dot(sk_ref[h * 2 + s], blk)


def _outproj(hf, hb, gate, o, h, w, *, rows):
    tm = ROW_TILE
    row = lambda n: pl.BlockSpec((tm, n), lambda i: (i, 0))
    wide = N_HEADS * HEAD_PAD
    nsc = 2 * PEER_HEADS * N_KEYS
    return pl.pallas_call(
        _outproj_body,
        grid=(rows // tm,),
        in_specs=[row(RG_WIDTH), row(RG_WIDTH), row(RG_WIDTH),
                  pl.BlockSpec((N_HEADS, tm, HEAD_PAD), lambda i: (0, i, 0)), row(D_MODEL),
                  _full((1, RG_WIDTH)), _full((1, wide)), _full((RG_WIDTH, D_MODEL)), _full((wide, D_MODEL)),
                  _full((1, D_MODEL)), _full((nsc, D_MODEL)), _full((2 * PEER_HEADS, N_KEYS, PEER_HALF))],
        out_specs=[row(D_MODEL), row(D_MODEL), pl.BlockSpec((nsc, tm), lambda i: (0, i))],
        out_shape=[jax.ShapeDtypeStruct((rows, D_MODEL), F32), jax.ShapeDtypeStruct((rows, D_MODEL), BF16),
                   jax.ShapeDtypeStruct((nsc, rows), F32)],
        compiler_params=_cparams("parallel"),
        name="outproj",
    )(hf, hb, gate, o, h, w["grg"], w["gat"], w["wout_rg"], w["wout_at"], w["g2"], w["wqt"], w["subk"])


def _sort_network(n):
    pairs, p = [], 1
    while p < n:
        k = p
        while k >= 1:
            for j in range(k % p, n - k, 2 * k):
                for i in range(min(k, n - j - k)):
                    if (i + j) // (2 * p) == (i + j + k) // (2 * p):
                        pairs.append((i + j, i + j + k))
            k //= 2
        p *= 2
    return pairs


def _top_rows(x, count):
    n = x.shape[0] // SUBLANES
    cols = [x[k * SUBLANES:(k + 1) * SUBLANES, :] for k in range(n)]
    for a, b in _sort_network(n):
        cols[a], cols[b] = jnp.maximum(cols[a], cols[b]), jnp.minimum(cols[a], cols[b])
    cols.append(jnp.full_like(cols[0], NEG_INF))
    rows = []
    for r in range(count):
        m = jnp.max(cols[0], axis=0, keepdims=True)
        rows.append(m)
        depth = min(n, count - 1 - r)
        hit = cols[0] == m
        for k in range(depth):
            cols[k] = jnp.where(hit, cols[k + 1], cols[k])
    return rows


def _route_body(sc_ref, th_ref, e1_ref, e2_ref):
    t = sc_ref.shape[1]
    half = PEER_HEADS * N_KEYS
    jrow = lax.broadcasted_iota(jnp.int32, (SUBLANES, t), 0)

    def bc(row):
        return jnp.broadcast_to(row, (SUBLANES, t))

    def head(h, carry):
        r0 = pl.multiple_of(h * N_KEYS, N_KEYS)
        s1 = sc_ref[pl.ds(r0, N_KEYS), :]
        s2 = sc_ref[pl.ds(half + r0, N_KEYS), :]
        a = _top_rows(s1, PEER_TOPK + 1)
        b = _top_rows(s2, PEER_TOPK + 1)
        b_lo = jnp.concatenate(b[:SUBLANES], axis=0)
        b_hi = jnp.concatenate(b[SUBLANES:2 * SUBLANES], axis=0)
        a_hi = jnp.concatenate(a[SUBLANES:2 * SUBLANES], axis=0)
        pieces = [bc(a[0]) + b_lo, bc(a[0]) + b_hi, bc(a[1]) + b_lo]
        for i in range(2, SUBLANES):
            pieces.append(jnp.where(jrow < (PEER_TOPK + 1) // (i + 1), bc(a[i]) + b_lo, NEG_INF))
        pieces.append(a_hi + bc(b[0]))
        ends = jnp.concatenate([a[0] + b[PEER_TOPK], a[PEER_TOPK] + b[0]]
                               + [jnp.full((1, t), NEG_INF, F32)] * (SUBLANES - 2), axis=0)
        pieces.append(ends)
        cand = jnp.concatenate(pieces, axis=0)
        x = cand
        for _ in range(PEER_TOPK - 1):
            x = jnp.where(x == jnp.max(x, axis=0, keepdims=True), NEG_INF, x)
        c16 = jnp.max(x, axis=0, keepdims=True)
        c17 = jnp.max(jnp.where(x == c16, NEG_INF, x), axis=0, keepdims=True)
        thr = 0.5 * (c16 + c17)
        top = a[0] + b[0]
        z = jnp.sum(jnp.where(cand >= thr, jnp.exp(cand - top), 0.0), axis=0, keepdims=True)
        th_ref[pl.ds(r0, N_KEYS), :] = jnp.exp((thr - s1) - b[0])
        e1_ref[pl.ds(r0, N_KEYS), :] = jnp.exp(s1 - a[0]) / z
        e2_ref[pl.ds(r0, N_KEYS), :] = jnp.exp(s2 - b[0])
        return carry

    lax.fori_loop(0, PEER_HEADS, head, 0)


def _route(sc, *, rows):
    t = ROUTE_TOKENS
    half = PEER_HEADS * N_KEYS
    out = pl.BlockSpec((half, t), lambda i: (0, i))
    return pl.pallas_call(
        _route_body,
        grid=(rows // t,),
        in_specs=[pl.BlockSpec((2 * half, t), lambda i: (0, i))],
        out_specs=[out, out, out],
        out_shape=[jax.ShapeDtypeStruct((half, rows), F32)] * 3,
        compiler_params=_cparams("parallel"),
        name="peer_route",
    )(sc)


def _peer_body(xn_ref, h_ref, th_ref, e1_ref, e2_ref, u_ref, vt_ref, out_ref, acc_s, hid_s, w_s):
    j = pl.program_id(1)
    tt = xn_ref.shape[0]

    @pl.when(j == 0)
    def _():
        acc_s[...] = jnp.zeros_like(acc_s)

    hid_s[...] = _dot_nt(u_ref[...], xn_ref[...])

    i1_base = pl.multiple_of(j * PEER_I1_BLOCK, PEER_I1_BLOCK)
    for il in range(PEER_I1_BLOCK):
        es = slice(il * N_KEYS, (il + 1) * N_KEYS)
        for lt in range(tt // LANES):
            ln = slice(lt * LANES, (lt + 1) * LANES)
            g = jnp.zeros((N_KEYS, LANES), F32)
            for h in range(PEER_HEADS):
                hs = slice(h * N_KEYS, (h + 1) * N_KEYS)
                th = th_ref[pl.ds(h * N_KEYS + i1_base, PEER_I1_BLOCK), ln][il:il + 1, :]
                e1 = e1_ref[pl.ds(h * N_KEYS + i1_base, PEER_I1_BLOCK), ln][il:il + 1, :]
                e2 = e2_ref[hs, ln]
                g = g + jnp.where(e2 >= th, e2, 0.0) * e1
            w_s[es, ln] = (g * _gelu_tanh(hid_s[es, ln])).astype(BF16)
    acc_s[...] += _dot(vt_ref[...], w_s[...])

    @pl.when(j == pl.num_programs(1) - 1)
    def _():
        out_ref[...] = h_ref[...] + acc_s[...].T


def _peer(xn2, hn, th, e1, e2, u, vt, *, rows):
    tt = PEER_TOKENS
    eb = PEER_I1_BLOCK * N_KEYS
    half = PEER_HEADS * N_KEYS
    tok = pl.BlockSpec((tt, D_MODEL), lambda i, j: (i, 0))
    tab = pl.BlockSpec((half, tt), lambda i, j: (0, i))
    return pl.pallas_call(
        _peer_body,
        grid=(rows // tt, N_EXPERTS // eb),
        in_specs=[tok, tok, tab, tab, tab,
                  pl.BlockSpec((eb, D_MODEL), lambda i, j: (j, 0)),
                  pl.BlockSpec((D_MODEL, eb), lambda i, j: (0, j))],
        out_specs=pl.BlockSpec((tt, D_MODEL), lambda i, j: (i, 0)),
        out_shape=jax.ShapeDtypeStruct((rows, D_MODEL), F32),
        scratch_shapes=[pltpu.VMEM((D_MODEL, tt), F32), pltpu.VMEM((eb, tt), F32), pltpu.VMEM((eb, tt), BF16)],
        compiler_params=_cparams("parallel", "arbitrary"),
        name="peer_dense",
    )(xn2, hn, th, e1, e2, u, vt)


def _block_diag(w):
    eye = jnp.eye(RG_HEADS, dtype=w.dtype)
    return jnp.einsum("hij,hg->higj", w, eye).reshape(RG_WIDTH, RG_WIDTH)


def _head_pad(w, lo, hi):
    k = w.shape[0]
    part = w[:, :, lo:hi]
    return jnp.pad(part, ((0, 0), (0, 0), (0, HEAD_PAD - (hi - lo)))).reshape(k, N_HEADS * HEAD_PAD)


def _rope_partner(x1, x2):
    z64 = jnp.zeros(x1.shape[:-1] + (QK_NOPE,), x1.dtype)
    z32 = jnp.zeros(x1.shape[:-1] + (HEAD_PAD - QK_DIM,), x1.dtype)
    return jnp.concatenate([z64, x2, x1, z32], axis=-1)


def _layer_weights(i, norm1_g, w_in, conv_w, conv_b, rg_wa, rg_ba, rg_wi, rg_bi, rg_lambda, q_norm_g, w_uq,
                   kv_norm_g, w_ukv, q_head_g, k_head_g, out_g_rg, out_g_attn, w_out, norm2_g, peer_wq,
                   peer_subkeys, peer_u, peer_v):
    half = QK_ROPE // 2
    wq3 = w_uq[i].reshape(Q_LORA, N_HEADS, QK_DIM)
    wkv3 = w_ukv[i].reshape(KV_LORA, N_HEADS, QK_NOPE + V_DIM)
    eye = jnp.eye(QK_ROPE, dtype=F32)
    ek = jnp.concatenate([jnp.zeros((QK_ROPE, QK_NOPE), F32), eye,
                          jnp.zeros((QK_ROPE, HEAD_PAD - QK_DIM), F32)], axis=-1)
    eks = _rope_partner(eye[:, :half], eye[:, half:])
    place = lambda e: jnp.pad(jnp.tile(e, (1, N_HEADS)), ((0, LANES - QK_ROPE), (0, 0))).astype(BF16)

    def head_gain(g):
        main = jnp.pad(g, (0, HEAD_PAD - QK_DIM)).reshape(1, HEAD_PAD)
        swap = _rope_partner(g[QK_NOPE:QK_NOPE + half], g[QK_NOPE + half:]).reshape(1, HEAD_PAD)
        return main, swap

    gq, gqs = head_gain(q_head_g[i])
    gk, gks = head_gain(k_head_g[i])
    wout = w_out[i]
    wout_at = jnp.pad(wout[RG_WIDTH:].reshape(N_HEADS, V_DIM, D_MODEL),
                      ((0, 0), (0, HEAD_PAD - V_DIM), (0, 0))).reshape(N_HEADS * HEAD_PAD, D_MODEL)
    gat = jnp.pad(out_g_attn[i].reshape(N_HEADS, V_DIM), ((0, 0), (0, HEAD_PAD - V_DIM))).reshape(1, -1)
    return dict(
        g1=norm1_g[i].reshape(1, -1),
        win=jnp.pad(w_in[i], ((0, 0), (0, P_IN_PAD - P_IN))).astype(BF16),
        qg=q_norm_g[i].reshape(1, -1),
        wq=_head_pad(wq3, 0, QK_DIM).astype(BF16),
        wqs=_rope_partner(wq3[:, :, QK_NOPE:QK_NOPE + half], wq3[:, :, QK_NOPE + half:])
        .reshape(Q_LORA, -1).astype(BF16),
        kvg=kv_norm_g[i].reshape(1, -1),
        wk=_head_pad(wkv3, 0, QK_NOPE).astype(BF16),
        wv=_head_pad(wkv3, QK_NOPE, QK_NOPE + V_DIM).astype(BF16),
        ek=place(ek), eks=place(eks), gq=gq, gqs=gqs, gk=gk, gks=gks,
        conv_w=conv_w[i], conv_b=conv_b[i].reshape(1, -1),
        wgate_f=jnp.concatenate([_block_diag(rg_wa[i, 0]), _block_diag(rg_wi[i, 0])], axis=1).astype(BF16),
        wgate_b=jnp.concatenate([_block_diag(rg_wa[i, 1]), _block_diag(rg_wi[i, 1])], axis=1).astype(BF16),
        gate_bias=jnp.stack([rg_ba[i, 0].reshape(-1), rg_bi[i, 0].reshape(-1),
                             rg_ba[i, 1].reshape(-1), rg_bi[i, 1].reshape(-1)]),
        lam=rg_lambda[i],
        grg=out_g_rg[i].reshape(1, -1), gat=gat,
        wout_rg=wout[:RG_WIDTH].astype(BF16), wout_at=wout_at.astype(BF16),
        g2=norm2_g[i].reshape(1, -1),
        wqt=peer_wq[i].T.astype(BF16),
        subk=peer_subkeys[i].reshape(2 * PEER_HEADS, N_KEYS, PEER_HALF).astype(BF16),
        u=peer_u[i].astype(BF16),
        vt=peer_v[i].T.astype(BF16),
    )


def _rope_tables(length):
    pos = jnp.arange(length, dtype=F32)
    inv = ROPE_THETA ** (-jnp.arange(0, QK_ROPE, 2, dtype=F32) / QK_ROPE)
    ang = pos[:, None] * inv[None, :]
    cos, sin = jnp.cos(ang), jnp.sin(ang)
    ones = jnp.ones((length, QK_NOPE), F32)
    zpad = jnp.zeros((length, HEAD_PAD - QK_DIM), F32)
    cos_t = jnp.concatenate([ones, cos, cos, zpad], axis=-1)
    sin_t = jnp.concatenate([jnp.zeros((length, QK_NOPE), F32), -sin, sin, zpad], axis=-1)
    return cos_t, sin_t


def _padded_len(batch, length):
    lp = -(-length // ATTN_Q_TILE) * ATTN_Q_TILE
    row_tile = math.lcm(ROW_TILE, PEER_TOKENS, ROUTE_TOKENS)
    while (batch * lp) % row_tile:
        lp += ATTN_Q_TILE
    return lp


def _encode(x, meta_tokens, layers):
    batch, n_tok, _ = x.shape
    seq_len = n_tok + N_META
    lp = _padded_len(batch, seq_len)
    rows = batch * lp
    meta = jnp.broadcast_to(meta_tokens[None].astype(x.dtype), (batch, N_META, D_MODEL))
    h = jnp.concatenate([meta, x, jnp.zeros((batch, lp - seq_len, D_MODEL), x.dtype)], axis=1)
    h = h.reshape(rows, D_MODEL)
    cos_t, sin_t = (jnp.tile(t, (batch, 1)) for t in _rope_tables(lp))
    for w in layers:
        xr, gate, q, k, v = _inproj(h, cos_t, sin_t, w, rows=rows)
        hf, hb = _scan(xr, w, batch=batch, padded_len=lp, seq_len=seq_len)
        per_seq = (N_HEADS, batch, lp, HEAD_PAD)
        kt = k.reshape(per_seq).transpose(0, 1, 3, 2)
        o = _attention(q.reshape(per_seq), kt, v.reshape(per_seq), batch=batch, padded_len=lp, seq_len=seq_len)
        hn, xn2, sc = _outproj(hf, hb, gate, o.reshape(N_HEADS, rows, HEAD_PAD), h, w, rows=rows)
        th, e1, e2 = _route(sc, rows=rows)
        h = _peer(xn2, hn, th, e1, e2, w["u"], w["vt"], rows=rows)
    return h.reshape(batch, lp, D_MODEL)[:, N_META:seq_len]


def kernel(x_prompt, x_sample, meta_tokens, norm1_g, w_in, conv_w, conv_b, rg_wa, rg_ba, rg_wi, rg_bi, rg_lambda, q_norm_g, w_uq, kv_norm_g, w_ukv, q_head_g, k_head_g, out_g_rg, out_g_attn, w_out, norm2_g, peer_wq, peer_subkeys, peer_u, peer_v):
    params = (norm1_g, w_in, conv_w, conv_b, rg_wa, rg_ba, rg_wi, rg_bi, rg_lambda, q_norm_g, w_uq, kv_norm_g,
              w_ukv, q_head_g, k_head_g, out_g_rg, out_g_attn, w_out, norm2_g, peer_wq, peer_subkeys, peer_u,
              peer_v)
    layers = [_layer_weights(i, *params) for i in range(norm1_g.shape[0])]
    y_prompt = _encode(x_prompt, meta_tokens, layers)
    y_sample = _encode(x_sample, meta_tokens, layers)
    return (y_prompt, y_sample)
```

```python
import functools
import math

import jax
import jax.numpy as jnp
from jax import lax
from jax.experimental import pallas as pl
from jax.experimental.pallas import tpu as pltpu

F32 = jnp.float32
BF16 = jnp.bfloat16

D_MODEL = 1024
N_META = 16
EPS = 1e-6
RG_WIDTH = 512
RG_HEADS = 8
RG_BLOCK = 64
CONV_W = 4
RG_C = 8.0
N_HEADS = 8
QK_NOPE = 64
QK_ROPE = 32
QK_DIM = 96
V_DIM = 64
Q_LORA = 256
KV_LORA = 128
ROPE_THETA = 10000.0
P_IN = 2 * RG_WIDTH + Q_LORA + KV_LORA + QK_ROPE
PEER_HEADS = 8
N_KEYS = 128
N_EXPERTS = N_KEYS * N_KEYS
PEER_TOPK = 16
PEER_HALF = 128

LANES = 128
SUBLANES = 8
HEAD_PAD = LANES
P_IN_PAD = 12 * LANES
ROW_TILE = 512
ATTN_Q_TILE = 1056
SCAN_CHUNK = ATTN_Q_TILE // 4
KV_CHUNK = 1024
PEER_TOKENS = 512
ROUTE_TOKENS = 256
PEER_I1_BLOCK = 16
VMEM_LIMIT = 52 * 1024 * 1024
NEG_INF = float("-inf")
MASK_VALUE = -1e30
LOG2E = 1.4426950408889634


def _cparams(*sem):
    return pltpu.CompilerParams(dimension_semantics=sem, vmem_limit_bytes=VMEM_LIMIT)


def _rms(x, g):
    return x * lax.rsqrt(jnp.mean(x * x, axis=-1, keepdims=True) + EPS) * g


def _gelu_tanh(x):
    return x * (0.5 * (1.0 + jnp.tanh(0.7978845608028654 * (x + 0.044715 * (x * x * x)))))


def _sigmoid(x):
    return 1.0 / (1.0 + jnp.exp(-x))


def _neg_expm1(y):
    u = jnp.exp(y)
    um1 = u - 1.0
    tiny = um1 == 0.0
    r = jnp.where(tiny, y, um1 * y / jnp.where(tiny, 1.0, jnp.log(u)))
    return -jnp.where(um1 == -1.0, -1.0, r)


def _dot(a, b):
    return jnp.dot(a, b, preferred_element_type=F32)


def _dot_nt(a, b):
    return lax.dot_general(a, b, (((1,), (1,)), ((), ())), preferred_element_type=F32)


def _full(shape):
    n = len(shape)
    return pl.BlockSpec(shape, lambda *_: (0,) * n)


def _inproj_body(x_ref, cos_ref, sin_ref, g1_ref, win_ref, qg_ref, wq_ref, wqs_ref, kvg_ref,
                 wk_ref, ek_ref, eks_ref, wv_ref, gq_ref, gqs_ref, gk_ref, gks_ref,
                 xr_ref, gate_ref, q_ref, k_ref, v_ref):
    xn = _rms(x_ref[...], g1_ref[...])
    p = _dot(xn.astype(BF16), win_ref[...])
    xr_ref[...] = p[:, :RG_WIDTH]
    gate_ref[...] = p[:, RG_WIDTH:2 * RG_WIDTH]
    o2 = 2 * RG_WIDTH
    qcn = _rms(p[:, o2:o2 + Q_LORA], qg_ref[...]).astype(BF16)
    kvn = _rms(p[:, o2 + Q_LORA:o2 + Q_LORA + KV_LORA], kvg_ref[...]).astype(BF16)
    kpe = p[:, o2 + Q_LORA + KV_LORA:].astype(BF16)
    q = _dot(qcn, wq_ref[...])
    qs = _dot(qcn, wqs_ref[...])
    k = _dot(kvn, wk_ref[...]) + _dot(kpe, ek_ref[...])
    ks = _dot(kpe, eks_ref[...])
    v = _dot(kvn, wv_ref[...]).astype(BF16)
    c = cos_ref[...]
    s = sin_ref[...]
    qscale = QK_DIM ** -0.5 * LOG2E
    for h in range(N_HEADS):
        sl = slice(h * HEAD_PAD, (h + 1) * HEAD_PAD)
        qh = q[:, sl]
        rq = lax.rsqrt(jnp.sum(qh * qh, axis=-1, keepdims=True) * (1.0 / QK_DIM) + EPS)
        q_ref[h] = ((qh * gq_ref[...] * c + qs[:, sl] * gqs_ref[...] * s) * (rq * qscale)).astype(BF16)
        kh = k[:, sl]
        rk = lax.rsqrt(jnp.sum(kh * kh, axis=-1, keepdims=True) * (1.0 / QK_DIM) + EPS)
        k_ref[h] = ((kh * gk_ref[...] * c + ks[:, sl] * gks_ref[...] * s) * rk).astype(BF16)
        v_ref[h] = v[:, sl]


def _inproj(h, cos_t, sin_t, w, *, rows):
    tm = ROW_TILE
    row = lambda n: pl.BlockSpec((tm, n), lambda i: (i, 0))
    pos = row(HEAD_PAD)
    wide = N_HEADS * HEAD_PAD
    heads = pl.BlockSpec((N_HEADS, tm, HEAD_PAD), lambda i: (0, i, 0))
    heads_shape = jax.ShapeDtypeStruct((N_HEADS, rows, HEAD_PAD), BF16)
    return pl.pallas_call(
        _inproj_body,
        grid=(rows // tm,),
        in_specs=[row(D_MODEL), pos, pos, _full((1, D_MODEL)), _full((D_MODEL, P_IN_PAD)),
                  _full((1, Q_LORA)), _full((Q_LORA, wide)), _full((Q_LORA, wide)),
                  _full((1, KV_LORA)), _full((KV_LORA, wide)), _full((LANES, wide)), _full((LANES, wide)),
                  _full((KV_LORA, wide)),
                  _full((1, HEAD_PAD)), _full((1, HEAD_PAD)), _full((1, HEAD_PAD)), _full((1, HEAD_PAD))],
        out_specs=[row(RG_WIDTH), row(RG_WIDTH), heads, heads, heads],
        out_shape=[jax.ShapeDtypeStruct((rows, RG_WIDTH), F32), jax.ShapeDtypeStruct((rows, RG_WIDTH), F32),
                   heads_shape, heads_shape, heads_shape],
        compiler_params=_cparams("parallel"),
        name="inproj",
    )(h, cos_t, sin_t, w["g1"], w["win"], w["qg"], w["wq"], w["wqs"], w["kvg"], w["wk"], w["ek"], w["eks"],
      w["wv"], w["gq"], w["gqs"], w["gk"], w["gks"])


def _scan_body(xm_f, xp_f, xn_f, xm_b, xp_b, xn_b, cw_ref, cb_ref, wf_ref, wb_ref, bias_ref, lam_ref,
               hf_ref, hb_ref, ext_s, a_s, b_s, cf_s, cb_s, *, seq_len, chunk, n_chunks):
    i = pl.program_id(1)
    C = chunk
    groups = C // SUBLANES

    @pl.when(i == 0)
    def _():
        cf_s[...] = jnp.zeros_like(cf_s)
        cb_s[...] = jnp.zeros_like(cb_s)

    rows8 = lax.broadcasted_iota(jnp.int32, (SUBLANES, RG_WIDTH), 0)
    rows_c = lax.broadcasted_iota(jnp.int32, (C, RG_WIDTH), 0)
    rmod = rows_c & (SUBLANES - 1)

    def masked(x, pos):
        return jnp.where((pos >= 0) & (pos < seq_len), x, 0.0)

    def affine_terms(xm, xp, xn, j, w_ref, d):
        base = j * C
        ext_s[0:SUBLANES, :] = masked(xp[...], base - SUBLANES + rows8)
        ext_s[SUBLANES:SUBLANES + C, :] = masked(xm[...], base + rows_c)
        ext_s[SUBLANES + C:, :] = masked(xn[...], base + C + rows8)
        xc = cb_ref[...]
        for t in range(CONV_W):
            xc = xc + cw_ref[t:t + 1, :] * ext_s[SUBLANES - 2 + t:SUBLANES - 2 + t + C, :]
        gm = _dot(xc.astype(BF16), w_ref[...])
        r = _sigmoid(gm[:, :RG_WIDTH] + bias_ref[2 * d:2 * d + 1, :])
        ig = _sigmoid(gm[:, RG_WIDTH:] + bias_ref[2 * d + 1:2 * d + 2, :])
        z = -lam_ref[d:d + 1, :]
        softplus = jnp.maximum(z, 0.0) + jnp.log1p(jnp.exp(-jnp.abs(z)))
        log_a = (-RG_C) * r * softplus
        a = jnp.exp(log_a)
        b = jnp.sqrt(_neg_expm1(2.0 * log_a)) * (ig * xc)
        b = jnp.where(base + rows_c < seq_len, b, 0.0)
        return a, b

    def rotate_in_groups(x, shift):
        return pltpu.roll(x.reshape(groups, SUBLANES, RG_WIDTH), shift, 1).reshape(C, RG_WIDTH)

    a, b = affine_terms(xm_f, xp_f, xn_f, i, wf_ref, 0)
    for d in (1, 2, 4):
        m = rmod >= d
        a_sh = rotate_in_groups(a, d)
        b_sh = rotate_in_groups(b, d)
        b = jnp.where(m, a * b_sh + b, b)
        a = jnp.where(m, a * a_sh, a)
    a_s[...] = a
    b_s[...] = b

    def fwd_group(g, carry):
        r0 = pl.multiple_of(g * SUBLANES, SUBLANES)
        hrow = a_s[pl.ds(r0, SUBLANES), :] * carry + b_s[pl.ds(r0, SUBLANES), :]
        hf_ref[pl.ds(r0, SUBLANES), :] = hrow
        return jnp.broadcast_to(hrow[SUBLANES - 1:SUBLANES, :], (SUBLANES, RG_WIDTH))

    cf_s[...] = lax.fori_loop(0, groups, fwd_group, cf_s[...])

    a, b = affine_terms(xm_b, xp_b, xn_b, n_chunks - 1 - i, wb_ref, 1)
    for d in (1, 2, 4):
        m = rmod < SUBLANES - d
        a_sh = rotate_in_groups(a, SUBLANES - d)
        b_sh = rotate_in_groups(b, SUBLANES - d)
        b = jnp.where(m, a * b_sh + b, b)
        a = jnp.where(m, a * a_sh, a)
    a_s[...] = a
    b_s[...] = b

    def bwd_group(g, carry):
        r0 = pl.multiple_of((groups - 1 - g) * SUBLANES, SUBLANES)
        hrow = a_s[pl.ds(r0, SUBLANES), :] * carry + b_s[pl.ds(r0, SUBLANES), :]
        hb_ref[pl.ds(r0, SUBLANES), :] = hrow
        return jnp.broadcast_to(hrow[0:1, :], (SUBLANES, RG_WIDTH))

    cb_s[...] = lax.fori_loop(0, groups, bwd_group, cb_s[...])


def _scan(xr, w, *, batch, padded_len, seq_len):
    C = SCAN_CHUNK
    n = padded_len // C
    per8 = C // SUBLANES
    last8 = batch * padded_len // SUBLANES - 1

    def chunk_f(b, i):
        return b * n + i

    def chunk_b(b, i):
        return b * n + (n - 1 - i)

    def specs(chunk_of):
        main = pl.BlockSpec((C, RG_WIDTH), lambda b, i: (chunk_of(b, i), 0))
        prev = pl.BlockSpec((SUBLANES, RG_WIDTH), lambda b, i: (jnp.maximum(chunk_of(b, i) * per8 - 1, 0), 0))
        nxt = pl.BlockSpec((SUBLANES, RG_WIDTH),
                           lambda b, i: (jnp.minimum((chunk_of(b, i) + 1) * per8, last8), 0))
        return [main, prev, nxt]

    rows = batch * padded_len
    return pl.pallas_call(
        functools.partial(_scan_body, seq_len=seq_len, chunk=C, n_chunks=n),
        grid=(batch, n),
        in_specs=specs(chunk_f) + specs(chunk_b) + [
            _full((CONV_W, RG_WIDTH)), _full((1, RG_WIDTH)),
            _full((RG_WIDTH, 2 * RG_WIDTH)), _full((RG_WIDTH, 2 * RG_WIDTH)),
            _full((4, RG_WIDTH)), _full((2, RG_WIDTH))],
        out_specs=[pl.BlockSpec((C, RG_WIDTH), lambda b, i: (chunk_f(b, i), 0)),
                   pl.BlockSpec((C, RG_WIDTH), lambda b, i: (chunk_b(b, i), 0))],
        out_shape=[jax.ShapeDtypeStruct((rows, RG_WIDTH), F32), jax.ShapeDtypeStruct((rows, RG_WIDTH), F32)],
        scratch_shapes=[pltpu.VMEM((C + 2 * SUBLANES, RG_WIDTH), F32), pltpu.VMEM((C, RG_WIDTH), F32),
                        pltpu.VMEM((C, RG_WIDTH), F32), pltpu.VMEM((SUBLANES, RG_WIDTH), F32),
                        pltpu.VMEM((SUBLANES, RG_WIDTH), F32)],
        compiler_params=_cparams("arbitrary", "arbitrary"),
        name="rglru_scan",
    )(xr, xr, xr, xr, xr, xr, w["conv_w"], w["conv_b"], w["wgate_f"], w["wgate_b"], w["gate_bias"], w["lam"])


def _attn_body(q_ref, kt_ref, v_ref, o_ref, s_a, s_b, m_s, l_s, acc_s, *, seq_len, n_full, rem):
    tq = q_ref.shape[2]
    m_s[...] = jnp.full_like(m_s, NEG_INF)
    l_s[...] = jnp.zeros_like(l_s)
    acc_s[...] = jnp.zeros_like(acc_s)

    def scores(dst, start, width):
        dst[:, :width] = _dot(q_ref[0, 0], kt_ref[0, 0, :, pl.ds(start, width)])

    def update(src, start, width, mask):
        s = src[:, :width]
        if mask:
            kpos = start + lax.broadcasted_iota(jnp.int32, (tq, width), 1)
            s = jnp.where(kpos < seq_len, s, MASK_VALUE)
        m = m_s[...]
        m_new = jnp.maximum(m, jnp.max(s, axis=-1, keepdims=True))
        alpha = jnp.exp2(m - m_new)
        p = jnp.exp2(s - m_new)
        m_s[...] = m_new
        l_s[...] = alpha * l_s[...] + jnp.sum(p, axis=-1, keepdims=True)
        acc_s[...] = alpha * acc_s[...] + _dot(p.astype(BF16), v_ref[0, 0, pl.ds(start, width), :])

    if rem:
        scores(s_b, n_full * KV_CHUNK, rem)
    if n_full:
        scores(s_a, 0, KV_CHUNK)
    if rem:
        update(s_b, n_full * KV_CHUNK, rem, True)
    pairs = max((n_full - 1) // 2, 0)
    tail = [(c * KV_CHUNK, KV_CHUNK, False) for c in range(2 * pairs, n_full)]

    def pair(i, carry):
        c0 = pl.multiple_of(2 * i * KV_CHUNK, KV_CHUNK)
        scores(s_b, c0 + KV_CHUNK, KV_CHUNK)
        update(s_a, c0, KV_CHUNK, False)
        scores(s_a, c0 + 2 * KV_CHUNK, KV_CHUNK)
        update(s_b, c0 + KV_CHUNK, KV_CHUNK, False)
        return carry

    lax.fori_loop(0, pairs, pair, 0)
    bufs = (s_a, s_b)
    for n, (start, width, mask) in enumerate(tail):
        if n + 1 < len(tail):
            scores(bufs[(n + 1) % 2], tail[n + 1][0], tail[n + 1][1])
        update(bufs[n % 2], start, width, mask)
    o_ref[0, 0] = acc_s[...] / l_s[...]


def _attention(q, kt, v, *, batch, padded_len, seq_len):
    kv_bytes = 4 * padded_len * HEAD_PAD * 2
    tq = ATTN_Q_TILE
    if padded_len % (2 * tq) == 0 and 2 * tq * KV_CHUNK * 10 + kv_bytes <= VMEM_LIMIT * 2 // 3:
        tq = 2 * tq
    n_full = seq_len // KV_CHUNK
    rem = -(-(seq_len - n_full * KV_CHUNK) // LANES) * LANES
    q_tile = pl.BlockSpec((1, 1, tq, HEAD_PAD), lambda b, h, i: (h, b, i, 0))
    return pl.pallas_call(
        functools.partial(_attn_body, seq_len=seq_len, n_full=n_full, rem=rem),
        grid=(batch, N_HEADS, padded_len // tq),
        in_specs=[q_tile,
                  pl.BlockSpec((1, 1, HEAD_PAD, padded_len), lambda b, h, i: (h, b, 0, 0)),
                  pl.BlockSpec((1, 1, padded_len, HEAD_PAD), lambda b, h, i: (h, b, 0, 0))],
        out_specs=q_tile,
        out_shape=jax.ShapeDtypeStruct((N_HEADS, batch, padded_len, HEAD_PAD), F32),
        scratch_shapes=[pltpu.VMEM((tq, KV_CHUNK), F32), pltpu.VMEM((tq, KV_CHUNK), F32),
                        pltpu.VMEM((tq, 1), F32), pltpu.VMEM((tq, 1), F32), pltpu.VMEM((tq, HEAD_PAD), F32)],
        compiler_params=_cparams("parallel", "parallel", "arbitrary"),
        name="attention",
    )(q, kt, v)


def _outproj_body(hf_ref, hb_ref, gate_ref, o_ref, h_ref, grg_ref, gat_ref, wrg_ref, wat_ref, g2_ref,
                  wqt_ref, sk_ref, hn_ref, xn2_ref, sc_ref):
    rg = (hf_ref[...] + hb_ref[...]) * _gelu_tanh(gate_ref[...])
    rgn = _rms(rg, grg_ref[...])
    o = jnp.concatenate([o_ref[hd] for hd in range(N_HEADS)], axis=1)
    ms = jnp.sum(o * o, axis=-1, keepdims=True) * (1.0 / (N_HEADS * V_DIM))
    attn = o * lax.rsqrt(ms + EPS) * gat_ref[...]
    hn = h_ref[...] + _dot(rgn.astype(BF16), wrg_ref[...]) + _dot(attn.astype(BF16), wat_ref[...])
    hn_ref[...] = hn
    xn2 = _rms(hn, g2_ref[...]).astype(BF16)
    xn2_ref[...] = xn2
    qpt = _dot_nt(wqt_ref[...], xn2)
    for h in range(PEER_HEADS):
        for s in range(2):
            r0 = (h * 2 + s) * PEER_HALF
            blk = qpt[r0:r0 + PEER_HALF, :].astype(BF16)
            o0 = s * PEER_HEADS * N_KEYS + h * N_KEYS
            sc_ref[o0:o0 + N_KEYS, :] = _dot(sk_ref[h * 2 + s], blk)


def _outproj(hf, hb, gate, o, h, w, *, rows):
    tm = ROW_TILE
    row = lambda n: pl.BlockSpec((tm, n), lambda i: (i, 0))
    wide = N_HEADS * HEAD_PAD
    nsc = 2 * PEER_HEADS * N_KEYS
    return pl.pallas_call(
        _outproj_body,
        grid=(rows // tm,),
        in_specs=[row(RG_WIDTH), row(RG_WIDTH), row(RG_WIDTH),
                  pl.BlockSpec((N_HEADS, tm, HEAD_PAD), lambda i: (0, i, 0)), row(D_MODEL),
                  _full((1, RG_WIDTH)), _full((1, wide)), _full((RG_WIDTH, D_MODEL)), _full((wide, D_MODEL)),
                  _full((1, D_MODEL)), _full((nsc, D_MODEL)), _full((2 * PEER_HEADS, N_KEYS, PEER_HALF))],
        out_specs=[row(D_MODEL), row(D_MODEL), pl.BlockSpec((nsc, tm), lambda i: (0, i))],
        out_shape=[jax.ShapeDtypeStruct((rows, D_MODEL), F32), jax.ShapeDtypeStruct((rows, D_MODEL), BF16),
                   jax.ShapeDtypeStruct((nsc, rows), F32)],
        compiler_params=_cparams("parallel"),
        name="outproj",
    )(hf, hb, gate, o, h, w["grg"], w["gat"], w["wout_rg"], w["wout_at"], w["g2"], w["wqt"], w["subk"])


def _sort_network(n):
    pairs, p = [], 1
    while p < n:
        k = p
        while k >= 1:
            for j in range(k % p, n - k, 2 * k):
                for i in range(min(k, n - j - k)):
                    if (i + j) // (2 * p) == (i + j + k) // (2 * p):
                        pairs.append((i + j, i + j + k))
            k //= 2
        p *= 2
    return pairs


def _top_rows(x, count):
    n = x.shape[0] // SUBLANES
    cols = [x[k * SUBLANES:(k + 1) * SUBLANES, :] for k in range(n)]
    for a, b in _sort_network(n):
        cols[a], cols[b] = jnp.maximum(cols[a], cols[b]), jnp.minimum(cols[a], cols[b])
    cols.append(jnp.full_like(cols[0], NEG_INF))
    rows = []
    for r in range(count):
        m = jnp.max(cols[0], axis=0, keepdims=True)
        rows.append(m)
        depth = min(n, count - 1 - r)
        hit = cols[0] == m
        for k in range(depth):
            cols[k] = jnp.where(hit, cols[k + 1], cols[k])
    return rows


def _route_body(sc_ref, th_ref, e1_ref, e2_ref):
    t = sc_ref.shape[1]
    half = PEER_HEADS * N_KEYS
    jrow = lax.broadcasted_iota(jnp.int32, (SUBLANES, t), 0)

    def bc(row):
        return jnp.broadcast_to(row, (SUBLANES, t))

    def head(h, carry):
        r0 = pl.multiple_of(h * N_KEYS, N_KEYS)
        s1 = sc_ref[pl.ds(r0, N_KEYS), :]
        s2 = sc_ref[pl.ds(half + r0, N_KEYS), :]
        a = _top_rows(s1, PEER_TOPK + 1)
        b = _top_rows(s2, PEER_TOPK + 1)
        b_lo = jnp.concatenate(b[:SUBLANES], axis=0)
        b_hi = jnp.concatenate(b[SUBLANES:2 * SUBLANES], axis=0)
        a_hi = jnp.concatenate(a[SUBLANES:2 * SUBLANES], axis=0)
        pieces = [bc(a[0]) + b_lo, bc(a[0]) + b_hi, bc(a[1]) + b_lo]
        for i in range(2, SUBLANES):
            pieces.append(jnp.where(jrow < (PEER_TOPK + 1) // (i + 1), bc(a[i]) + b_lo, NEG_INF))
        pieces.append(a_hi + bc(b[0]))
        ends = jnp.concatenate([a[0] + b[PEER_TOPK], a[PEER_TOPK] + b[0]]
                               + [jnp.full((1, t), NEG_INF, F32)] * (SUBLANES - 2), axis=0)
        pieces.append(ends)
        cand = jnp.concatenate(pieces, axis=0)
        x = cand
        for _ in range(PEER_TOPK - 1):
            x = jnp.where(x == jnp.max(x, axis=0, keepdims=True), NEG_INF, x)
        c16 = jnp.max(x, axis=0, keepdims=True)
        c17 = jnp.max(jnp.where(x == c16, NEG_INF, x), axis=0, keepdims=True)
        thr = 0.5 * (c16 + c17)
        top = a[0] + b[0]
        z = jnp.sum(jnp.where(cand >= thr, jnp.exp(cand - top), 0.0), axis=0, keepdims=True)
        th_ref[pl.ds(r0, N_KEYS), :] = jnp.exp((thr - s1) - b[0])
        e1_ref[pl.ds(r0, N_KEYS), :] = jnp.exp(s1 - a[0]) / z
        e2_ref[pl.ds(r0, N_KEYS), :] = jnp.exp(s2 - b[0])
        return carry

    lax.fori_loop(0, PEER_HEADS, head, 0)


def _route(sc, *, rows):
    t = ROUTE_TOKENS
    half = PEER_HEADS * N_KEYS
    out = pl.BlockSpec((half, t), lambda i: (0, i))
    return pl.pallas_call(
        _route_body,
        grid=(rows // t,),
        in_specs=[pl.BlockSpec((2 * half, t), lambda i: (0, i))],
        out_specs=[out, out, out],
        out_shape=[jax.ShapeDtypeStruct((half, rows), F32)] * 3,
        compiler_params=_cparams("parallel"),
        name="peer_route",
    )(sc)


def _peer_body(xn_ref, h_ref, th_ref, e1_ref, e2_ref, u_ref, vt_ref, out_ref, acc_s, hid_s, w_s):
    j = pl.program_id(1)
    tt = xn_ref.shape[0]

    @pl.when(j == 0)
    def _():
        acc_s[...] = jnp.zeros_like(acc_s)

    hid_s[...] = _dot_nt(u_ref[...], xn_ref[...])

    i1_base = pl.multiple_of(j * PEER_I1_BLOCK, PEER_I1_BLOCK)
    for il in range(PEER_I1_BLOCK):
        es = slice(il * N_KEYS, (il + 1) * N_KEYS)
        for lt in range(tt // LANES):
            ln = slice(lt * LANES, (lt + 1) * LANES)
            g = jnp.zeros((N_KEYS, LANES), F32)
            for h in range(PEER_HEADS):
                hs = slice(h * N_KEYS, (h + 1) * N_KEYS)
                th = th_ref[pl.ds(h * N_KEYS + i1_base, PEER_I1_BLOCK), ln][il:il + 1, :]
                e1 = e1_ref[pl.ds(h * N_KEYS + i1_base, PEER_I1_BLOCK), ln][il:il + 1, :]
                e2 = e2_ref[hs, ln]
                g = g + jnp.where(e2 >= th, e2, 0.0) * e1
            w_s[es, ln] = (g * _gelu_tanh(hid_s[es, ln])).astype(BF16)
    acc_s[...] += _dot(vt_ref[...], w_s[...])

    @pl.when(j == pl.num_programs(1) - 1)
    def _():
        out_ref[...] = h_ref[...] + acc_s[...].T


def _peer(xn2, hn, th, e1, e2, u, vt, *, rows):
    tt = PEER_TOKENS
    eb = PEER_I1_BLOCK * N_KEYS
    half = PEER_HEADS * N_KEYS
    tok = pl.BlockSpec((tt, D_MODEL), lambda i, j: (i, 0))
    tab = pl.BlockSpec((half, tt), lambda i, j: (0, i))
    return pl.pallas_call(
        _peer_body,
        grid=(rows // tt, N_EXPERTS // eb),
        in_specs=[tok, tok, tab, tab, tab,
                  pl.BlockSpec((eb, D_MODEL), lambda i, j: (j, 0)),
                  pl.BlockSpec((D_MODEL, eb), lambda i, j: (0, j))],
        out_specs=pl.BlockSpec((tt, D_MODEL), lambda i, j: (i, 0)),
        out_shape=jax.ShapeDtypeStruct((rows, D_MODEL), F32),
        scratch_shapes=[pltpu.VMEM((D_MODEL, tt), F32), pltpu.VMEM((eb, tt), F32), pltpu.VMEM((eb, tt), BF16)],
        compiler_params=_cparams("parallel", "arbitrary"),
        name="peer_dense",
    )(xn2, hn, th, e1, e2, u, vt)


def _block_diag(w):
    eye = jnp.eye(RG_HEADS, dtype=w.dtype)
    return jnp.einsum("hij,hg->higj", w, eye).reshape(RG_WIDTH, RG_WIDTH)


def _head_pad(w, lo, hi):
    k = w.shape[0]
    part = w[:, :, lo:hi]
    return jnp.pad(part, ((0, 0), (0, 0), (0, HEAD_PAD - (hi - lo)))).reshape(k, N_HEADS * HEAD_PAD)


def _rope_partner(x1, x2):
    z64 = jnp.zeros(x1.shape[:-1] + (QK_NOPE,), x1.dtype)
    z32 = jnp.zeros(x1.shape[:-1] + (HEAD_PAD - QK_DIM,), x1.dtype)
    return jnp.concatenate([z64, x2, x1, z32], axis=-1)


def _layer_weights(i, norm1_g, w_in, conv_w, conv_b, rg_wa, rg_ba, rg_wi, rg_bi, rg_lambda, q_norm_g, w_uq,
                   kv_norm_g, w_ukv, q_head_g, k_head_g, out_g_rg, out_g_attn, w_out, norm2_g, peer_wq,
                   peer_subkeys, peer_u, peer_v):
    half = QK_ROPE // 2
    wq3 = w_uq[i].reshape(Q_LORA, N_HEADS, QK_DIM)
    wkv3 = w_ukv[i].reshape(KV_LORA, N_HEADS, QK_NOPE + V_DIM)
    eye = jnp.eye(QK_ROPE, dtype=F32)
    ek = jnp.concatenate([jnp.zeros((QK_ROPE, QK_NOPE), F32), eye,
                          jnp.zeros((QK_ROPE, HEAD_PAD - QK_DIM), F32)], axis=-1)
    eks = _rope_partner(eye[:, :half], eye[:, half:])
    place = lambda e: jnp.pad(jnp.tile(e, (1, N_HEADS)), ((0, LANES - QK_ROPE), (0, 0))).astype(BF16)

    def head_gain(g):
        main = jnp.pad(g, (0, HEAD_PAD - QK_DIM)).reshape(1, HEAD_PAD)
        swap = _rope_partner(g[QK_NOPE:QK_NOPE + half], g[QK_NOPE + half:]).reshape(1, HEAD_PAD)
        return main, swap

    gq, gqs = head_gain(q_head_g[i])
    gk, gks = head_gain(k_head_g[i])
    wout = w_out[i]
    wout_at = jnp.pad(wout[RG_WIDTH:].reshape(N_HEADS, V_DIM, D_MODEL),
                      ((0, 0), (0, HEAD_PAD - V_DIM), (0, 0))).reshape(N_HEADS * HEAD_PAD, D_MODEL)
    gat = jnp.pad(out_g_attn[i].reshape(N_HEADS, V_DIM), ((0, 0), (0, HEAD_PAD - V_DIM))).reshape(1, -1)
    return dict(
        g1=norm1_g[i].reshape(1, -1),
        win=jnp.pad(w_in[i], ((0, 0), (0, P_IN_PAD - P_IN))).astype(BF16),
        qg=q_norm_g[i].reshape(1, -1),
        wq=_head_pad(wq3, 0, QK_DIM).astype(BF16),
        wqs=_rope_partner(wq3[:, :, QK_NOPE:QK_NOPE + half], wq3[:, :, QK_NOPE + half:])
        .reshape(Q_LORA, -1).astype(BF16),
        kvg=kv_norm_g[i].reshape(1, -1),
        wk=_head_pad(wkv3, 0, QK_NOPE).astype(BF16),
        wv=_head_pad(wkv3, QK_NOPE, QK_NOPE + V_DIM).astype(BF16),
        ek=place(ek), eks=place(eks), gq=gq, gqs=gqs, gk=gk, gks=gks,
        conv_w=conv_w[i], conv_b=conv_b[i].reshape(1, -1),
        wgate_f=jnp.concatenate([_block_diag(rg_wa[i, 0]), _block_diag(rg_wi[i, 0])], axis=1).astype(BF16),
        wgate_b=jnp.concatenate([_block_diag(rg_wa[i, 1]), _block_diag(rg_wi[i, 1])], axis=1).astype(BF16),
        gate_bias=jnp.stack([rg_ba[i, 0].reshape(-1), rg_bi[i, 0].reshape(-1),
                             rg_ba[i, 1].reshape(-1), rg_bi[i, 1].reshape(-1)]),
        lam=rg_lambda[i],
        grg=out_g_rg[i].reshape(1, -1), gat=gat,
        wout_rg=wout[:RG_WIDTH].astype(BF16), wout_at=wout_at.astype(BF16),
        g2=norm2_g[i].reshape(1, -1),
        wqt=peer_wq[i].T.astype(BF16),
        subk=peer_subkeys[i].reshape(2 * PEER_HEADS, N_KEYS, PEER_HALF).astype(BF16),
        u=peer_u[i].astype(BF16),
        vt=peer_v[i].T.astype(BF16),
    )


def _rope_tables(length):
    pos = jnp.arange(length, dtype=F32)
    inv = ROPE_THETA ** (-jnp.arange(0, QK_ROPE, 2, dtype=F32) / QK_ROPE)
    ang = pos[:, None] * inv[None, :]
    cos, sin = jnp.cos(ang), jnp.sin(ang)
    ones = jnp.ones((length, QK_NOPE), F32)
    zpad = jnp.zeros((length, HEAD_PAD - QK_DIM), F32)
    cos_t = jnp.concatenate([ones, cos, cos, zpad], axis=-1)
    sin_t = jnp.concatenate([jnp.zeros((length, QK_NOPE), F32), -sin, sin, zpad], axis=-1)
    return cos_t, sin_t


def _padded_len(batch, length):
    lp = -(-length // ATTN_Q_TILE) * ATTN_Q_TILE
    row_tile = math.lcm(ROW_TILE, PEER_TOKENS, ROUTE_TOKENS)
    while (batch * lp) % row_tile:
        lp += ATTN_Q_TILE
    return lp


def _encode(x, meta_tokens, layers):
    batch, n_tok, _ = x.shape
    seq_len = n_tok + N_META
    lp = _padded_len(batch, seq_len)
    rows = batch * lp
    meta = jnp.broadcast_to(meta_tokens[None].astype(x.dtype), (batch, N_META, D_MODEL))
    h = jnp.concatenate([meta, x, jnp.zeros((batch, lp - seq_len, D_MODEL), x.dtype)], axis=1)
    h = h.reshape(rows, D_MODEL)
    cos_t, sin_t = (jnp.tile(t, (batch, 1)) for t in _rope_tables(lp))
    for w in layers:
        xr, gate, q, k, v = _inproj(h, cos_t, sin_t, w, rows=rows)
        hf, hb = _scan(xr, w, batch=batch, padded_len=lp, seq_len=seq_len)
        per_seq = (N_HEADS, batch, lp, HEAD_PAD)
        kt = k.reshape(per_seq).transpose(0, 1, 3, 2)
        o = _attention(q.reshape(per_seq), kt, v.reshape(per_seq), batch=batch, padded_len=lp, seq_len=seq_len)
        hn, xn2, sc = _outproj(hf, hb, gate, o.reshape(N_HEADS, rows, HEAD_PAD), h, w, rows=rows)
        th, e1, e2 = _route(sc, rows=rows)
        h = _peer(xn2, hn, th, e1, e2, w["u"], w["vt"], rows=rows)
    return h.reshape(batch, lp, D_MODEL)[:, N_META:seq_len]


def kernel(x_prompt, x_sample, meta_tokens, norm1_g, w_in, conv_w, conv_b, rg_wa, rg_ba, rg_wi, rg_bi, rg_lambda, q_norm_g, w_uq, kv_norm_g, w_ukv, q_head_g, k_head_g, out_g_rg, out_g_attn, w_out, norm2_g, peer_wq, peer_subkeys, peer_u, peer_v):
    params = (norm1_g, w_in, conv_w, conv_b, rg_wa, rg_ba, rg_wi, rg_bi, rg_lambda, q_norm_g, w_uq, kv_norm_g,
              w_ukv, q_head_g, k_head_g, out_g_rg, out_g_attn, w_out, norm2_g, peer_wq, peer_subkeys, peer_u,
              peer_v)
    layers = [_layer_weights(i, *params) for i in range(norm1_g.shape[0])]
    y_prompt = _encode(x_prompt, meta_tokens, layers)
    y_sample = _encode(x_sample, meta_tokens, layers)
    return (y_prompt, y_sample)
```

```python
import functools
import math

import jax
import jax.numpy as jnp
from jax import lax
from jax.experimental import pallas as pl
from jax.experimental.pallas import tpu as pltpu

F32 = jnp.float32
BF16 = jnp.bfloat16

D_MODEL = 1024
N_META = 16
EPS = 1e-6
RG_WIDTH = 512
RG_HEADS = 8
RG_BLOCK = 64
CONV_W = 4
RG_C = 8.0
N_HEADS = 8
QK_NOPE = 64
QK_ROPE = 32
QK_DIM = 96
V_DIM = 64
Q_LORA = 256
KV_LORA = 128
ROPE_THETA = 10000.0
P_IN = 2 * RG_WIDTH + Q_LORA + KV_LORA + QK_ROPE
PEER_HEADS = 8
N_KEYS = 128
N_EXPERTS = N_KEYS * N_KEYS
PEER_TOPK = 16
PEER_HALF = 128

LANES = 128
SUBLANES = 8
HEAD_PAD = LANES
P_IN_PAD = 12 * LANES
ROW_TILE = 512
ATTN_Q_TILE = 1056
SCAN_CHUNK = ATTN_Q_TILE // 4
KV_CHUNK = 1024
PEER_TOKENS = 512
ROUTE_TOKENS = 256
PEER_I1_BLOCK = 16
VMEM_LIMIT = 56 * 1024 * 1024
NEG_INF = float("-inf")
MASK_VALUE = -1e30
LOG2E = 1.4426950408889634


def _cparams(*sem):
    return pltpu.CompilerParams(dimension_semantics=sem, vmem_limit_bytes=VMEM_LIMIT)


def _rms(x, g):
    return x * lax.rsqrt(jnp.mean(x * x, axis=-1, keepdims=True) + EPS) * g


def _gelu_tanh(x):
    return x * (0.5 * (1.0 + jnp.tanh(0.7978845608028654 * (x + 0.044715 * (x * x * x)))))


def _sigmoid(x):
    return 1.0 / (1.0 + jnp.exp(-x))


def _neg_expm1(y):
    u = jnp.exp(y)
    um1 = u - 1.0
    tiny = um1 == 0.0
    r = jnp.where(tiny, y, um1 * y / jnp.where(tiny, 1.0, jnp.log(u)))
    return -jnp.where(um1 == -1.0, -1.0, r)


def _dot(a, b):
    return jnp.dot(a, b, preferred_element_type=F32)


def _dot_nt(a, b):
    return lax.dot_general(a, b, (((1,), (1,)), ((), ())), preferred_element_type=F32)


def _full(shape):
    n = len(shape)
    return pl.BlockSpec(shape, lambda *_: (0,) * n)


def _inproj_body(x_ref, cos_ref, sin_ref, g1_ref, win_ref, qg_ref, wq_ref, wqs_ref, kvg_ref,
                 wk_ref, ek_ref, eks_ref, wv_ref, gq_ref, gqs_ref, gk_ref, gks_ref,
                 xr_ref, gate_ref, q_ref, k_ref, v_ref):
    xn = _rms(x_ref[...], g1_ref[...])
    p = _dot(xn.astype(BF16), win_ref[...])
    xr_ref[...] = p[:, :RG_WIDTH]
    gate_ref[...] = p[:, RG_WIDTH:2 * RG_WIDTH]
    o2 = 2 * RG_WIDTH
    qcn = _rms(p[:, o2:o2 + Q_LORA], qg_ref[...]).astype(BF16)
    kvn = _rms(p[:, o2 + Q_LORA:o2 + Q_LORA + KV_LORA], kvg_ref[...]).astype(BF16)
    kpe = p[:, o2 + Q_LORA + KV_LORA:].astype(BF16)
    q = _dot(qcn, wq_ref[...])
    qs = _dot(qcn, wqs_ref[...])
    k = _dot(kvn, wk_ref[...]) + _dot(kpe, ek_ref[...])
    ks = _dot(kpe, eks_ref[...])
    v = _dot(kvn, wv_ref[...]).astype(BF16)
    c = cos_ref[...]
    s = sin_ref[...]
    qscale = QK_DIM ** -0.5 * LOG2E
    for h in range(N_HEADS):
        sl = slice(h * HEAD_PAD, (h + 1) * HEAD_PAD)
        qh = q[:, sl]
        rq = lax.rsqrt(jnp.sum(qh * qh, axis=-1, keepdims=True) * (1.0 / QK_DIM) + EPS)
        q_ref[h] = ((qh * gq_ref[...] * c + qs[:, sl] * gqs_ref[...] * s) * (rq * qscale)).astype(BF16)
        kh = k[:, sl]
        rk = lax.rsqrt(jnp.sum(kh * kh, axis=-1, keepdims=True) * (1.0 / QK_DIM) + EPS)
        k_ref[h] = ((kh * gk_ref[...] * c + ks[:, sl] * gks_ref[...] * s) * rk).astype(BF16)
        v_ref[h] = v[:, sl]


def _inproj(h, cos_t, sin_t, w, *, rows):
    tm = ROW_TILE
    row = lambda n: pl.BlockSpec((tm, n), lambda i: (i, 0))
    pos = row(HEAD_PAD)
    wide = N_HEADS * HEAD_PAD
    heads = pl.BlockSpec((N_HEADS, tm, HEAD_PAD), lambda i: (0, i, 0))
    heads_shape = jax.ShapeDtypeStruct((N_HEADS, rows, HEAD_PAD), BF16)
    return pl.pallas_call(
        _inproj_body,
        grid=(rows // tm,),
        in_specs=[row(D_MODEL), pos, pos, _full((1, D_MODEL)), _full((D_MODEL, P_IN_PAD)),
                  _full((1, Q_LORA)), _full((Q_LORA, wide)), _full((Q_LORA, wide)),
                  _full((1, KV_LORA)), _full((KV_LORA, wide)), _full((LANES, wide)), _full((LANES, wide)),
                  _full((KV_LORA, wide)),
                  _full((1, HEAD_PAD)), _full((1, HEAD_PAD)), _full((1, HEAD_PAD)), _full((1, HEAD_PAD))],
        out_specs=[row(RG_WIDTH), row(RG_WIDTH), heads, heads, heads],
        out_shape=[jax.ShapeDtypeStruct((rows, RG_WIDTH), F32), jax.ShapeDtypeStruct((rows, RG_WIDTH), F32),
                   heads_shape, heads_shape, heads_shape],
        compiler_params=_cparams("parallel"),
        name="inproj",
    )(h, cos_t, sin_t, w["g1"], w["win"], w["qg"], w["wq"], w["wqs"], w["kvg"], w["wk"], w["ek"], w["eks"],
      w["wv"], w["gq"], w["gqs"], w["gk"], w["gks"])


def _scan_body(xm_f, xp_f, xn_f, xm_b, xp_b, xn_b, cw_ref, cb_ref, wf_ref, wb_ref, bias_ref, lam_ref,
               hf_ref, hb_ref, ext_s, a_s, b_s, cf_s, cb_s, *, seq_len, chunk, n_chunks):
    i = pl.program_id(1)
    C = chunk
    groups = C // SUBLANES

    @pl.when(i == 0)
    def _():
        cf_s[...] = jnp.zeros_like(cf_s)
        cb_s[...] = jnp.zeros_like(cb_s)

    rows8 = lax.broadcasted_iota(jnp.int32, (SUBLANES, RG_WIDTH), 0)
    rows_c = lax.broadcasted_iota(jnp.int32, (C, RG_WIDTH), 0)
    rmod = rows_c & (SUBLANES - 1)

    def masked(x, pos):
        return jnp.where((pos >= 0) & (pos < seq_len), x, 0.0)

    def affine_terms(xm, xp, xn, j, w_ref, d):
        base = j * C
        ext_s[0:SUBLANES, :] = masked(xp[...], base - SUBLANES + rows8)
        ext_s[SUBLANES:SUBLANES + C, :] = masked(xm[...], base + rows_c)
        ext_s[SUBLANES + C:, :] = masked(xn[...], base + C + rows8)
        xc = cb_ref[...]
        for t in range(CONV_W):
            xc = xc + cw_ref[t:t + 1, :] * ext_s[SUBLANES - 2 + t:SUBLANES - 2 + t + C, :]
        gm = _dot(xc.astype(BF16), w_ref[...])
        r = _sigmoid(gm[:, :RG_WIDTH] + bias_ref[2 * d:2 * d + 1, :])
        ig = _sigmoid(gm[:, RG_WIDTH:] + bias_ref[2 * d + 1:2 * d + 2, :])
        z = -lam_ref[d:d + 1, :]
        softplus = jnp.maximum(z, 0.0) + jnp.log1p(jnp.exp(-jnp.abs(z)))
        log_a = (-RG_C) * r * softplus
        a = jnp.exp(log_a)
        b = jnp.sqrt(_neg_expm1(2.0 * log_a)) * (ig * xc)
        b = jnp.where(base + rows_c < seq_len, b, 0.0)
        return a, b

    def rotate_in_groups(x, shift):
        return pltpu.roll(x.reshape(groups, SUBLANES, RG_WIDTH), shift, 1).reshape(C, RG_WIDTH)

    a, b = affine_terms(xm_f, xp_f, xn_f, i, wf_ref, 0)
    for d in (1, 2, 4):
        m = rmod >= d
        a_sh = rotate_in_groups(a, d)
        b_sh = rotate_in_groups(b, d)
        b = jnp.where(m, a * b_sh + b, b)
        a = jnp.where(m, a * a_sh, a)
    a_s[...] = a
    b_s[...] = b

    def fwd_group(g, carry):
        r0 = pl.multiple_of(g * SUBLANES, SUBLANES)
        hrow = a_s[pl.ds(r0, SUBLANES), :] * carry + b_s[pl.ds(r0, SUBLANES), :]
        hf_ref[pl.ds(r0, SUBLANES), :] = hrow
        return jnp.broadcast_to(hrow[SUBLANES - 1:SUBLANES, :], (SUBLANES, RG_WIDTH))

    cf_s[...] = lax.fori_loop(0, groups, fwd_group, cf_s[...])

    a, b = affine_terms(xm_b, xp_b, xn_b, n_chunks - 1 - i, wb_ref, 1)
    for d in (1, 2, 4):
        m = rmod < SUBLANES - d
        a_sh = rotate_in_groups(a, SUBLANES - d)
        b_sh = rotate_in_groups(b, SUBLANES - d)
        b = jnp.where(m, a * b_sh + b, b)
        a = jnp.where(m, a * a_sh, a)
    a_s[...] = a
    b_s[...] = b

    def bwd_group(g, carry):
        r0 = pl.multiple_of((groups - 1 - g) * SUBLANES, SUBLANES)
        hrow = a_s[pl.ds(r0, SUBLANES), :] * carry + b_s[pl.ds(r0, SUBLANES), :]
        hb_ref[pl.ds(r0, SUBLANES), :] = hrow
        return jnp.broadcast_to(hrow[0:1, :], (SUBLANES, RG_WIDTH))

    cb_s[...] = lax.fori_loop(0, groups, bwd_group, cb_s[...])


def _scan(xr, w, *, batch, padded_len, seq_len):
    C = SCAN_CHUNK
    n = padded_len // C
    per8 = C // SUBLANES
    last8 = batch * padded_len // SUBLANES - 1

    def chunk_f(b, i):
        return b * n + i

    def chunk_b(b, i):
        return b * n + (n - 1 - i)

    def specs(chunk_of):
        main = pl.BlockSpec((C, RG_WIDTH), lambda b, i: (chunk_of(b, i), 0))
        prev = pl.BlockSpec((SUBLANES, RG_WIDTH), lambda b, i: (jnp.maximum(chunk_of(b, i) * per8 - 1, 0), 0))
        nxt = pl.BlockSpec((SUBLANES, RG_WIDTH),
                           lambda b, i: (jnp.minimum((chunk_of(b, i) + 1) * per8, last8), 0))
        return [main, prev, nxt]

    rows = batch * padded_len
    return pl.pallas_call(
        functools.partial(_scan_body, seq_len=seq_len, chunk=C, n_chunks=n),
        grid=(batch, n),
        in_specs=specs(chunk_f) + specs(chunk_b) + [
            _full((CONV_W, RG_WIDTH)), _full((1, RG_WIDTH)),
            _full((RG_WIDTH, 2 * RG_WIDTH)), _full((RG_WIDTH, 2 * RG_WIDTH)),
            _full((4, RG_WIDTH)), _full((2, RG_WIDTH))],
        out_specs=[pl.BlockSpec((C, RG_WIDTH), lambda b, i: (chunk_f(b, i), 0)),
                   pl.BlockSpec((C, RG_WIDTH), lambda b, i: (chunk_b(b, i), 0))],
        out_shape=[jax.ShapeDtypeStruct((rows, RG_WIDTH), F32), jax.ShapeDtypeStruct((rows, RG_WIDTH), F32)],
        scratch_shapes=[pltpu.VMEM((C + 2 * SUBLANES, RG_WIDTH), F32), pltpu.VMEM((C, RG_WIDTH), F32),
                        pltpu.VMEM((C, RG_WIDTH), F32), pltpu.VMEM((SUBLANES, RG_WIDTH), F32),
                        pltpu.VMEM((SUBLANES, RG_WIDTH), F32)],
        compiler_params=_cparams("arbitrary", "arbitrary"),
        name="rglru_scan",
    )(xr, xr, xr, xr, xr, xr, w["conv_w"], w["conv_b"], w["wgate_f"], w["wgate_b"], w["gate_bias"], w["lam"])


def _attn_body(q_ref, kt_ref, v_ref, o_ref, s_a, s_b, m_s, l_s, acc_s, *, seq_len, n_full, rem):
    tq = q_ref.shape[2]
    m_s[...] = jnp.full_like(m_s, NEG_INF)
    l_s[...] = jnp.zeros_like(l_s)
    acc_s[...] = jnp.zeros_like(acc_s)

    def scores(dst, start, width):
        dst[:, :width] = _dot(q_ref[0, 0], kt_ref[0, 0, :, pl.ds(start, width)])

    def update(src, start, width, mask):
        s = src[:, :width]
        if mask:
            kpos = start + lax.broadcasted_iota(jnp.int32, (tq, width), 1)
            s = jnp.where(kpos < seq_len, s, MASK_VALUE)
        m = m_s[...]
        m_new = jnp.maximum(m, jnp.max(s, axis=-1, keepdims=True))
        alpha = jnp.exp2(m - m_new)
        p = jnp.exp2(s - m_new)
        m_s[...] = m_new
        l_s[...] = alpha * l_s[...] + jnp.sum(p, axis=-1, keepdims=True)
        acc_s[...] = alpha * acc_s[...] + _dot(p.astype(BF16), v_ref[0, 0, pl.ds(start, width), :])

    if rem:
        scores(s_b, n_full * KV_CHUNK, rem)
    if n_full:
        scores(s_a, 0, KV_CHUNK)
    if rem:
        update(s_b, n_full * KV_CHUNK, rem, True)
    pairs = max((n_full - 1) // 2, 0)
    tail = [(c * KV_CHUNK, KV_CHUNK, False) for c in range(2 * pairs, n_full)]

    def pair(i, carry):
        c0 = pl.multiple_of(2 * i * KV_CHUNK, KV_CHUNK)
        scores(s_b, c0 + KV_CHUNK, KV_CHUNK)
        update(s_a, c0, KV_CHUNK, False)
        scores(s_a, c0 + 2 * KV_CHUNK, KV_CHUNK)
        update(s_b, c0 + KV_CHUNK, KV_CHUNK, False)
        return carry

    lax.fori_loop(0, pairs, pair, 0)
    bufs = (s_a, s_b)
    for n, (start, width, mask) in enumerate(tail):
        if n + 1 < len(tail):
            scores(bufs[(n + 1) % 2], tail[n + 1][0], tail[n + 1][1])
        update(bufs[n % 2], start, width, mask)
    o_ref[0, 0] = acc_s[...] / l_s[...]


def _attention(q, kt, v, *, batch, padded_len, seq_len):
    once = pl.Buffered(1)
    kv_bytes = 2 * padded_len * HEAD_PAD * 2
    tq = ATTN_Q_TILE
    if padded_len % (2 * tq) == 0 and 2 * tq * KV_CHUNK * 10 + kv_bytes <= VMEM_LIMIT * 2 // 3:
        tq = 2 * tq
    n_full = seq_len // KV_CHUNK
    rem = -(-(seq_len - n_full * KV_CHUNK) // LANES) * LANES
    q_tile = pl.BlockSpec((1, 1, tq, HEAD_PAD), lambda b, h, i: (h, b, i, 0))
    return pl.pallas_call(
        functools.partial(_attn_body, seq_len=seq_len, n_full=n_full, rem=rem),
        grid=(batch, N_HEADS, padded_len // tq),
        in_specs=[q_tile,
                  pl.BlockSpec((1, 1, HEAD_PAD, padded_len), lambda b, h, i: (h, b, 0, 0), pipeline_mode=once),
                  pl.BlockSpec((1, 1, padded_len, HEAD_PAD), lambda b, h, i: (h, b, 0, 0), pipeline_mode=once)],
        out_specs=q_tile,
        out_shape=jax.ShapeDtypeStruct((N_HEADS, batch, padded_len, HEAD_PAD), F32),
        scratch_shapes=[pltpu.VMEM((tq, KV_CHUNK), F32), pltpu.VMEM((tq, KV_CHUNK), F32),
                        pltpu.VMEM((tq, 1), F32), pltpu.VMEM((tq, 1), F32), pltpu.VMEM((tq, HEAD_PAD), F32)],
        compiler_params=_cparams("parallel", "parallel", "arbitrary"),
        name="attention",
    )(q, kt, v)


def _outproj_body(hf_ref, hb_ref, gate_ref, o_ref, h_ref, grg_ref, gat_ref, wrg_ref, wat_ref, g2_ref,
                  wqt_ref, sk_ref, hn_ref, xn2_ref, sc_ref):
    rg = (hf_ref[...] + hb_ref[...]) * _gelu_tanh(gate_ref[...])
    rgn = _rms(rg, grg_ref[...])
    o = jnp.concatenate([o_ref[hd] for hd in range(N_HEADS)], axis=1)
    ms = jnp.sum(o * o, axis=-1, keepdims=True) * (1.0 / (N_HEADS * V_DIM))
    attn = o * lax.rsqrt(ms + EPS) * gat_ref[...]
    hn = h_ref[...] + _dot(rgn.astype(BF16), wrg_ref[...]) + _dot(attn.astype(BF16), wat_ref[...])
    hn_ref[...] = hn
    xn2 = _rms(hn, g2_ref[...]).astype(BF16)
    xn2_ref[...] = xn2
    qpt = _dot_nt(wqt_ref[...], xn2)
    for h in range(PEER_HEADS):
        for s in range(2):
            r0 = (h * 2 + s) * PEER_HALF
            blk = qpt[r0:r0 + PEER_HALF, :].astype(BF16)
            o0 = s * PEER_HEADS * N_KEYS + h * N_KEYS
            sc_ref[o0:o0 + N_KEYS, :] = _dot(sk_ref[h * 2 + s], blk)


def _outproj(hf, hb, gate, o, h, w, *, rows):
    tm = ROW_TILE
    row = lambda n: pl.BlockSpec((tm, n), lambda i: (i, 0))
    wide = N_HEADS * HEAD_PAD
    nsc = 2 * PEER_HEADS * N_KEYS
    return pl.pallas_call(
        _outproj_body,
        grid=(rows // tm,),
        in_specs=[row(RG_WIDTH), row(RG_WIDTH), row(RG_WIDTH),
                  pl.BlockSpec((N_HEADS, tm, HEAD_PAD), lambda i: (0, i, 0)), row(D_MODEL),
                  _full((1, RG_WIDTH)), _full((1, wide)), _full((RG_WIDTH, D_MODEL)), _full((wide, D_MODEL)),
                  _full((1, D_MODEL)), _full((nsc, D_MODEL)), _full((2 * PEER_HEADS, N_KEYS, PEER_HALF))],
        out_specs=[row(D_MODEL), row(D_MODEL), pl.BlockSpec((nsc, tm), lambda i: (0, i))],
        out_shape=[jax.ShapeDtypeStruct((rows, D_MODEL), F32), jax.ShapeDtypeStruct((rows, D_MODEL), BF16),
                   jax.ShapeDtypeStruct((nsc, rows), F32)],
        compiler_params=_cparams("parallel"),
        name="outproj",
    )(hf, hb, gate, o, h, w["grg"], w["gat"], w["wout_rg"], w["wout_at"], w["g2"], w["wqt"], w["subk"])


def _sort_network(n):
    pairs, p = [], 1
    while p < n:
        k = p
        while k >= 1:
            for j in range(k % p, n - k, 2 * k):
                for i in range(min(k, n - j - k)):
                    if (i + j) // (2 * p) == (i + j + k) // (2 * p):
                        pairs.append((i + j, i + j + k))
            k //= 2
        p *= 2
    return pairs


def _top_rows(x, count):
    n = x.shape[0] // SUBLANES
    cols = [x[k * SUBLANES:(k + 1) * SUBLANES, :] for k in range(n)]
    for a, b in _sort_network(n):
        cols[a], cols[b] = jnp.maximum(cols[a], cols[b]), jnp.minimum(cols[a], cols[b])
    cols.append(jnp.full_like(cols[0], NEG_INF))
    rows = []
    for r in range(count):
        m = jnp.max(cols[0], axis=0, keepdims=True)
        rows.append(m)
        depth = min(n, count - 1 - r)
        hit = cols[0] == m
        for k in range(depth):
            cols[k] = jnp.where(hit, cols[k + 1], cols[k])
    return rows


def _route_body(sc_ref, th_ref, e1_ref, e2_ref):
    t = sc_ref.shape[1]
    half = PEER_HEADS * N_KEYS
    jrow = lax.broadcasted_iota(jnp.int32, (SUBLANES, t), 0)

    def bc(row):
        return jnp.broadcast_to(row, (SUBLANES, t))

    def head(h, carry):
        r0 = pl.multiple_of(h * N_KEYS, N_KEYS)
        s1 = sc_ref[pl.ds(r0, N_KEYS), :]
        s2 = sc_ref[pl.ds(half + r0, N_KEYS), :]
        a = _top_rows(s1, PEER_TOPK + 1)
        b = _top_rows(s2, PEER_TOPK + 1)
        b_lo = jnp.concatenate(b[:SUBLANES], axis=0)
        b_hi = jnp.concatenate(b[SUBLANES:2 * SUBLANES], axis=0)
        a_hi = jnp.concatenate(a[SUBLANES:2 * SUBLANES], axis=0)
        pieces = [bc(a[0]) + b_lo, bc(a[0]) + b_hi, bc(a[1]) + b_lo]
        for i in range(2, SUBLANES):
            pieces.append(jnp.where(jrow < (PEER_TOPK + 1) // (i + 1), bc(a[i]) + b_lo, NEG_INF))
        pieces.append(a_hi + bc(b[0]))
        ends = jnp.concatenate([a[0] + b[PEER_TOPK], a[PEER_TOPK] + b[0]]
                               + [jnp.full((1, t), NEG_INF, F32)] * (SUBLANES - 2), axis=0)
        pieces.append(ends)
        cand = jnp.concatenate(pieces, axis=0)
        x = cand
        for _ in range(PEER_TOPK - 1):
            x = jnp.where(x == jnp.max(x, axis=0, keepdims=True), NEG_INF, x)
        c16 = jnp.max(x, axis=0, keepdims=True)
        c17 = jnp.max(jnp.where(x == c16, NEG_INF, x), axis=0, keepdims=True)
        thr = 0.5 * (c16 + c17)
        top = a[0] + b[0]
        z = jnp.sum(jnp.where(cand >= thr, jnp.exp(cand - top), 0.0), axis=0, keepdims=True)
        th_ref[pl.ds(r0, N_KEYS), :] = jnp.exp((thr - s1) - b[0])
        e1_ref[pl.ds(r0, N_KEYS), :] = jnp.exp(s1 - a[0]) / z
        e2_ref[pl.ds(r0, N_KEYS), :] = jnp.exp(s2 - b[0])
        return carry

    lax.fori_loop(0, PEER_HEADS, head, 0)


def _route(sc, *, rows):
    t = ROUTE_TOKENS
    half = PEER_HEADS * N_KEYS
    out = pl.BlockSpec((half, t), lambda i: (0, i))
    return pl.pallas_call(
        _route_body,
        grid=(rows // t,),
        in_specs=[pl.BlockSpec((2 * half, t), lambda i: (0, i))],
        out_specs=[out, out, out],
        out_shape=[jax.ShapeDtypeStruct((half, rows), F32)] * 3,
        compiler_params=_cparams("parallel"),
        name="peer_route",
    )(sc)


def _peer_body(xn_ref, h_ref, th_ref, e1_ref, e2_ref, u_ref, vt_ref, out_ref, acc_s, hid_s, w_s):
    j = pl.program_id(1)
    tt = xn_ref.shape[0]

    @pl.when(j == 0)
    def _():
        acc_s[...] = jnp.zeros_like(acc_s)

    hid_s[...] = _dot_nt(u_ref[...], xn_ref[...])

    i1_base = pl.multiple_of(j * PEER_I1_BLOCK, PEER_I1_BLOCK)
    for il in range(PEER_I1_BLOCK):
        es = slice(il * N_KEYS, (il + 1) * N_KEYS)
        for lt in range(tt // LANES):
            ln = slice(lt * LANES, (lt + 1) * LANES)
            g = jnp.zeros((N_KEYS, LANES), F32)
            for h in range(PEER_HEADS):
                hs = slice(h * N_KEYS, (h + 1) * N_KEYS)
                th = th_ref[pl.ds(h * N_KEYS + i1_base, PEER_I1_BLOCK), ln][il:il + 1, :]
                e1 = e1_ref[pl.ds(h * N_KEYS + i1_base, PEER_I1_BLOCK), ln][il:il + 1, :]
                e2 = e2_ref[hs, ln]
                g = g + jnp.where(e2 >= th, e2, 0.0) * e1
            w_s[es, ln] = (g * _gelu_tanh(hid_s[es, ln])).astype(BF16)
    acc_s[...] += _dot(vt_ref[...], w_s[...])

    @pl.when(j == pl.num_programs(1) - 1)
    def _():
        out_ref[...] = h_ref[...] + acc_s[...].T


def _peer(xn2, hn, th, e1, e2, u, vt, *, rows):
    tt = PEER_TOKENS
    eb = PEER_I1_BLOCK * N_KEYS
    half = PEER_HEADS * N_KEYS
    tok = pl.BlockSpec((tt, D_MODEL), lambda i, j: (i, 0))
    tab = pl.BlockSpec((half, tt), lambda i, j: (0, i))
    return pl.pallas_call(
        _peer_body,
        grid=(rows // tt, N_EXPERTS // eb),
        in_specs=[tok, tok, tab, tab, tab,
                  pl.BlockSpec((eb, D_MODEL), lambda i, j: (j, 0)),
                  pl.BlockSpec((D_MODEL, eb), lambda i, j: (0, j))],
        out_specs=pl.BlockSpec((tt, D_MODEL), lambda i, j: (i, 0)),
        out_shape=jax.ShapeDtypeStruct((rows, D_MODEL), F32),
        scratch_shapes=[pltpu.VMEM((D_MODEL, tt), F32), pltpu.VMEM((eb, tt), F32), pltpu.VMEM((eb, tt), BF16)],
        compiler_params=_cparams("parallel", "arbitrary"),
        name="peer_dense",
    )(xn2, hn, th, e1, e2, u, vt)


def _block_diag(w):
    eye = jnp.eye(RG_HEADS, dtype=w.dtype)
    return jnp.einsum("hij,hg->higj", w, eye).reshape(RG_WIDTH, RG_WIDTH)


def _head_pad(w, lo, hi):
    k = w.shape[0]
    part = w[:, :, lo:hi]
    return jnp.pad(part, ((0, 0), (0, 0), (0, HEAD_PAD - (hi - lo)))).reshape(k, N_HEADS * HEAD_PAD)


def _rope_partner(x1, x2):
    z64 = jnp.zeros(x1.shape[:-1] + (QK_NOPE,), x1.dtype)
    z32 = jnp.zeros(x1.shape[:-1] + (HEAD_PAD - QK_DIM,), x1.dtype)
    return jnp.concatenate([z64, x2, x1, z32], axis=-1)


def _layer_weights(i, norm1_g, w_in, conv_w, conv_b, rg_wa, rg_ba, rg_wi, rg_bi, rg_lambda, q_norm_g, w_uq,
                   kv_norm_g, w_ukv, q_head_g, k_head_g, out_g_rg, out_g_attn, w_out, norm2_g, peer_wq,
                   peer_subkeys, peer_u, peer_v):
    half = QK_ROPE // 2
    wq3 = w_uq[i].reshape(Q_LORA, N_HEADS, QK_DIM)
    wkv3 = w_ukv[i].reshape(KV_LORA, N_HEADS, QK_NOPE + V_DIM)
    eye = jnp.eye(QK_ROPE, dtype=F32)
    ek = jnp.concatenate([jnp.zeros((QK_ROPE, QK_NOPE), F32), eye,
                          jnp.zeros((QK_ROPE, HEAD_PAD - QK_DIM), F32)], axis=-1)
    eks = _rope_partner(eye[:, :half], eye[:, half:])
    place = lambda e: jnp.pad(jnp.tile(e, (1, N_HEADS)), ((0, LANES - QK_ROPE), (0, 0))).astype(BF16)

    def head_gain(g):
        main = jnp.pad(g, (0, HEAD_PAD - QK_DIM)).reshape(1, HEAD_PAD)
        swap = _rope_partner(g[QK_NOPE:QK_NOPE + half], g[QK_NOPE + half:]).reshape(1, HEAD_PAD)
        return main, swap

    gq, gqs = head_gain(q_head_g[i])
    gk, gks = head_gain(k_head_g[i])
    wout = w_out[i]
    wout_at = jnp.pad(wout[RG_WIDTH:].reshape(N_HEADS, V_DIM, D_MODEL),
                      ((0, 0), (0, HEAD_PAD - V_DIM), (0, 0))).reshape(N_HEADS * HEAD_PAD, D_MODEL)
    gat = jnp.pad(out_g_attn[i].reshape(N_HEADS, V_DIM), ((0, 0), (0, HEAD_PAD - V_DIM))).reshape(1, -1)
    return dict(
        g1=norm1_g[i].reshape(1, -1),
        win=jnp.pad(w_in[i], ((0, 0), (0, P_IN_PAD - P_IN))).astype(BF16),
        qg=q_norm_g[i].reshape(1, -1),
        wq=_head_pad(wq3, 0, QK_DIM).astype(BF16),
        wqs=_rope_partner(wq3[:, :, QK_NOPE:QK_NOPE + half], wq3[:, :, QK_NOPE + half:])
        .reshape(Q_LORA, -1).astype(BF16),
        kvg=kv_norm_g[i].reshape(1, -1),
        wk=_head_pad(wkv3, 0, QK_NOPE).astype(BF16),
        wv=_head_pad(wkv3, QK_NOPE, QK_NOPE + V_DIM).astype(BF16),
        ek=place(ek), eks=place(eks), gq=gq, gqs=gqs, gk=gk, gks=gks,
        conv_w=conv_w[i], conv_b=conv_b[i].reshape(1, -1),
        wgate_f=jnp.concatenate([_block_diag(rg_wa[i, 0]), _block_diag(rg_wi[i, 0])], axis=1).astype(BF16),
        wgate_b=jnp.concatenate([_block_diag(rg_wa[i, 1]), _block_diag(rg_wi[i, 1])], axis=1).astype(BF16),
        gate_bias=jnp.stack([rg_ba[i, 0].reshape(-1), rg_bi[i, 0].reshape(-1),
                             rg_ba[i, 1].reshape(-1), rg_bi[i, 1].reshape(-1)]),
        lam=rg_lambda[i],
        grg=out_g_rg[i].reshape(1, -1), gat=gat,
        wout_rg=wout[:RG_WIDTH].astype(BF16), wout_at=wout_at.astype(BF16),
        g2=norm2_g[i].reshape(1, -1),
        wqt=peer_wq[i].T.astype(BF16),
        subk=peer_subkeys[i].reshape(2 * PEER_HEADS, N_KEYS, PEER_HALF).astype(BF16),
        u=peer_u[i].astype(BF16),
        vt=peer_v[i].T.astype(BF16),
    )


def _rope_tables(length):
    pos = jnp.arange(length, dtype=F32)
    inv = ROPE_THETA ** (-jnp.arange(0, QK_ROPE, 2, dtype=F32) / QK_ROPE)
    ang = pos[:, None] * inv[None, :]
    cos, sin = jnp.cos(ang), jnp.sin(ang)
    ones = jnp.ones((length, QK_NOPE), F32)
    zpad = jnp.zeros((length, HEAD_PAD - QK_DIM), F32)
    cos_t = jnp.concatenate([ones, cos, cos, zpad], axis=-1)
    sin_t = jnp.concatenate([jnp.zeros((length, QK_NOPE), F32), -sin, sin, zpad], axis=-1)
    return cos_t, sin_t


def _padded_len(batch, length):
    lp = -(-length // ATTN_Q_TILE) * ATTN_Q_TILE
    row_tile = math.lcm(ROW_TILE, PEER_TOKENS, ROUTE_TOKENS)
    while (batch * lp) % row_tile:
        lp += ATTN_Q_TILE
    return lp


def _encode(x, meta_tokens, layers):
    batch, n_tok, _ = x.shape
    seq_len = n_tok + N_META
    lp = _padded_len(batch, seq_len)
    rows = batch * lp
    meta = jnp.broadcast_to(meta_tokens[None].astype(x.dtype), (batch, N_META, D_MODEL))
    h = jnp.concatenate([meta, x, jnp.zeros((batch, lp - seq_len, D_MODEL), x.dtype)], axis=1)
    h = h.reshape(rows, D_MODEL)
    cos_t, sin_t = (jnp.tile(t, (batch, 1)) for t in _rope_tables(lp))
    for w in layers:
        xr, gate, q, k, v = _inproj(h, cos_t, sin_t, w, rows=rows)
        hf, hb = _scan(xr, w, batch=batch, padded_len=lp, seq_len=seq_len)
        per_seq = (N_HEADS, batch, lp, HEAD_PAD)
        kt = k.reshape(per_seq).transpose(0, 1, 3, 2)
        o = _attention(q.reshape(per_seq), kt, v.reshape(per_seq), batch=batch, padded_len=lp, seq_len=seq_len)
        hn, xn2, sc = _outproj(hf, hb, gate, o.reshape(N_HEADS, rows, HEAD_PAD), h, w, rows=rows)
        th, e1, e2 = _route(sc, rows=rows)
        h = _peer(xn2, hn, th, e1, e2, w["u"], w["vt"], rows=rows)
    return h.reshape(batch, lp, D_MODEL)[:, N_META:seq_len]


def kernel(x_prompt, x_sample, meta_tokens, norm1_g, w_in, conv_w, conv_b, rg_wa, rg_ba, rg_wi, rg_bi, rg_lambda, q_norm_g, w_uq, kv_norm_g, w_ukv, q_head_g, k_head_g, out_g_rg, out_g_attn, w_out, norm2_g, peer_wq, peer_subkeys, peer_u, peer_v):
    params = (norm1_g, w_in, conv_w, conv_b, rg_wa, rg_ba, rg_wi, rg_bi, rg_lambda, q_norm_g, w_uq, kv_norm_g,
              w_ukv, q_head_g, k_head_g, out_g_rg, out_g_attn, w_out, norm2_g, peer_wq, peer_subkeys, peer_u,
              peer_v)
    layers = [_layer_weights(i, *params) for i in range(norm1_g.shape[0])]
    y_prompt = _encode(x_prompt, meta_tokens, layers)
    y_sample = _encode(x_sample, meta_tokens, layers)
    return (y_prompt, y_sample)
```
